```python
import math
import jax
import jax.numpy as jnp
from jax import lax
import numpy as np

D_MODEL = 1024
BATCH = 8
SEQ = 2048
DEPTH = 4

N_MIXERS = 3
EPS = 1e-6
CONV_WIDTH = 3
E_CONV = 2 * D_MODEL
HEAD_DIM = 64
ATT_HEADS = D_MODEL // HEAD_DIM
E_ATT = ATT_HEADS * HEAD_DIM
DILATION_PATTERNS = ((128, 1), (512, 4), (2048, 16))
N_DIL_GROUPS = len(DILATION_PATTERNS)
ATT_BLOCK = 128
REL_BUCKETS = 32
REL_MAX_DIST = 2048
NEG_INF = -1e30
E_SSM = D_MODEL
SSM_GROUP = 16
SSM_GROUPS = E_SSM // SSM_GROUP
SSM_STATE = 64
DT_MIN = 0.001
DT_MAX = 0.1

kernel_name = 'hybrid_conv_dilattn_s5_adaln_trunk'


def rmsnorm(x, g=None):
    xf = x.astype(jnp.float32)
    y = xf * lax.rsqrt(jnp.mean(xf * xf, axis=-1, keepdims=True) + EPS)
    if g is not None:
        y = y * g.astype(jnp.float32)
    return y.astype(x.dtype)


def ada_modulation(c, w, b):
    mod = jax.nn.silu(c) @ w + b
    shift, scale, gate = jnp.split(mod, 3, axis=-1)
    return shift[:, None, :], scale[:, None, :], gate[:, None, :]


def short_conv_mixer(h, w_in, conv_w, conv_b, w_out):
    u, gc, gb, z = jnp.split(h @ w_in, 4, axis=-1)
    v = gc * u
    conv = lax.conv_general_dilated(
        v, conv_w[:, None, :], window_strides=(1,), padding=[(CONV_WIDTH - 1, 0)],
        dimension_numbers=('NWC', 'WIO', 'NWC'), feature_group_count=v.shape[-1])
    y = gb * (conv + conv_b) * jax.nn.silu(z)
    return y @ w_out


def t5_bucket(dist):
    max_exact = REL_BUCKETS // 2
    d = np.maximum(dist, 0)
    ratio = np.log(np.maximum(d, 1) / max_exact) / math.log(REL_MAX_DIST / max_exact)
    large = np.minimum(max_exact + (ratio * (REL_BUCKETS - max_exact)).astype(np.int64), REL_BUCKETS - 1)
    return np.where(d < max_exact, d, large).astype(np.int32)


def dilated_window_attention(q, k, v, bias_h, window, dil):
    B, S, H, Dh = q.shape
    L = S // dil
    wsub = window // dil
    assert wsub <= ATT_BLOCK
    nb = -(-L // ATT_BLOCK)
    Lp = nb * ATT_BLOCK

    def to_sub(a):
        a = a.reshape(B, L, dil, H, Dh).transpose(0, 2, 1, 3, 4)
        a = jnp.pad(a, ((0, 0), (0, 0), (0, Lp - L), (0, 0), (0, 0)))
        return a.reshape(B, dil, nb, ATT_BLOCK, H, Dh)

    def with_prev(a):
        prev = jnp.pad(a[:, :, :-1], ((0, 0), (0, 0), (1, 0), (0, 0), (0, 0), (0, 0)))
        return jnp.concatenate([prev, a], axis=3)

    qb = to_sub(q)
    kw = with_prev(to_sub(k))
    vw = with_prev(to_sub(v))
    s = jnp.einsum('brnqhd,brnkhd->brnhqk', qb, kw).astype(jnp.float32) / math.sqrt(Dh)

    qi = np.arange(ATT_BLOCK)[:, None]
    ki = np.arange(2 * ATT_BLOCK)[None, :]
    dist = ATT_BLOCK + qi - ki
    in_band = (dist >= 0) & (dist <= wsub)
    first = in_band & (ki >= ATT_BLOCK)
    valid = np.concatenate([first[None], np.broadcast_to(in_band, (nb - 1,) + in_band.shape)], axis=0)
    bucket = t5_bucket(dist * dil)
    bias = jnp.transpose(bias_h[bucket], (2, 0, 1)).astype(jnp.float32)

    s = jnp.where(valid[None, None, :, None], s + bias, NEG_INF)
    m = jnp.max(s, axis=-1, keepdims=True)
    p = jnp.exp(s - m)
    den = jnp.sum(p, axis=-1)
    o = jnp.einsum('brnhqk,brnkhd->brnqhd', p.astype(vw.dtype), vw)
    o = o / jnp.swapaxes(den, -1, -2)[..., None]
    lse = jnp.swapaxes(m[..., 0] + jnp.log(den), -1, -2)

    o = o.reshape(B, dil, Lp, H, Dh)[:, :, :L].transpose(0, 2, 1, 3, 4).reshape(B, S, H, Dh)
    lse = lse.reshape(B, dil, Lp, H)[:, :, :L].transpose(0, 2, 1, 3).reshape(B, S, H)
    return o, lse


def dilated_attention_mixer(h, w_in, w_out, rel_bias):
    B, S, _ = h.shape
    proj = h @ w_in
    n_qkv = 3 * N_DIL_GROUPS * E_ATT
    qkv = proj[..., :n_qkv].reshape(B, S, N_DIL_GROUPS, 3, ATT_HEADS, HEAD_DIM)
    z = proj[..., n_qkv:]
    outs = []
    lses = []
    for g, (window, dil) in enumerate(DILATION_PATTERNS):
        o, lse = dilated_window_attention(
            qkv[:, :, g, 0], qkv[:, :, g, 1], qkv[:, :, g, 2],
            rel_bias[:, g * ATT_HEADS:(g + 1) * ATT_HEADS], window, dil)
        outs.append(o)
        lses.append(lse)
    wgt = jax.nn.softmax(jnp.stack(lses, axis=0), axis=0)
    o = jnp.einsum('gbsh,gbshd->bshd', wgt, jnp.stack(outs, axis=0).astype(jnp.float32))
    y = o.reshape(B, S, E_ATT).astype(h.dtype) * jax.nn.silu(z)
    return y @ w_out


def s5_mixer(h, w_in, log_dt, lambda_re, lambda_im, b_re, b_im, c_re, c_im, d_skip, w_glu, b_glu, w_out):
    B, S, _ = h.shape
    u, z = jnp.split(h @ w_in, 2, axis=-1)
    uf = u.astype(jnp.float32)
    ug = uf.reshape(B, S, SSM_GROUPS, SSM_GROUP)
    dt = jnp.exp(log_dt.astype(jnp.float32))[:, None]
    lr = lambda_re.astype(jnp.float32)
    li = lambda_im.astype(jnp.float32)
    mag = jnp.exp(lr * dt)
    ar = mag * jnp.cos(li * dt)
    ai = mag * jnp.sin(li * dt)
    inv = 1.0 / (lr * lr + li * li)
    qr = ((ar - 1.0) * lr + ai * li) * inv
    qi = (ai * lr - (ar - 1.0) * li) * inv
    br = b_re.astype(jnp.float32)
    bi = b_im.astype(jnp.float32)
    bbr = qr[..., None] * br - qi[..., None] * bi
    bbi = qr[..., None] * bi + qi[..., None] * br
    xr = jnp.einsum('bsgk,gpk->bsgp', ug, bbr)
    xi = jnp.einsum('bsgk,gpk->bsgp', ug, bbi)
    a_r = jnp.broadcast_to(ar[None, None], (1, S, SSM_GROUPS, SSM_STATE))
    a_i = jnp.broadcast_to(ai[None, None], (1, S, SSM_GROUPS, SSM_STATE))

    def combine(e1, e2):
        a1r, a1i, b1r, b1i = e1
        a2r, a2i, b2r, b2i = e2
        return (a2r * a1r - a2i * a1i, a2r * a1i + a2i * a1r,
                a2r * b1r - a2i * b1i + b2r, a2r * b1i + a2i * b1r + b2i)

    _, _, sr, si = lax.associative_scan(combine, (a_r, a_i, xr, xi), axis=1)
    y = (jnp.einsum('bsgp,gkp->bsgk', sr, c_re.astype(jnp.float32))
         - jnp.einsum('bsgp,gkp->bsgk', si, c_im.astype(jnp.float32)))
    y = y.reshape(B, S, E_SSM) + d_skip.astype(jnp.float32) * uf
    g = jax.nn.gelu(y)
    y = g * jax.nn.sigmoid(g @ w_glu.astype(jnp.float32) + b_glu.astype(jnp.float32))
    y = y.astype(h.dtype) * jax.nn.silu(z)
    return y @ w_out


def setup_inputs(seed: int = 0) -> dict:
    key = jax.random.key(seed)
    ks = iter(jax.random.split(key, 64))

    def nrm(shape, scale):
        return jax.random.normal(next(ks), shape, jnp.float32) * scale

    D = D_MODEL
    inp = {}
    inp['x'] = nrm((BATCH, SEQ, D), 1.0)
    inp['c'] = nrm((BATCH, D), 1.0)
    inp['rel_bias'] = nrm((REL_BUCKETS, N_DIL_GROUPS * ATT_HEADS), 0.1)
    inp['final_g'] = 1.0 + nrm((D,), 0.01)

    def ada(prefix):
        inp[prefix + 'ada_w'] = nrm((D, 3 * D), 0.5 * D ** -0.5)
        inp[prefix + 'ada_b'] = nrm((3 * D,), 0.01)

    def conv_layer(prefix):
        ada(prefix)
        inp[prefix + 'w_in'] = nrm((D, 4 * E_CONV), D ** -0.5)
        inp[prefix + 'conv_w'] = nrm((CONV_WIDTH, E_CONV), CONV_WIDTH ** -0.5)
        inp[prefix + 'conv_b'] = nrm((E_CONV,), 0.01)
        inp[prefix + 'w_out'] = nrm((E_CONV, D), E_CONV ** -0.5)

    conv_layer('l0_')
    ada('l1_')
    inp['l1_w_in'] = nrm((D, (3 * N_DIL_GROUPS + 1) * E_ATT), D ** -0.5)
    inp['l1_w_out'] = nrm((E_ATT, D), E_ATT ** -0.5)
    ada('l2_')
    inp['l2_w_in'] = nrm((D, 2 * E_SSM), D ** -0.5)
    inp['l2_log_dt'] = jax.random.uniform(next(ks), (SSM_GROUPS,), jnp.float32, math.log(DT_MIN), math.log(DT_MAX))
    inp['l2_lambda_re'] = -0.5 + nrm((SSM_GROUPS, SSM_STATE), 0.01)
    inp['l2_lambda_im'] = math.pi * jnp.arange(SSM_STATE, dtype=jnp.float32)[None, :] + nrm((SSM_GROUPS, SSM_STATE), 0.01)
    inp['l2_b_re'] = nrm((SSM_GROUPS, SSM_STATE, SSM_GROUP), (2 * SSM_GROUP) ** -0.5)
    inp['l2_b_im'] = nrm((SSM_GROUPS, SSM_STATE, SSM_GROUP), (2 * SSM_GROUP) ** -0.5)
    inp['l2_c_re'] = nrm((SSM_GROUPS, SSM_GROUP, SSM_STATE), SSM_STATE ** -0.5)
    inp['l2_c_im'] = nrm((SSM_GROUPS, SSM_GROUP, SSM_STATE), SSM_STATE ** -0.5)
    inp['l2_d_skip'] = nrm((E_SSM,), 1.0)
    inp['l2_w_glu'] = nrm((E_SSM, E_SSM), E_SSM ** -0.5)
    inp['l2_b_glu'] = nrm((E_SSM,), 0.01)
    inp['l2_w_out'] = nrm((E_SSM, D), E_SSM ** -0.5)
    conv_layer('l3_')
    return inp


def reference(x, c, rel_bias, final_g,
              l0_ada_w, l0_ada_b, l0_w_in, l0_conv_w, l0_conv_b, l0_w_out,
              l1_ada_w, l1_ada_b, l1_w_in, l1_w_out,
              l2_ada_w, l2_ada_b, l2_w_in, l2_log_dt, l2_lambda_re, l2_lambda_im,
              l2_b_re, l2_b_im, l2_c_re, l2_c_im, l2_d_skip, l2_w_glu, l2_b_glu, l2_w_out,
              l3_ada_w, l3_ada_b, l3_w_in, l3_conv_w, l3_conv_b, l3_w_out):
    mixers = (short_conv_mixer, dilated_attention_mixer, s5_mixer)
    layers = (
        (l0_ada_w, l0_ada_b, (l0_w_in, l0_conv_w, l0_conv_b, l0_w_out)),
        (l1_ada_w, l1_ada_b, (l1_w_in, l1_w_out, rel_bias)),
        (l2_ada_w, l2_ada_b, (l2_w_in, l2_log_dt, l2_lambda_re, l2_lambda_im, l2_b_re, l2_b_im,
                              l2_c_re, l2_c_im, l2_d_skip, l2_w_glu, l2_b_glu, l2_w_out)),
        (l3_ada_w, l3_ada_b, (l3_w_in, l3_conv_w, l3_conv_b, l3_w_out)),
    )
    for i in range(DEPTH):
        ada_w, ada_b, mix_params = layers[i]
        shift, scale, gate = ada_modulation(c, ada_w, ada_b)
        h = rmsnorm(x) * (1.0 + scale) + shift
        x = x + gate * mixers[i % N_MIXERS](h, *mix_params)
    return rmsnorm(x, final_g)
```

```python
import functools
import math

import numpy as np
import jax
import jax.numpy as jnp
from jax import lax
from jax.experimental import pallas as pl
from jax.experimental.pallas import tpu as pltpu

EPS = 1e-6
HEAD_DIM = 64
ATT_BLOCK = 128
DILATION_PATTERNS = ((128, 1), (512, 4), (2048, 16))
REL_BUCKETS = 32
REL_MAX_DIST = 2048
NEG_INF = -1e30
SSM_GROUP = 16
SSM_STATE = 64
LANES = 128
SUBLANES = 8
VMEM_LIMIT_BYTES = 56 * 1024 * 1024

F32 = jnp.float32
BF16 = jnp.bfloat16


def _cparams(semantics):
    return pltpu.CompilerParams(dimension_semantics=semantics, vmem_limit_bytes=VMEM_LIMIT_BYTES)


def _resident(shape):
    nd = len(shape)
    return pl.BlockSpec(shape, lambda *_: (0,) * nd, pipeline_mode=pl.Buffered(1))


def _silu(z):
    return z * (1.0 / (1.0 + jnp.exp(-z)))


def _sigmoid(z):
    return 1.0 / (1.0 + jnp.exp(-z))


def _mod_norm(x, scale, shift):
    ms = jnp.mean(x * x, axis=-1, keepdims=True)
    return (x * lax.rsqrt(ms + EPS)) * (1.0 + scale) + shift


def _dot(a, b):
    return jnp.dot(a, b, preferred_element_type=F32)


def _modulation_kernel(c_ref, w0, w1, w2, w3, b0, b1, b2, b3, o0, o1, o2, o3):
    sc = _silu(c_ref[...]).astype(BF16)
    for w, b, o in ((w0, b0, o0), (w1, b1, o1), (w2, b2, o2), (w3, b3, o3)):
        o[...] = _dot(sc, w[...].astype(BF16)) + b[...]


def _modulation(c, ws, bs, tn=512):
    B, D = c.shape
    N = ws[0].shape[1]
    w_spec = pl.BlockSpec((D, tn), lambda j: (0, j))
    b_spec = pl.BlockSpec((1, tn), lambda j: (0, j))
    o_spec = pl.BlockSpec((B, tn), lambda j: (0, j))
    return pl.pallas_call(
        _modulation_kernel,
        grid=(N // tn,),
        in_specs=[pl.BlockSpec((B, D), lambda j: (0, 0))] + [w_spec] * 4 + [b_spec] * 4,
        out_specs=[o_spec] * 4,
        out_shape=[jax.ShapeDtypeStruct((B, N), F32)] * 4,
        compiler_params=_cparams(("arbitrary",)),
        name="modulation",
    )(c, *ws, *[b.reshape(1, N) for b in bs])


def _conv_layer_kernel(*refs, ec, final):
    if final:
        (x_ref, shift_ref, scale_ref, gate_ref, w_in_ref, cw_ref, cb_ref, w_out_ref, g_ref,
         o_ref, carry_ref, vs_ref) = refs
    else:
        (x_ref, shift_ref, scale_ref, gate_ref, w_in_ref, cw_ref, cb_ref, w_out_ref,
         o_ref, carry_ref, vs_ref) = refs
    tm, D = x_ref.shape
    E = w_out_ref.shape[0]

    @pl.when(pl.program_id(1) == 0)
    def _():
        carry_ref[...] = jnp.zeros_like(carry_ref)

    x = x_ref[...]
    h = _mod_norm(x, scale_ref[...], shift_ref[...]).astype(BF16)
    acc = jnp.zeros((tm, D), F32)
    for j in range(E // ec):
        c0 = j * ec
        u = _dot(h, w_in_ref[:, c0:c0 + ec])
        gc = _dot(h, w_in_ref[:, E + c0:E + c0 + ec])
        gb = _dot(h, w_in_ref[:, 2 * E + c0:2 * E + c0 + ec])
        z = _dot(h, w_in_ref[:, 3 * E + c0:3 * E + c0 + ec])
        v = gc * u
        vs_ref[0:SUBLANES, :] = carry_ref[j]
        vs_ref[SUBLANES:SUBLANES + tm, :] = v
        carry_ref[j] = v[tm - SUBLANES:tm, :]
        v1 = vs_ref[SUBLANES - 1:SUBLANES - 1 + tm, :]
        v2 = vs_ref[SUBLANES - 2:SUBLANES - 2 + tm, :]
        cw = cw_ref[:, c0:c0 + ec]
        conv = cw[0:1, :] * v2 + cw[1:2, :] * v1 + cw[2:3, :] * v + cb_ref[:, c0:c0 + ec]
        y = gb * conv * _silu(z)
        acc = acc + _dot(y.astype(BF16), w_out_ref[c0:c0 + ec, :])
    out = x + gate_ref[...] * acc
    if final:
        ms = jnp.mean(out * out, axis=-1, keepdims=True)
        out = out * lax.rsqrt(ms + EPS) * g_ref[...]
    o_ref[...] = out


def _conv_layer(x, mod, w_in, conv_w, conv_b, w_out, *, time_major_in, time_major_out,
                final_g=None, tm=512, ec=512):
    if time_major_in:
        S, B, D = x.shape
        x = x.reshape(S, B * D)
        x_spec = pl.BlockSpec((tm, D), lambda b, i: (i, b))
    else:
        B, S, D = x.shape
        x_spec = pl.BlockSpec((None, tm, D), lambda b, i: (b, i, 0))
    if time_major_out:
        o_spec = pl.BlockSpec((tm, D), lambda b, i: (i, b))
        o_shape = (S, B * D)
    else:
        o_spec = pl.BlockSpec((None, tm, D), lambda b, i: (b, i, 0))
        o_shape = (B, S, D)
    E = w_out.shape[0]
    final = final_g is not None
    mod_specs = [pl.BlockSpec((None, 1, D), functools.partial(lambda b, i, k: (b, 0, k), k=k))
                 for k in range(3)]
    in_specs = [x_spec] + mod_specs + [_resident(w_in.shape), _resident(conv_w.shape),
                                       _resident((1, E)), _resident(w_out.shape)]
    args = [x, mod, mod, mod, w_in, conv_w, conv_b.reshape(1, E), w_out]
    if final:
        in_specs.append(_resident((1, D)))
        args.append(final_g.reshape(1, D))
    return pl.pallas_call(
        functools.partial(_conv_layer_kernel, ec=ec, final=final),
        grid=(B, S // tm),
        in_specs=in_specs,
        out_specs=o_spec,
        out_shape=jax.ShapeDtypeStruct(o_shape, F32),
        scratch_shapes=[pltpu.VMEM((E // ec, SUBLANES, ec), F32),
                        pltpu.VMEM((SUBLANES + tm, ec), F32)],
        compiler_params=_cparams(("arbitrary", "arbitrary")),
        name="conv_layer_final" if final else "conv_layer",
    )(*args)


def _t5_bucket(dist):
    max_exact = REL_BUCKETS // 2
    d = np.maximum(dist, 0)
    ratio = np.log(np.maximum(d, 1) / max_exact) / math.log(REL_MAX_DIST / max_exact)
    large = np.minimum(max_exact + (ratio * (REL_BUCKETS - max_exact)).astype(np.int64), REL_BUCKETS - 1)
    return np.where(d < max_exact, d, large).astype(np.int32)


def _bias_tables(rel_bias_g, window, dil):
    wsub = window // dil
    qi = np.arange(ATT_BLOCK)[:, None]
    ki = np.arange(2 * ATT_BLOCK)[None, :]
    dist = ATT_BLOCK + qi - ki
    in_band = (dist >= 0) & (dist <= wsub)
    first = in_band & (ki >= ATT_BLOCK)
    bucket = _t5_bucket(dist * dil)
    bias = jnp.transpose(rel_bias_g[bucket], (2, 0, 1)).astype(F32)
    valid = np.stack([in_band, first])[:, None]
    return jnp.where(valid, bias[None], NEG_INF)


def _qkv_kernel(x_ref, shift_ref, scale_ref, w_ref, q_ref, k_ref, v_ref):
    E = q_ref.shape[-1]
    h = _mod_norm(x_ref[...], scale_ref[...], shift_ref[...]).astype(BF16)
    q_ref[...] = (_dot(h, w_ref[:, 0:E]) * (1.0 / math.sqrt(HEAD_DIM))).astype(BF16)
    k_ref[...] = _dot(h, w_ref[:, E:2 * E]).astype(BF16)
    v_ref[...] = _dot(h, w_ref[:, 2 * E:3 * E]).astype(BF16)


def _qkv_proj(x, mod, w_qkv, dil, tl=512):
    B, S, D = x.shape
    L = S // dil
    tl = min(tl, L)
    E = w_qkv.shape[1] // 3
    xv = x.reshape(B, L, dil * D)
    mod_specs = [pl.BlockSpec((None, 1, D), functools.partial(lambda b, r, i, k: (b, 0, k), k=k))
                 for k in range(2)]
    o_spec = pl.BlockSpec((None, None, tl, E), lambda b, r, i: (b, r, i, 0))
    return pl.pallas_call(
        _qkv_kernel,
        grid=(B, dil, L // tl),
        in_specs=[pl.BlockSpec((None, tl, D), lambda b, r, i: (b, i, r))] + mod_specs
                 + [_resident(w_qkv.shape)],
        out_specs=[o_spec] * 3,
        out_shape=[jax.ShapeDtypeStruct((B, dil, L, E), BF16)] * 3,
        compiler_params=_cparams(("arbitrary",) * 3),
        name=f"qkv_proj_d{dil}",
    )(xv, mod, mod, w_qkv)


def _attn_kernel(*refs, nsub, has_prev):
    if has_prev:
        q_ref, k_ref, v_ref, kp_ref, vp_ref, bias0_ref, biasb_ref, o_ref, lse_ref = refs
    else:
        q_ref, k_ref, v_ref, bias0_ref, biasb_ref, o_ref, lse_ref = refs
    n_pairs = q_ref.shape[-1] // LANES
    lane = lax.broadcasted_iota(jnp.int32, (ATT_BLOCK, LANES), 1)
    lo = lane < HEAD_DIM

    def one_head(qh, kk, vv, bias):
        s = lax.dot_general(qh, kk, (((1,), (1,)), ((), ())), preferred_element_type=F32) + bias
        m = jnp.max(s, axis=-1, keepdims=True)
        p = jnp.exp(s - m)
        den = jnp.sum(p, axis=-1, keepdims=True)
        pv = _dot(p.astype(BF16), vv)
        return pv / den, m + jnp.log(den)

    def block(row, kk_of, vv_of, bias_of):
        lse_tile = jnp.zeros((ATT_BLOCK, LANES), F32)
        for hp in range(n_pairs):
            cols = slice(hp * LANES, (hp + 1) * LANES)
            q2 = q_ref[pl.ds(row, ATT_BLOCK), cols]
            zero = jnp.zeros_like(q2)
            kk, vv = kk_of(cols), vv_of(cols)
            oa, la = one_head(jnp.where(lo, q2, zero), kk, vv, bias_of(2 * hp))
            ob, lb = one_head(jnp.where(lo, zero, q2), kk, vv, bias_of(2 * hp + 1))
            o_ref[pl.ds(row, ATT_BLOCK), cols] = jnp.where(lo, oa, ob)
            lse_tile = jnp.where(lane == 2 * hp, la, jnp.where(lane == 2 * hp + 1, lb, lse_tile))
        lse_ref[pl.ds(row, ATT_BLOCK), :] = lse_tile

    if has_prev:
        block(0,
              lambda cols: jnp.concatenate([kp_ref[:, cols], k_ref[0:ATT_BLOCK, cols]], axis=0),
              lambda cols: jnp.concatenate([vp_ref[:, cols], v_ref[0:ATT_BLOCK, cols]], axis=0),
              lambda h: bias0_ref[h])
    else:
        block(0,
              lambda cols: k_ref[0:ATT_BLOCK, cols],
              lambda cols: v_ref[0:ATT_BLOCK, cols],
              lambda h: bias0_ref[h])

    if nsub > 1:
        def body(j, carry):
            row = pl.multiple_of(j * ATT_BLOCK, ATT_BLOCK)
            prev = pl.multiple_of(row - ATT_BLOCK, ATT_BLOCK)
            block(row,
                  lambda cols: k_ref[pl.ds(prev, 2 * ATT_BLOCK), cols],
                  lambda cols: v_ref[pl.ds(prev, 2 * ATT_BLOCK), cols],
                  lambda h: biasb_ref[h])
            return carry
        lax.fori_loop(1, nsub, body, 0)


def _attention(q, k, v, tables, tq=512):
    B, dil, L, E = q.shape
    tq = min(tq, L)
    nsub = tq // ATT_BLOCK
    ntile = L // tq
    has_prev = ntile > 1
    H = tables.shape[1]
    qkv_spec = pl.BlockSpec((None, None, tq, E), lambda b, r, i: (b, r, i, 0))
    in_specs = [qkv_spec] * 3
    args = [q, k, v]
    if has_prev:
        prev_spec = pl.BlockSpec((None, None, ATT_BLOCK, E),
                                 lambda b, r, i: (b, r, jnp.maximum(i * nsub - 1, 0), 0))
        in_specs += [prev_spec] * 2
        args += [k, v]
        in_specs.append(pl.BlockSpec((None, H, ATT_BLOCK, 2 * ATT_BLOCK),
                                     lambda b, r, i: (jnp.where(i == 0, 1, 0), 0, 0, 0)))
        args.append(tables)
    else:
        in_specs.append(_resident((H, ATT_BLOCK, ATT_BLOCK)))
        args.append(tables[1, :, :, ATT_BLOCK:])
    in_specs.append(_resident((H, ATT_BLOCK, 2 * ATT_BLOCK)))
    args.append(tables[0])
    o, lse = pl.pallas_call(
        functools.partial(_attn_kernel, nsub=nsub, has_prev=has_prev),
        grid=(B, dil, ntile),
        in_specs=in_specs,
        out_specs=[pl.BlockSpec((None, tq, E), lambda b, r, i: (b, i, r)),
                   pl.BlockSpec((None, tq, LANES), lambda b, r, i: (b, i, r))],
        out_shape=[jax.ShapeDtypeStruct((B, L, dil * E), F32),
                   jax.ShapeDtypeStruct((B, L, dil * LANES), F32)],
        compiler_params=_cparams(("arbitrary",) * 3),
        name=f"attention_d{dil}",
    )(*args)
    return o.reshape(B, L * dil, E), lse.reshape(B, L * dil, LANES)


def _attn_out_kernel(x_ref, shift_ref, scale_ref, gate_ref, o0_ref, o1_ref, o2_ref,
                     l0_ref, l1_ref, l2_ref, wz_ref, w_out_ref, out_ref):
    tm, D = x_ref.shape
    E = wz_ref.shape[1]
    x = x_ref[...]
    h = _mod_norm(x, scale_ref[...], shift_ref[...]).astype(BF16)
    z = _dot(h, wz_ref[...])
    l0, l1, l2 = l0_ref[...], l1_ref[...], l2_ref[...]
    m = jnp.maximum(jnp.maximum(l0, l1), l2)
    e0, e1, e2 = jnp.exp(l0 - m), jnp.exp(l1 - m), jnp.exp(l2 - m)
    den = e0 + e1 + e2
    wg = (e0 / den, e1 / den, e2 / den)
    lo = lax.broadcasted_iota(jnp.int32, (tm, LANES), 1) < HEAD_DIM
    parts = []
    for hp in range(E // LANES):
        cols = slice(hp * LANES, (hp + 1) * LANES)
        acc = None
        for w, o_ref in zip(wg, (o0_ref, o1_ref, o2_ref)):
            wexp = jnp.where(lo, w[:, 2 * hp:2 * hp + 1], w[:, 2 * hp + 1:2 * hp + 2])
            term = wexp * o_ref[:, cols]
            acc = term if acc is None else acc + term
        parts.append(acc)
    o = jnp.concatenate(parts, axis=-1)
    y = (o * _silu(z)).astype(BF16)
    out_ref[...] = x + gate_ref[...] * _dot(y, w_out_ref[...])


def _attn_out(x, mod, os_, lses, w_z, w_out, tm=512):
    B, S, D = x.shape
    E = w_z.shape[1]
    tile = lambda w: pl.BlockSpec((None, tm, w), lambda b, i: (b, i, 0))
    mod_specs = [pl.BlockSpec((None, 1, D), functools.partial(lambda b, i, k: (b, 0, k), k=k))
                 for k in range(3)]
    return pl.pallas_call(
        _attn_out_kernel,
        grid=(B, S // tm),
        in_specs=[tile(D)] + mod_specs + [tile(E)] * 3 + [tile(LANES)] * 3
                 + [_resident(w_z.shape), _resident(w_out.shape)],
        out_specs=pl.BlockSpec((tm, D), lambda b, i: (i, b)),
        out_shape=jax.ShapeDtypeStruct((S, B * D), F32),
        compiler_params=_cparams(("arbitrary", "arbitrary")),
        name="attention_out",
    )(x, mod, mod, mod, *os_, *lses, w_z, w_out).reshape(S, B, D)


def _s5_kernel(x_ref, shift_ref, scale_ref, gate_ref, w_in_ref, bbr_ref, bbi_ref, ar_ref, ai_ref,
               cr_ref, ci_ref, dskip_ref, w_glu_ref, b_glu_ref, w_out_ref,
               o_ref, state_ref, sr_ref, si_ref, *, scan_lanes):
    ts, B, D = x_ref.shape
    R = ts * B
    n_chunks, ck, cn = bbr_ref.shape
    E = n_chunks * ck

    @pl.when(pl.program_id(0) == 0)
    def _():
        state_ref[...] = jnp.zeros_like(state_ref)

    x3 = x_ref[...]
    ms = jnp.mean(x3 * x3, axis=-1, keepdims=True)
    h3 = (x3 * lax.rsqrt(ms + EPS)) * (1.0 + scale_ref[...][None]) + shift_ref[...][None]
    h = h3.reshape(R, D).astype(BF16)
    uz = _dot(h, w_in_ref[...])
    u, z = uz[:, :E], uz[:, E:]
    u16 = u.astype(BF16)

    y_parts = []
    for c in range(n_chunks):
        uc = u16[:, c * ck:(c + 1) * ck]
        sr_ref[...] = _dot(uc, bbr_ref[c])
        si_ref[...] = _dot(uc, bbi_ref[c])
        for l0 in range(0, cn, scan_lanes):
            ls = slice(l0, l0 + scan_lanes)
            gl = slice(c * cn + l0, c * cn + l0 + scan_lanes)
            ar, ai = ar_ref[:, gl], ai_ref[:, gl]

            def step(t, carry, ls=ls, ar=ar, ai=ai):
                pr, pi = carry
                row = pl.multiple_of(t * B, B)
                nr = ar * pr - ai * pi + sr_ref[pl.ds(row, B), ls]
                ni = ar * pi + ai * pr + si_ref[pl.ds(row, B), ls]
                sr_ref[pl.ds(row, B), ls] = nr
                si_ref[pl.ds(row, B), ls] = ni
                return nr, ni

            fr, fi = lax.fori_loop(0, ts, step, (state_ref[0, :, gl], state_ref[1, :, gl]), unroll=8)
            state_ref[0, :, gl] = fr
            state_ref[1, :, gl] = fi
        y_parts.append(_dot(sr_ref[...].astype(BF16), cr_ref[c]) - _dot(si_ref[...].astype(BF16), ci_ref[c]))
    y = jnp.concatenate(y_parts, axis=-1) + dskip_ref[...] * u
    g = y * (0.5 * (1.0 + jnp.tanh(math.sqrt(2.0 / math.pi) * (y + 0.044715 * (y * y * y)))))
    y = g * _sigmoid(_dot(g.astype(BF16), w_glu_ref[...]) + b_glu_ref[...])
    y = (y * _silu(z)).astype(BF16)
    out = _dot(y, w_out_ref[...]).reshape(ts, B, D)
    o_ref[...] = x3 + gate_ref[...][None] * out


def _s5_layer(xt, mod, w_in, bbr, bbi, ar, ai, cr, ci, d_skip, w_glu, b_glu, w_out, ts=64, scan_lanes=512):
    S, B, D = xt.shape
    E = w_glu.shape[0]
    N = ar.shape[1]
    mod_specs = [pl.BlockSpec((B, D), functools.partial(lambda i, k: (0, k), k=k)) for k in range(3)]
    weights = [w_in, bbr, bbi, ar, ai, cr, ci, d_skip.reshape(1, E), w_glu, b_glu.reshape(1, E), w_out]
    return pl.pallas_call(
        functools.partial(_s5_kernel, scan_lanes=scan_lanes),
        grid=(S // ts,),
        in_specs=[pl.BlockSpec((ts, B, D), lambda i: (i, 0, 0))] + mod_specs
                 + [_resident(w.shape) for w in weights],
        out_specs=pl.BlockSpec((ts, B, D), lambda i: (i, 0, 0)),
        out_shape=jax.ShapeDtypeStruct((S, B, D), F32),
        scratch_shapes=[pltpu.VMEM((2, B, N), F32),
                        pltpu.VMEM((ts * B, bbr.shape[2]), F32),
                        pltpu.VMEM((ts * B, bbr.shape[2]), F32)],
        compiler_params=_cparams(("arbitrary",)),
        name="s5_layer",
    )(xt, mod, mod, mod, *weights)


def _s5_params(log_dt, lambda_re, lambda_im, b_re, b_im, c_re, c_im, B, groups_per_chunk=16):
    G, P = lambda_re.shape
    K = b_re.shape[2]
    dt = jnp.exp(log_dt.astype(F32))[:, None]
    lr = lambda_re.astype(F32)
    li = lambda_im.astype(F32)
    mag = jnp.exp(lr * dt)
    ar = mag * jnp.cos(li * dt)
    ai = mag * jnp.sin(li * dt)
    inv = 1.0 / (lr * lr + li * li)
    qr = ((ar - 1.0) * lr + ai * li) * inv
    qi = (ai * lr - (ar - 1.0) * li) * inv
    br = b_re.astype(F32)
    bi = b_im.astype(F32)
    bbr = qr[..., None] * br - qi[..., None] * bi
    bbi = qr[..., None] * bi + qi[..., None] * br
    gc = groups_per_chunk
    nc = G // gc
    eye = jnp.eye(gc, dtype=F32)

    def in_blocks(m):
        m = m.reshape(nc, gc, P, K)
        return jnp.einsum('cgpk,gh->cgkhp', m, eye).reshape(nc, gc * K, gc * P).astype(BF16)

    def out_blocks(m):
        m = m.astype(F32).reshape(nc, gc, K, P)
        return jnp.einsum('cgkp,gh->cgphk', m, eye).reshape(nc, gc * P, gc * K).astype(BF16)

    bcast = lambda a: jnp.broadcast_to(a.reshape(1, G * P), (B, G * P))
    return in_blocks(bbr), in_blocks(bbi), bcast(ar), bcast(ai), out_blocks(c_re), out_blocks(c_im)


def kernel(x, c, rel_bias, final_g, l0_ada_w, l0_ada_b, l0_w_in, l0_conv_w, l0_conv_b, l0_w_out, l1_ada_w, l1_ada_b, l1_w_in, l1_w_out, l2_ada_w, l2_ada_b, l2_w_in, l2_log_dt, l2_lambda_re, l2_lambda_im, l2_b_re, l2_b_im, l2_c_re, l2_c_im, l2_d_skip, l2_w_glu, l2_b_glu, l2_w_out, l3_ada_w, l3_ada_b, l3_w_in, l3_conv_w, l3_conv_b, l3_w_out):
    B, S, D = x.shape
    assert B == SUBLANES, "the S5 scan keeps the batch on the sublane axis"
    mods = _modulation(c, (l0_ada_w, l1_ada_w, l2_ada_w, l3_ada_w), (l0_ada_b, l1_ada_b, l2_ada_b, l3_ada_b))
    mod3 = [m.reshape(B, 1, 3 * D) for m in mods]

    x = _conv_layer(x, mod3[0], l0_w_in.astype(BF16), l0_conv_w, l0_conv_b, l0_w_out.astype(BF16),
                    time_major_in=False, time_major_out=False)

    n_groups = len(DILATION_PATTERNS)
    E = l1_w_out.shape[0]
    H = E // HEAD_DIM
    w1 = l1_w_in.astype(BF16)
    os_, lses = [], []
    for g, (window, dil) in enumerate(DILATION_PATTERNS):
        q, k, v = _qkv_proj(x, mod3[1], w1[:, 3 * E * g:3 * E * (g + 1)], dil)
        tables = _bias_tables(rel_bias[:, g * H:(g + 1) * H], window, dil)
        o, lse = _attention(q, k, v, tables)
        os_.append(o)
        lses.append(lse)
    xt = _attn_out(x, mod3[1], os_, lses, w1[:, 3 * E * n_groups:], l1_w_out.astype(BF16))

    ssm = _s5_params(l2_log_dt, l2_lambda_re, l2_lambda_im, l2_b_re, l2_b_im, l2_c_re, l2_c_im, B)
    xt = _s5_layer(xt, mods[2], l2_w_in.astype(BF16), *ssm, l2_d_skip, l2_w_glu.astype(BF16), l2_b_glu,
                   l2_w_out.astype(BF16))

    return _conv_layer(xt, mod3[3], l3_w_in.astype(BF16), l3_conv_w, l3_conv_b, l3_w_out.astype(BF16),
                       time_major_in=True, time_major_out=False, final_g=final_g)
```

```python
import functools
import math

import numpy as np
import jax
import jax.numpy as jnp
from jax import lax
from jax.experimental import pallas as pl
from jax.experimental.pallas import tpu as pltpu

EPS = 1e-6
HEAD_DIM = 64
ATT_BLOCK = 128
DILATION_PATTERNS = ((128, 1), (512, 4), (2048, 16))
REL_BUCKETS = 32
REL_MAX_DIST = 2048
NEG_INF = -1e30
LANES = 128
SUBLANES = 8
VMEM_LIMIT_BYTES = 56 * 1024 * 1024

F32 = jnp.float32
BF16 = jnp.bfloat16


def _cparams(semantics):
    return pltpu.CompilerParams(dimension_semantics=semantics, vmem_limit_bytes=VMEM_LIMIT_BYTES)


def _resident(shape):
    nd = len(shape)
    return pl.BlockSpec(shape, lambda *_: (0,) * nd, pipeline_mode=pl.Buffered(1))


def _silu(z):
    return z * (1.0 / (1.0 + jnp.exp(-z)))


def _sigmoid(z):
    return 1.0 / (1.0 + jnp.exp(-z))


def _mod_norm(x, scale, shift):
    ms = jnp.mean(x * x, axis=-1, keepdims=True)
    return (x * lax.rsqrt(ms + EPS)) * (1.0 + scale) + shift


def _dot(a, b):
    return jnp.dot(a, b, preferred_element_type=F32)


def _rows(start, size, stride):
    return pl.ds(start, size) if stride == 1 else pl.ds(start, size, stride=stride)


def _modulation_kernel(c_ref, w0, w1, w2, w3, b0, b1, b2, b3, o0, o1, o2, o3):
    sc = _silu(c_ref[...]).astype(BF16)
    for w, b, o in ((w0, b0, o0), (w1, b1, o1), (w2, b2, o2), (w3, b3, o3)):
        o[...] = _dot(sc, w[...].astype(BF16)) + b[...]


def _modulation(c, ws, bs, tn=512):
    B, D = c.shape
    N = ws[0].shape[1]
    w_spec = pl.BlockSpec((D, tn), lambda j: (0, j))
    b_spec = pl.BlockSpec((1, tn), lambda j: (0, j))
    o_spec = pl.BlockSpec((B, tn), lambda j: (0, j))
    return pl.pallas_call(
        _modulation_kernel,
        grid=(N // tn,),
        in_specs=[pl.BlockSpec((B, D), lambda j: (0, 0))] + [w_spec] * 4 + [b_spec] * 4,
        out_specs=[o_spec] * 4,
        out_shape=[jax.ShapeDtypeStruct((B, N), F32)] * 4,
        compiler_params=_cparams(("arbitrary",)),
        name="modulation",
    )(c, *ws, *[b.reshape(1, N) for b in bs])


def _conv_layer_kernel(*refs, ec, final):
    if final:
        (x_ref, shift_ref, scale_ref, gate_ref, w_in_ref, cw_ref, cb_ref, w_out_ref, g_ref,
         o_ref, carry_ref, vs_ref) = refs
    else:
        (x_ref, shift_ref, scale_ref, gate_ref, w_in_ref, cw_ref, cb_ref, w_out_ref,
         o_ref, carry_ref, vs_ref) = refs
    tm, D = x_ref.shape
    E = w_out_ref.shape[0]

    @pl.when(pl.program_id(1) == 0)
    def _():
        carry_ref[...] = jnp.zeros_like(carry_ref)

    x = x_ref[...]
    h = _mod_norm(x, scale_ref[...], shift_ref[...]).astype(BF16)
    acc = jnp.zeros((tm, D), F32)
    for j in range(E // ec):
        c0 = j * ec
        u = _dot(h, w_in_ref[:, c0:c0 + ec])
        gc = _dot(h, w_in_ref[:, E + c0:E + c0 + ec])
        gb = _dot(h, w_in_ref[:, 2 * E + c0:2 * E + c0 + ec])
        z = _dot(h, w_in_ref[:, 3 * E + c0:3 * E + c0 + ec])
        v = gc * u
        vs_ref[0:SUBLANES, :] = carry_ref[j]
        vs_ref[SUBLANES:SUBLANES + tm, :] = v
        carry_ref[j] = v[tm - SUBLANES:tm, :]
        v1 = vs_ref[SUBLANES - 1:SUBLANES - 1 + tm, :]
        v2 = vs_ref[SUBLANES - 2:SUBLANES - 2 + tm, :]
        cw = cw_ref[:, c0:c0 + ec]
        conv = cw[0:1, :] * v2 + cw[1:2, :] * v1 + cw[2:3, :] * v + cb_ref[:, c0:c0 + ec]
        y = gb * conv * _silu(z)
        acc = acc + _dot(y.astype(BF16), w_out_ref[c0:c0 + ec, :])
    out = x + gate_ref[...] * acc
    if final:
        ms = jnp.mean(out * out, axis=-1, keepdims=True)
        out = out * lax.rsqrt(ms + EPS) * g_ref[...]
    o_ref[...] = out


def _conv_layer(x, mod, w_in, conv_w, conv_b, w_out, *, time_major_in, final_g=None, tm=512, ec=512):
    if time_major_in:
        S, B, D = x.shape
        x = x.reshape(S, B * D)
        x_spec = pl.BlockSpec((tm, D), lambda b, i: (i, b))
    else:
        B, S, D = x.shape
        x_spec = pl.BlockSpec((None, tm, D), lambda b, i: (b, i, 0))
    E = w_out.shape[0]
    final = final_g is not None
    mod_specs = [pl.BlockSpec((None, 1, D), functools.partial(lambda b, i, k: (b, 0, k), k=k))
                 for k in range(3)]
    in_specs = [x_spec] + mod_specs + [_resident(w_in.shape), _resident(conv_w.shape),
                                       _resident((1, E)), _resident(w_out.shape)]
    args = [x, mod, mod, mod, w_in, conv_w, conv_b.reshape(1, E), w_out]
    if final:
        in_specs.append(_resident((1, D)))
        args.append(final_g.reshape(1, D))
    return pl.pallas_call(
        functools.partial(_conv_layer_kernel, ec=ec, final=final),
        grid=(B, S // tm),
        in_specs=in_specs,
        out_specs=pl.BlockSpec((None, tm, D), lambda b, i: (b, i, 0)),
        out_shape=jax.ShapeDtypeStruct((B, S, D), F32),
        scratch_shapes=[pltpu.VMEM((E // ec, SUBLANES, ec), F32),
                        pltpu.VMEM((SUBLANES + tm, ec), F32)],
        compiler_params=_cparams(("arbitrary", "arbitrary")),
        name="conv_layer_final" if final else "conv_layer",
    )(*args)


def _t5_bucket(dist):
    max_exact = REL_BUCKETS // 2
    d = np.maximum(dist, 0)
    ratio = np.log(np.maximum(d, 1) / max_exact) / math.log(REL_MAX_DIST / max_exact)
    large = np.minimum(max_exact + (ratio * (REL_BUCKETS - max_exact)).astype(np.int64), REL_BUCKETS - 1)
    return np.where(d < max_exact, d, large).astype(np.int32)


def _band_structure(window, dil):
    wsub = window // dil
    qi = np.arange(ATT_BLOCK)[:, None]
    ki = np.arange(2 * ATT_BLOCK)[None, :]
    dist = ATT_BLOCK + qi - ki
    in_band = (dist >= 0) & (dist <= wsub)
    first = in_band & (ki >= ATT_BLOCK)
    return _t5_bucket(dist * dil), np.stack([in_band, first]).astype(np.float32)


def _qkv_kernel(x_ref, shift_ref, scale_ref, w_ref, *refs, dils):
    out_refs, h_ref = refs[:-1], refs[-1]
    tn, D = x_ref.shape
    E = out_refs[0].shape[-1]
    h = _mod_norm(x_ref[...], scale_ref[...], shift_ref[...])
    n_col = D // LANES
    for c in range(n_col):
        h_ref[c] = h[:, c * LANES:(c + 1) * LANES]
    for g, dil in enumerate(dils):
        n = tn // dil
        if dil == 1:
            hp = h
        else:
            hp = jnp.concatenate(
                [jnp.concatenate([h_ref[c, pl.ds(r, n, stride=dil), :] for c in range(n_col)], axis=1)
                 for r in range(dil)], axis=0)
        hp = hp.astype(BF16)
        for j in range(3):
            c0 = (3 * g + j) * E
            res = _dot(hp, w_ref[:, c0:c0 + E])
            if j == 0:
                res = res * (1.0 / math.sqrt(HEAD_DIM))
            out_refs[3 * g + j][...] = res.astype(BF16).reshape(dil, n, E)


def _qkv_proj(x, mod, w_qkv, dils, E, tn=512):
    B, S, D = x.shape
    mod_specs = [pl.BlockSpec((None, 1, D), functools.partial(lambda b, i, k: (b, 0, k), k=k))
                 for k in range(2)]
    out_specs, out_shape = [], []
    for dil in dils:
        out_specs += [pl.BlockSpec((None, dil, tn // dil, E), lambda b, i: (b, 0, i, 0))] * 3
        out_shape += [jax.ShapeDtypeStruct((B, dil, S // dil, E), BF16)] * 3
    return pl.pallas_call(
        functools.partial(_qkv_kernel, dils=dils),
        grid=(B, S // tn),
        in_specs=[pl.BlockSpec((None, tn, D), lambda b, i: (b, i, 0))] + mod_specs
                 + [_resident(w_qkv.shape)],
        out_specs=out_specs,
        out_shape=out_shape,
        scratch_shapes=[pltpu.VMEM((D // LANES, tn, LANES), F32)],
        compiler_params=_cparams(("arbitrary", "arbitrary")),
        name="qkv_proj",
    )(x, mod, mod, w_qkv)


def _attn_kernel(*refs, nr, nsub, has_prev, stride, n_heads):
    if has_prev:
        (q_ref, k_ref, v_ref, kp_ref, vp_ref, rb_ref, bucket_ref, mask_ref,
         o_ref, lse_ref, tbl_ref, s_ref, p_ref) = refs
    else:
        (q_ref, k_ref, v_ref, rb_ref, bucket_ref, mask_ref,
         o_ref, lse_ref, tbl_ref, s_ref, p_ref) = refs
    n_pairs = n_heads // 2
    pair_rows = 2 * ATT_BLOCK
    nk = s_ref.shape[1]
    step = pl.program_id(1)

    @pl.when((pl.program_id(0) == 0) & (step == 0))
    def _():
        def fill(b, carry):
            hit = bucket_ref[...] == b
            for h in range(n_heads):
                rows = slice(h * ATT_BLOCK, (h + 1) * ATT_BLOCK)
                s_ref[rows, :] = jnp.where(hit, rb_ref[b, h], s_ref[rows, :])
            return carry
        s_ref[...] = jnp.zeros_like(s_ref)
        lax.fori_loop(0, REL_BUCKETS, fill, 0)
        for t in range(tbl_ref.shape[0]):
            for h in range(n_heads):
                rows = slice(h * ATT_BLOCK, (h + 1) * ATT_BLOCK)
                tbl_ref[t, rows, :] = jnp.where(mask_ref[t] > 0.0, s_ref[rows, :], NEG_INF)

    lane = lax.broadcasted_iota(jnp.int32, (ATT_BLOCK, LANES), 1)
    lo = lane < HEAD_DIM

    def block(q_of, k_of, v_of, table, out_rows):
        for hp in range(n_pairs):
            cols = slice(hp * LANES, (hp + 1) * LANES)
            rows = slice(hp * pair_rows, (hp + 1) * pair_rows)
            q2 = q_of(cols)
            zero = jnp.zeros_like(q2)
            q_pair = jnp.concatenate([jnp.where(lo, q2, zero), jnp.where(lo, zero, q2)], axis=0)
            s_ref[rows, :] = lax.dot_general(q_pair, k_of(cols), (((1,), (1,)), ((), ())),
                                             preferred_element_type=F32) + tbl_ref[table, rows, :]
        s = s_ref[...]
        m = jnp.max(s, axis=-1, keepdims=True)
        p = jnp.exp(s - m)
        den = jnp.sum(p, axis=-1, keepdims=True)
        p_ref[...] = p.astype(BF16)
        lse = m + jnp.log(den)
        lse_tile = jnp.zeros((ATT_BLOCK, LANES), F32)
        for hp in range(n_pairs):
            cols = slice(hp * LANES, (hp + 1) * LANES)
            rows = slice(hp * pair_rows, (hp + 1) * pair_rows)
            res = _dot(p_ref[rows, :], v_of(cols)) / den[rows]
            o_ref[hp, out_rows, :] = jnp.where(lo, res[:ATT_BLOCK], res[ATT_BLOCK:])
            la = lse[hp * pair_rows:hp * pair_rows + ATT_BLOCK]
            lb = lse[hp * pair_rows + ATT_BLOCK:(hp + 1) * pair_rows]
            lse_tile = jnp.where(lane == 2 * hp, la, jnp.where(lane == 2 * hp + 1, lb, lse_tile))
        lse_ref[out_rows, :] = lse_tile

    for r in range(nr):
        for j in range(nsub):
            row = j * ATT_BLOCK
            q_of = lambda cols, r=r, row=row: q_ref[r, row:row + ATT_BLOCK, cols]
            if has_prev:
                out_rows = _rows(row * stride + r, ATT_BLOCK, stride)
                if j == 0:
                    k_of = lambda cols, r=r: jnp.concatenate([kp_ref[r, :, cols], k_ref[r, 0:ATT_BLOCK, cols]], axis=0)
                    v_of = lambda cols, r=r: jnp.concatenate([vp_ref[r, :, cols], v_ref[r, 0:ATT_BLOCK, cols]], axis=0)
                    table = jnp.where(step == 0, 1, 0)
                else:
                    k_of = lambda cols, r=r, row=row: k_ref[r, row - ATT_BLOCK:row + ATT_BLOCK, cols]
                    v_of = lambda cols, r=r, row=row: v_ref[r, row - ATT_BLOCK:row + ATT_BLOCK, cols]
                    table = 0
            else:
                out_rows = _rows(step * nr + r, ATT_BLOCK, stride)
                k_of = lambda cols, r=r: k_ref[r, :, cols]
                v_of = lambda cols, r=r: v_ref[r, :, cols]
                table = 0
            block(q_of, k_of, v_of, table, out_rows)


def _attention(q, k, v, rel_bias_g, window, dil, tokens_per_step=512):
    B, _, L, E = q.shape
    S = L * dil
    H = E // HEAD_DIM
    bucket, masks = _band_structure(window, dil)
    has_prev = L > ATT_BLOCK
    if has_prev:
        nr = dil
        nsub = tokens_per_step // (dil * ATT_BLOCK)
        tq = nsub * ATT_BLOCK
        n_steps = L // tq
        nk = 2 * ATT_BLOCK
        qkv_spec = pl.BlockSpec((None, nr, tq, E), lambda b, i: (b, 0, i, 0))
        prev_spec = pl.BlockSpec((None, nr, ATT_BLOCK, E),
                                 lambda b, i: (b, 0, jnp.maximum(i * nsub - 1, 0), 0))
        in_specs = [qkv_spec] * 3 + [prev_spec] * 2
        args = [q, k, v, k, v]
        out_block_rows = tokens_per_step
        out_step = lambda i: i
    else:
        nr = 4
        nsub = 1
        n_steps = dil // nr
        nk = ATT_BLOCK
        bucket, masks = bucket[:, ATT_BLOCK:], masks[1:, :, ATT_BLOCK:]
        qkv_spec = pl.BlockSpec((None, nr, ATT_BLOCK, E), lambda b, i: (b, i, 0, 0))
        in_specs = [qkv_spec] * 3
        args = [q, k, v]
        out_block_rows = S
        out_step = lambda i: 0
    in_specs += [pl.BlockSpec(memory_space=pltpu.SMEM), _resident(bucket.shape), _resident(masks.shape)]
    args += [rel_bias_g, jnp.asarray(bucket), jnp.asarray(masks)]
    return pl.pallas_call(
        functools.partial(_attn_kernel, nr=nr, nsub=nsub, has_prev=has_prev, stride=dil, n_heads=H),
        grid=(B, n_steps),
        in_specs=in_specs,
        out_specs=[pl.BlockSpec((None, E // LANES, out_block_rows, LANES), lambda b, i: (b, 0, out_step(i), 0)),
                   pl.BlockSpec((None, out_block_rows, LANES), lambda b, i: (b, out_step(i), 0))],
        out_shape=[jax.ShapeDtypeStruct((B, E // LANES, S, LANES), F32),
                   jax.ShapeDtypeStruct((B, S, LANES), F32)],
        scratch_shapes=[pltpu.VMEM((masks.shape[0], H * ATT_BLOCK, nk), F32),
                        pltpu.VMEM((H * ATT_BLOCK, nk), F32),
                        pltpu.VMEM((H * ATT_BLOCK, nk), BF16)],
        compiler_params=_cparams(("arbitrary", "arbitrary")),
        name=f"attention_d{dil}",
    )(*args)


def _attn_out_kernel(x_ref, shift_ref, scale_ref, gate_ref, o0_ref, o1_ref, o2_ref,
                     l0_ref, l1_ref, l2_ref, wz_ref, w_out_ref, out_ref):
    tm, D = x_ref.shape
    E = wz_ref.shape[1]
    x = x_ref[...]
    h = _mod_norm(x, scale_ref[...], shift_ref[...]).astype(BF16)
    z = _dot(h, wz_ref[...])
    l0, l1, l2 = l0_ref[...], l1_ref[...], l2_ref[...]
    m = jnp.maximum(jnp.maximum(l0, l1), l2)
    e0, e1, e2 = jnp.exp(l0 - m), jnp.exp(l1 - m), jnp.exp(l2 - m)
    den = e0 + e1 + e2
    wg = (e0 / den, e1 / den, e2 / den)
    lo = lax.broadcasted_iota(jnp.int32, (tm, LANES), 1) < HEAD_DIM
    parts = []
    for hp in range(E // LANES):
        acc = None
        for w, o_ref in zip(wg, (o0_ref, o1_ref, o2_ref)):
            wexp = jnp.where(lo, w[:, 2 * hp:2 * hp + 1], w[:, 2 * hp + 1:2 * hp + 2])
            term = wexp * o_ref[hp]
            acc = term if acc is None else acc + term
        parts.append(acc)
    o = jnp.concatenate(parts, axis=-1)
    y = (o * _silu(z)).astype(BF16)
    out_ref[...] = x + gate_ref[...] * _dot(y, w_out_ref[...])


def _attn_out(x, mod, os_, lses, w_z, w_out, tm=512):
    B, S, D = x.shape
    E = w_z.shape[1]
    tile = lambda w: pl.BlockSpec((None, tm, w), lambda b, i: (b, i, 0))
    mod_specs = [pl.BlockSpec((None, 1, D), functools.partial(lambda b, i, k: (b, 0, k), k=k))
                 for k in range(3)]
    return pl.pallas_call(
        _attn_out_kernel,
        grid=(B, S // tm),
        in_specs=[tile(D)] + mod_specs
                 + [pl.BlockSpec((None, E // LANES, tm, LANES), lambda b, i: (b, 0, i, 0))] * 3
                 + [tile(LANES)] * 3
                 + [_resident(w_z.shape), _resident(w_out.shape)],
        out_specs=pl.BlockSpec((tm, D), lambda b, i: (i, b)),
        out_shape=jax.ShapeDtypeStruct((S, B * D), F32),
        compiler_params=_cparams(("arbitrary", "arbitrary")),
        name="attention_out",
    )(x, mod, mod, mod, *os_, *lses, w_z, w_out).reshape(S, B, D)


def _s5_kernel(x_ref, shift_ref, scale_ref, gate_ref, w_in_ref, bbr_ref, bbi_ref, ar_ref, ai_ref,
               cr_ref, ci_ref, dskip_ref, w_glu_ref, b_glu_ref, w_out_ref,
               o_ref, state_ref, sr_ref, si_ref, *, scan_lanes):
    ts, B, D = x_ref.shape
    R = ts * B
    n_chunks, ck, cn = bbr_ref.shape
    E = n_chunks * ck

    @pl.when(pl.program_id(0) == 0)
    def _():
        state_ref[...] = jnp.zeros_like(state_ref)

    x3 = x_ref[...]
    ms = jnp.mean(x3 * x3, axis=-1, keepdims=True)
    h3 = (x3 * lax.rsqrt(ms + EPS)) * (1.0 + scale_ref[...][None]) + shift_ref[...][None]
    h = h3.reshape(R, D).astype(BF16)
    uz = _dot(h, w_in_ref[...])
    u, z = uz[:, :E], uz[:, E:]
    u16 = u.astype(BF16)

    y_parts = []
    for c in range(n_chunks):
        uc = u16[:, c * ck:(c + 1) * ck]
        sr_ref[...] = _dot(uc, bbr_ref[c])
        si_ref[...] = _dot(uc, bbi_ref[c])
        for l0 in range(0, cn, scan_lanes):
            ls = slice(l0, l0 + scan_lanes)
            gl = slice(c * cn + l0, c * cn + l0 + scan_lanes)
            ar, ai = ar_ref[:, gl], ai_ref[:, gl]

            def step(t, carry, ls=ls, ar=ar, ai=ai):
                pr, pi = carry
                row = pl.multiple_of(t * B, B)
                nr = ar * pr - ai * pi + sr_ref[pl.ds(row, B), ls]
                ni = ar * pi + ai * pr + si_ref[pl.ds(row, B), ls]
                sr_ref[pl.ds(row, B), ls] = nr
                si_ref[pl.ds(row, B), ls] = ni
                return nr, ni

            fr, fi = lax.fori_loop(0, ts, step, (state_ref[0, :, gl], state_ref[1, :, gl]), unroll=8)
            state_ref[0, :, gl] = fr
            state_ref[1, :, gl] = fi
        y_parts.append(_dot(sr_ref[...].astype(BF16), cr_ref[c]) - _dot(si_ref[...].astype(BF16), ci_ref[c]))
    y = jnp.concatenate(y_parts, axis=-1) + dskip_ref[...] * u
    g = y * (0.5 * (1.0 + jnp.tanh(math.sqrt(2.0 / math.pi) * (y + 0.044715 * (y * y * y)))))
    y = g * _sigmoid(_dot(g.astype(BF16), w_glu_ref[...]) + b_glu_ref[...])
    y = (y * _silu(z)).astype(BF16)
    out = _dot(y, w_out_ref[...]).reshape(ts, B, D)
    o_ref[...] = x3 + gate_ref[...][None] * out


def _s5_layer(xt, mod, w_in, bbr, bbi, ar, ai, cr, ci, d_skip, w_glu, b_glu, w_out, ts=64, scan_lanes=512):
    S, B, D = xt.shape
    E = w_glu.shape[0]
    N = ar.shape[1]
    mod_specs = [pl.BlockSpec((B, D), functools.partial(lambda i, k: (0, k), k=k)) for k in range(3)]
    weights = [w_in, bbr, bbi, ar, ai, cr, ci, d_skip.reshape(1, E), w_glu, b_glu.reshape(1, E), w_out]
    return pl.pallas_call(
        functools.partial(_s5_kernel, scan_lanes=scan_lanes),
        grid=(S // ts,),
        in_specs=[pl.BlockSpec((ts, B, D), lambda i: (i, 0, 0))] + mod_specs
                 + [_resident(w.shape) for w in weights],
        out_specs=pl.BlockSpec((ts, B, D), lambda i: (i, 0, 0)),
        out_shape=jax.ShapeDtypeStruct((S, B, D), F32),
        scratch_shapes=[pltpu.VMEM((2, B, N), F32),
                        pltpu.VMEM((ts * B, bbr.shape[2]), F32),
                        pltpu.VMEM((ts * B, bbr.shape[2]), F32)],
        compiler_params=_cparams(("arbitrary",)),
        name="s5_layer",
    )(xt, mod, mod, mod, *weights)


def _s5_params(log_dt, lambda_re, lambda_im, b_re, b_im, c_re, c_im, B, groups_per_chunk=16):
    G, P = lambda_re.shape
    K = b_re.shape[2]
    dt = jnp.exp(log_dt.astype(F32))[:, None]
    lr = lambda_re.astype(F32)
    li = lambda_im.astype(F32)
    mag = jnp.exp(lr * dt)
    ar = mag * jnp.cos(li * dt)
    ai = mag * jnp.sin(li * dt)
    inv = 1.0 / (lr * lr + li * li)
    qr = ((ar - 1.0) * lr + ai * li) * inv
    qi = (ai * lr - (ar - 1.0) * li) * inv
    br = b_re.astype(F32)
    bi = b_im.astype(F32)
    bbr = qr[..., None] * br - qi[..., None] * bi
    bbi = qr[..., None] * bi + qi[..., None] * br
    gc = groups_per_chunk
    nc = G // gc
    eye = jnp.eye(gc, dtype=F32)

    def in_blocks(m):
        m = m.reshape(nc, gc, P, K)
        return jnp.einsum('cgpk,gh->cgkhp', m, eye).reshape(nc, gc * K, gc * P).astype(BF16)

    def out_blocks(m):
        m = m.astype(F32).reshape(nc, gc, K, P)
        return jnp.einsum('cgkp,gh->cgphk', m, eye).reshape(nc, gc * P, gc * K).astype(BF16)

    bcast = lambda a: jnp.broadcast_to(a.reshape(1, G * P), (B, G * P))
    return in_blocks(bbr), in_blocks(bbi), bcast(ar), bcast(ai), out_blocks(c_re), out_blocks(c_im)


def kernel(x, c, rel_bias, final_g, l0_ada_w, l0_ada_b, l0_w_in, l0_conv_w, l0_conv_b, l0_w_out, l1_ada_w, l1_ada_b, l1_w_in, l1_w_out, l2_ada_w, l2_ada_b, l2_w_in, l2_log_dt, l2_lambda_re, l2_lambda_im, l2_b_re, l2_b_im, l2_c_re, l2_c_im, l2_d_skip, l2_w_glu, l2_b_glu, l2_w_out, l3_ada_w, l3_ada_b, l3_w_in, l3_conv_w, l3_conv_b, l3_w_out):
    B, S, D = x.shape
    assert B == SUBLANES, "the S5 scan keeps the batch on the sublane axis"
    mods = _modulation(c, (l0_ada_w, l1_ada_w, l2_ada_w, l3_ada_w), (l0_ada_b, l1_ada_b, l2_ada_b, l3_ada_b))
    mod3 = [m.reshape(B, 1, 3 * D) for m in mods]

    x = _conv_layer(x, mod3[0], l0_w_in.astype(BF16), l0_conv_w, l0_conv_b, l0_w_out.astype(BF16),
                    time_major_in=False)

    n_groups = len(DILATION_PATTERNS)
    dils = tuple(dil for _, dil in DILATION_PATTERNS)
    E = l1_w_out.shape[0]
    H = E // HEAD_DIM
    w1 = l1_w_in.astype(BF16)
    qkv = _qkv_proj(x, mod3[1], w1[:, :3 * E * n_groups], dils, E)
    os_, lses = [], []
    for g, (window, dil) in enumerate(DILATION_PATTERNS):
        o, lse = _attention(*qkv[3 * g:3 * g + 3], rel_bias[:, g * H:(g + 1) * H], window, dil)
        os_.append(o)
        lses.append(lse)
    xt = _attn_out(x, mod3[1], os_, lses, w1[:, 3 * E * n_groups:], l1_w_out.astype(BF16))

    ssm = _s5_params(l2_log_dt, l2_lambda_re, l2_lambda_im, l2_b_re, l2_b_im, l2_c_re, l2_c_im, B)
    xt = _s5_layer(xt, mods[2], l2_w_in.astype(BF16), *ssm, l2_d_skip, l2_w_glu.astype(BF16), l2_b_glu,
                   l2_w_out.astype(BF16))

    return _conv_layer(xt, mod3[3], l3_w_in.astype(BF16), l3_conv_w, l3_conv_b, l3_w_out.astype(BF16),
                       time_major_in=True, final_g=final_g)
```

```python
import functools
import math

import numpy as np
import jax
import jax.numpy as jnp
from jax import lax
from jax.experimental import pallas as pl
from jax.experimental.pallas import tpu as pltpu

EPS = 1e-6
HEAD_DIM = 64
ATT_BLOCK = 128
DILATION_PATTERNS = ((128, 1), (512, 4), (2048, 16))
REL_BUCKETS = 32
REL_MAX_DIST = 2048
NEG_INF = -1e30
LOG2E = math.log2(math.e)
LANES = 128
SUBLANES = 8
VMEM_LIMIT_BYTES = 56 * 1024 * 1024

F32 = jnp.float32
BF16 = jnp.bfloat16


def _cparams(semantics):
    return pltpu.CompilerParams(dimension_semantics=semantics, vmem_limit_bytes=VMEM_LIMIT_BYTES)


def _resident(shape):
    nd = len(shape)
    return pl.BlockSpec(shape, lambda *_: (0,) * nd, pipeline_mode=pl.Buffered(1))


def _silu(z):
    return z * (1.0 / (1.0 + jnp.exp(-z)))


def _sigmoid(z):
    return 1.0 / (1.0 + jnp.exp(-z))


def _mod_norm(x, scale, shift):
    ms = jnp.mean(x * x, axis=-1, keepdims=True)
    return (x * lax.rsqrt(ms + EPS)) * (1.0 + scale) + shift


def _dot(a, b):
    return jnp.dot(a, b, preferred_element_type=F32)


def _rows(start, size, stride):
    return pl.ds(start, size) if stride == 1 else pl.ds(start, size, stride=stride)


def _modulation_kernel(c_ref, w0, w1, w2, w3, b0, b1, b2, b3, o0, o1, o2, o3):
    sc = _silu(c_ref[...]).astype(BF16)
    for w, b, o in ((w0, b0, o0), (w1, b1, o1), (w2, b2, o2), (w3, b3, o3)):
        o[...] = _dot(sc, w[...].astype(BF16)) + b[...]


def _modulation(c, ws, bs, tn=512):
    B, D = c.shape
    N = ws[0].shape[1]
    w_spec = pl.BlockSpec((D, tn), lambda j: (0, j))
    b_spec = pl.BlockSpec((1, tn), lambda j: (0, j))
    o_spec = pl.BlockSpec((B, tn), lambda j: (0, j))
    return pl.pallas_call(
        _modulation_kernel,
        grid=(N // tn,),
        in_specs=[pl.BlockSpec((B, D), lambda j: (0, 0))] + [w_spec] * 4 + [b_spec] * 4,
        out_specs=[o_spec] * 4,
        out_shape=[jax.ShapeDtypeStruct((B, N), F32)] * 4,
        compiler_params=_cparams(("arbitrary",)),
        name="modulation",
    )(c, *ws, *[b.reshape(1, N) for b in bs])


def _conv_layer_kernel(*refs, ec, final):
    if final:
        (x_ref, shift_ref, scale_ref, gate_ref, w_in_ref, cw_ref, cb_ref, w_out_ref, g_ref,
         o_ref, carry_ref, vs_ref) = refs
    else:
        (x_ref, shift_ref, scale_ref, gate_ref, w_in_ref, cw_ref, cb_ref, w_out_ref,
         o_ref, carry_ref, vs_ref) = refs
    tm, D = x_ref.shape
    E = w_out_ref.shape[0]

    @pl.when(pl.program_id(1) == 0)
    def _():
        carry_ref[...] = jnp.zeros_like(carry_ref)

    x = x_ref[...]
    h = _mod_norm(x, scale_ref[...], shift_ref[...]).astype(BF16)
    acc = jnp.zeros((tm, D), F32)
    for j in range(E // ec):
        c0 = j * ec
        u = _dot(h, w_in_ref[:, c0:c0 + ec])
        gc = _dot(h, w_in_ref[:, E + c0:E + c0 + ec])
        gb = _dot(h, w_in_ref[:, 2 * E + c0:2 * E + c0 + ec])
        z = _dot(h, w_in_ref[:, 3 * E + c0:3 * E + c0 + ec])
        v = gc * u
        vs_ref[0:SUBLANES, :] = carry_ref[j]
        vs_ref[SUBLANES:SUBLANES + tm, :] = v
        carry_ref[j] = v[tm - SUBLANES:tm, :]
        v1 = vs_ref[SUBLANES - 1:SUBLANES - 1 + tm, :]
        v2 = vs_ref[SUBLANES - 2:SUBLANES - 2 + tm, :]
        cw = cw_ref[:, c0:c0 + ec]
        conv = cw[0:1, :] * v2 + cw[1:2, :] * v1 + cw[2:3, :] * v + cb_ref[:, c0:c0 + ec]
        y = gb * conv * _silu(z)
        acc = acc + _dot(y.astype(BF16), w_out_ref[c0:c0 + ec, :])
    out = x + gate_ref[...] * acc
    if final:
        ms = jnp.mean(out * out, axis=-1, keepdims=True)
        out = out * lax.rsqrt(ms + EPS) * g_ref[...]
    o_ref[...] = out


def _conv_layer(x, mod, w_in, conv_w, conv_b, w_out, *, time_major_in, final_g=None, tm=512, ec=512):
    if time_major_in:
        S, B, D = x.shape
        x = x.reshape(S, B * D)
        x_spec = pl.BlockSpec((tm, D), lambda b, i: (i, b))
    else:
        B, S, D = x.shape
        x_spec = pl.BlockSpec((None, tm, D), lambda b, i: (b, i, 0))
    E = w_out.shape[0]
    final = final_g is not None
    mod_specs = [pl.BlockSpec((None, 1, D), functools.partial(lambda b, i, k: (b, 0, k), k=k))
                 for k in range(3)]
    in_specs = [x_spec] + mod_specs + [_resident(w_in.shape), _resident(conv_w.shape),
                                       _resident((1, E)), _resident(w_out.shape)]
    args = [x, mod, mod, mod, w_in, conv_w, conv_b.reshape(1, E), w_out]
    if final:
        in_specs.append(_resident((1, D)))
        args.append(final_g.reshape(1, D))
    return pl.pallas_call(
        functools.partial(_conv_layer_kernel, ec=ec, final=final),
        grid=(B, S // tm),
        in_specs=in_specs,
        out_specs=pl.BlockSpec((None, tm, D), lambda b, i: (b, i, 0)),
        out_shape=jax.ShapeDtypeStruct((B, S, D), F32),
        scratch_shapes=[pltpu.VMEM((E // ec, SUBLANES, ec), F32),
                        pltpu.VMEM((SUBLANES + tm, ec), F32)],
        compiler_params=_cparams(("arbitrary", "arbitrary")),
        name="conv_layer_final" if final else "conv_layer",
    )(*args)


def _t5_bucket(dist):
    max_exact = REL_BUCKETS // 2
    d = np.maximum(dist, 0)
    ratio = np.log(np.maximum(d, 1) / max_exact) / math.log(REL_MAX_DIST / max_exact)
    large = np.minimum(max_exact + (ratio * (REL_BUCKETS - max_exact)).astype(np.int64), REL_BUCKETS - 1)
    return np.where(d < max_exact, d, large).astype(np.int32)


def _band_structure(window, dil):
    wsub = window // dil
    qi = np.arange(ATT_BLOCK)[:, None]
    ki = np.arange(2 * ATT_BLOCK)[None, :]
    dist = ATT_BLOCK + qi - ki
    in_band = (dist >= 0) & (dist <= wsub)
    first = in_band & (ki >= ATT_BLOCK)
    return _t5_bucket(dist * dil), np.stack([in_band, first]).astype(np.float32)


def _qkv_kernel(x_ref, shift_ref, scale_ref, w_ref, *refs, dils):
    out_refs, h_ref = refs[:-1], refs[-1]
    tn, D = x_ref.shape
    E = out_refs[0].shape[-1]
    h = _mod_norm(x_ref[...], scale_ref[...], shift_ref[...])
    n_col = D // LANES
    for c in range(n_col):
        h_ref[c] = h[:, c * LANES:(c + 1) * LANES]
    for g, dil in enumerate(dils):
        n = tn // dil
        if dil == 1:
            hp = h
        else:
            hp = jnp.concatenate(
                [jnp.concatenate([h_ref[c, pl.ds(r, n, stride=dil), :] for c in range(n_col)], axis=1)
                 for r in range(dil)], axis=0)
        hp = hp.astype(BF16)
        for j in range(3):
            c0 = (3 * g + j) * E
            res = _dot(hp, w_ref[:, c0:c0 + E])
            if j == 0:
                res = res * (LOG2E / math.sqrt(HEAD_DIM))
            out_refs[3 * g + j][...] = res.astype(BF16).reshape(dil, n, E)


def _qkv_proj(x, mod, w_qkv, dils, E, tn=512):
    B, S, D = x.shape
    mod_specs = [pl.BlockSpec((None, 1, D), functools.partial(lambda b, i, k: (b, 0, k), k=k))
                 for k in range(2)]
    out_specs, out_shape = [], []
    for dil in dils:
        out_specs += [pl.BlockSpec((None, dil, tn // dil, E), lambda b, i: (b, 0, i, 0))] * 3
        out_shape += [jax.ShapeDtypeStruct((B, dil, S // dil, E), BF16)] * 3
    return pl.pallas_call(
        functools.partial(_qkv_kernel, dils=dils),
        grid=(B, S // tn),
        in_specs=[pl.BlockSpec((None, tn, D), lambda b, i: (b, i, 0))] + mod_specs
                 + [_resident(w_qkv.shape)],
        out_specs=out_specs,
        out_shape=out_shape,
        scratch_shapes=[pltpu.VMEM((D // LANES, tn, LANES), F32)],
        compiler_params=_cparams(("arbitrary", "arbitrary")),
        name="qkv_proj",
    )(x, mod, mod, w_qkv)


def _attn_kernel(*refs, nr, nsub, has_prev, stride, n_heads):
    if has_prev:
        (q_ref, k_ref, v_ref, kp_ref, vp_ref, rb_ref, bucket_ref, mask_ref,
         o_ref, m_ref, den_ref, tbl_ref, s_ref, p_ref, stat_ref) = refs
    else:
        (q_ref, k_ref, v_ref, rb_ref, bucket_ref, mask_ref,
         o_ref, m_ref, den_ref, tbl_ref, s_ref, p_ref, stat_ref) = refs
    n_pairs = n_heads // 2
    pair_rows = 2 * ATT_BLOCK
    nk = s_ref.shape[1]
    step = pl.program_id(1)

    @pl.when((pl.program_id(0) == 0) & (step == 0))
    def _():
        def fill(b, carry):
            hit = bucket_ref[...] == b
            for h in range(n_heads):
                rows = slice(h * ATT_BLOCK, (h + 1) * ATT_BLOCK)
                s_ref[rows, :] = jnp.where(hit, rb_ref[b, h] * LOG2E, s_ref[rows, :])
            return carry
        s_ref[...] = jnp.zeros_like(s_ref)
        lax.fori_loop(0, REL_BUCKETS, fill, 0)
        for t in range(tbl_ref.shape[0]):
            for h in range(n_heads):
                rows = slice(h * ATT_BLOCK, (h + 1) * ATT_BLOCK)
                tbl_ref[t, rows, :] = jnp.where(mask_ref[t] > 0.0, s_ref[rows, :], NEG_INF)
        stat_ref[0] = jnp.zeros((ATT_BLOCK, LANES), F32)
        stat_ref[1] = jnp.ones((ATT_BLOCK, LANES), F32)

    lane = lax.broadcasted_iota(jnp.int32, (ATT_BLOCK, LANES), 1)
    lo = lane < HEAD_DIM

    def block(q_of, k_of, v_of, table, out_rows):
        for hp in range(n_pairs):
            cols = slice(hp * LANES, (hp + 1) * LANES)
            rows = slice(hp * pair_rows, (hp + 1) * pair_rows)
            q2 = q_of(cols)
            zero = jnp.zeros_like(q2)
            q_pair = jnp.concatenate([jnp.where(lo, q2, zero), jnp.where(lo, zero, q2)], axis=0)
            s_ref[rows, :] = lax.dot_general(q_pair, k_of(cols), (((1,), (1,)), ((), ())),
                                             preferred_element_type=F32)
        for h in range(n_heads):
            rows = slice(h * ATT_BLOCK, (h + 1) * ATT_BLOCK)
            s = s_ref[rows, :] + tbl_ref[table, rows, :]
            m = jnp.max(s, axis=-1, keepdims=True)
            p = jnp.exp2(s - m)
            den = jnp.sum(p, axis=-1, keepdims=True)
            p_ref[rows, :] = p.astype(BF16)
            stat_ref[0, :, h:h + 1] = m
            stat_ref[1, :, h:h + 1] = den
        m_ref[out_rows, :] = stat_ref[0]
        den_ref[out_rows, :] = stat_ref[1]
        for hp in range(n_pairs):
            cols = slice(hp * LANES, (hp + 1) * LANES)
            rows = slice(hp * pair_rows, (hp + 1) * pair_rows)
            res = _dot(p_ref[rows, :], v_of(cols))
            o_ref[hp, out_rows, :] = jnp.where(lo, res[:ATT_BLOCK], res[ATT_BLOCK:])

    for r in range(nr):
        for j in range(nsub):
            row = j * ATT_BLOCK
            q_of = lambda cols, r=r, row=row: q_ref[r, row:row + ATT_BLOCK, cols]
            if has_prev:
                out_rows = _rows(row * stride + r, ATT_BLOCK, stride)
                if j == 0:
                    k_of = lambda cols, r=r: jnp.concatenate([kp_ref[r, :, cols], k_ref[r, 0:ATT_BLOCK, cols]], axis=0)
                    v_of = lambda cols, r=r: jnp.concatenate([vp_ref[r, :, cols], v_ref[r, 0:ATT_BLOCK, cols]], axis=0)
                    table = jnp.where(step == 0, 1, 0)
                else:
                    k_of = lambda cols, r=r, row=row: k_ref[r, row - ATT_BLOCK:row + ATT_BLOCK, cols]
                    v_of = lambda cols, r=r, row=row: v_ref[r, row - ATT_BLOCK:row + ATT_BLOCK, cols]
                    table = 0
            else:
                out_rows = _rows(step * nr + r, ATT_BLOCK, stride)
                k_of = lambda cols, r=r: k_ref[r, :, cols]
                v_of = lambda cols, r=r: v_ref[r, :, cols]
                table = 0
            block(q_of, k_of, v_of, table, out_rows)


def _attention(q, k, v, rel_bias_g, window, dil, tokens_per_step=512):
    B, _, L, E = q.shape
    S = L * dil
    H = E // HEAD_DIM
    bucket, masks = _band_structure(window, dil)
    has_prev = L > ATT_BLOCK
    if has_prev:
        nr = dil
        nsub = tokens_per_step // (dil * ATT_BLOCK)
        tq = nsub * ATT_BLOCK
        n_steps = L // tq
        nk = 2 * ATT_BLOCK
        qkv_spec = pl.BlockSpec((None, nr, tq, E), lambda b, i: (b, 0, i, 0))
        prev_spec = pl.BlockSpec((None, nr, ATT_BLOCK, E),
                                 lambda b, i: (b, 0, jnp.maximum(i * nsub - 1, 0), 0))
        in_specs = [qkv_spec] * 3 + [prev_spec] * 2
        args = [q, k, v, k, v]
        out_block_rows = tokens_per_step
        out_step = lambda i: i
    else:
        nr = 4
        nsub = 1
        n_steps = dil // nr
        nk = ATT_BLOCK
        bucket, masks = bucket[:, ATT_BLOCK:], masks[1:, :, ATT_BLOCK:]
        qkv_spec = pl.BlockSpec((None, nr, ATT_BLOCK, E), lambda b, i: (b, i, 0, 0))
        in_specs = [qkv_spec] * 3
        args = [q, k, v]
        out_block_rows = S
        out_step = lambda i: 0
    in_specs += [pl.BlockSpec(memory_space=pltpu.SMEM), _resident(bucket.shape), _resident(masks.shape)]
    args += [rel_bias_g, jnp.asarray(bucket), jnp.asarray(masks)]
    stat_spec = pl.BlockSpec((None, out_block_rows, LANES), lambda b, i: (b, out_step(i), 0))
    stat_shape = jax.ShapeDtypeStruct((B, S, LANES), F32)
    return pl.pallas_call(
        functools.partial(_attn_kernel, nr=nr, nsub=nsub, has_prev=has_prev, stride=dil, n_heads=H),
        grid=(B, n_steps),
        in_specs=in_specs,
        out_specs=[pl.BlockSpec((None, E // LANES, out_block_rows, LANES), lambda b, i: (b, 0, out_step(i), 0)),
                   stat_spec, stat_spec],
        out_shape=[jax.ShapeDtypeStruct((B, E // LANES, S, LANES), F32), stat_shape, stat_shape],
        scratch_shapes=[pltpu.VMEM((masks.shape[0], H * ATT_BLOCK, nk), F32),
                        pltpu.VMEM((H * ATT_BLOCK, nk), F32),
                        pltpu.VMEM((H * ATT_BLOCK, nk), BF16),
                        pltpu.VMEM((2, ATT_BLOCK, LANES), F32)],
        compiler_params=_cparams(("arbitrary", "arbitrary")),
        name=f"attention_d{dil}",
    )(*args)


def _attn_out_kernel(x_ref, shift_ref, scale_ref, gate_ref, o0_ref, o1_ref, o2_ref,
                     m0_ref, m1_ref, m2_ref, d0_ref, d1_ref, d2_ref, wz_ref, w_out_ref, out_ref):
    tm, D = x_ref.shape
    E = wz_ref.shape[1]
    x = x_ref[...]
    h = _mod_norm(x, scale_ref[...], shift_ref[...]).astype(BF16)
    z = _dot(h, wz_ref[...])
    m0, m1, m2 = m0_ref[...], m1_ref[...], m2_ref[...]
    m = jnp.maximum(jnp.maximum(m0, m1), m2)
    e0, e1, e2 = jnp.exp2(m0 - m), jnp.exp2(m1 - m), jnp.exp2(m2 - m)
    den = e0 * d0_ref[...] + e1 * d1_ref[...] + e2 * d2_ref[...]
    wg = (e0 / den, e1 / den, e2 / den)
    lo = lax.broadcasted_iota(jnp.int32, (tm, LANES), 1) < HEAD_DIM
    parts = []
    for hp in range(E // LANES):
        acc = None
        for w, o_ref in zip(wg, (o0_ref, o1_ref, o2_ref)):
            wexp = jnp.where(lo, w[:, 2 * hp:2 * hp + 1], w[:, 2 * hp + 1:2 * hp + 2])
            term = wexp * o_ref[hp]
            acc = term if acc is None else acc + term
        parts.append(acc)
    o = jnp.concatenate(parts, axis=-1)
    y = (o * _silu(z)).astype(BF16)
    out_ref[...] = x + gate_ref[...] * _dot(y, w_out_ref[...])


def _attn_out(x, mod, os_, ms, dens, w_z, w_out, tm=512):
    B, S, D = x.shape
    E = w_z.shape[1]
    tile = lambda w: pl.BlockSpec((None, tm, w), lambda b, i: (b, i, 0))
    mod_specs = [pl.BlockSpec((None, 1, D), functools.partial(lambda b, i, k: (b, 0, k), k=k))
                 for k in range(3)]
    return pl.pallas_call(
        _attn_out_kernel,
        grid=(B, S // tm),
        in_specs=[tile(D)] + mod_specs
                 + [pl.BlockSpec((None, E // LANES, tm, LANES), lambda b, i: (b, 0, i, 0))] * 3
                 + [tile(LANES)] * 6
                 + [_resident(w_z.shape), _resident(w_out.shape)],
        out_specs=pl.BlockSpec((tm, D), lambda b, i: (i, b)),
        out_shape=jax.ShapeDtypeStruct((S, B * D), F32),
        compiler_params=_cparams(("arbitrary", "arbitrary")),
        name="attention_out",
    )(x, mod, mod, mod, *os_, *ms, *dens, w_z, w_out).reshape(S, B, D)


def _s5_kernel(x_ref, shift_ref, scale_ref, gate_ref, w_in_ref, bbr_ref, bbi_ref, ar_ref, ai_ref,
               cr_ref, ci_ref, dskip_ref, w_glu_ref, b_glu_ref, w_out_ref,
               o_ref, state_ref, sr_ref, si_ref, *, scan_lanes):
    ts, B, D = x_ref.shape
    R = ts * B
    n_chunks, ck, cn = bbr_ref.shape
    E = n_chunks * ck

    @pl.when(pl.program_id(0) == 0)
    def _():
        state_ref[...] = jnp.zeros_like(state_ref)

    x3 = x_ref[...]
    ms = jnp.mean(x3 * x3, axis=-1, keepdims=True)
    h3 = (x3 * lax.rsqrt(ms + EPS)) * (1.0 + scale_ref[...][None]) + shift_ref[...][None]
    h = h3.reshape(R, D).astype(BF16)
    uz = _dot(h, w_in_ref[...])
    u, z = uz[:, :E], uz[:, E:]
    u16 = u.astype(BF16)

    def in_proj(c):
        uc = u16[:, c * ck:(c + 1) * ck]
        sr_ref[c] = _dot(uc, bbr_ref[c])
        si_ref[c] = _dot(uc, bbi_ref[c])

    def scan(c):
        for l0 in range(0, cn, scan_lanes):
            ls = slice(l0, l0 + scan_lanes)
            gl = slice(c * cn + l0, c * cn + l0 + scan_lanes)
            ar, ai = ar_ref[:, gl], ai_ref[:, gl]
            pr, pi = state_ref[0, :, gl], state_ref[1, :, gl]
            for t in range(ts):
                rows = slice(t * B, (t + 1) * B)
                nr = ar * pr - ai * pi + sr_ref[c, rows, ls]
                ni = ar * pi + ai * pr + si_ref[c, rows, ls]
                sr_ref[c, rows, ls] = nr
                si_ref[c, rows, ls] = ni
                pr, pi = nr, ni
            state_ref[0, :, gl] = pr
            state_ref[1, :, gl] = pi

    def out_proj(c):
        return _dot(sr_ref[c].astype(BF16), cr_ref[c]) - _dot(si_ref[c].astype(BF16), ci_ref[c])

    y_parts = []
    in_proj(0)
    for c in range(n_chunks):
        if c + 1 < n_chunks:
            in_proj(c + 1)
        scan(c)
        if c >= 1:
            y_parts.append(out_proj(c - 1))
    y_parts.append(out_proj(n_chunks - 1))
    y = jnp.concatenate(y_parts, axis=-1) + dskip_ref[...] * u
    g = y * (0.5 * (1.0 + jnp.tanh(math.sqrt(2.0 / math.pi) * (y + 0.044715 * (y * y * y)))))
    y = g * _sigmoid(_dot(g.astype(BF16), w_glu_ref[...]) + b_glu_ref[...])
    y = (y * _silu(z)).astype(BF16)
    out = _dot(y, w_out_ref[...]).reshape(ts, B, D)
    o_ref[...] = x3 + gate_ref[...][None] * out


def _s5_layer(xt, mod, w_in, bbr, bbi, ar, ai, cr, ci, d_skip, w_glu, b_glu, w_out, ts=64, scan_lanes=512):
    S, B, D = xt.shape
    E = w_glu.shape[0]
    N = ar.shape[1]
    mod_specs = [pl.BlockSpec((B, D), functools.partial(lambda i, k: (0, k), k=k)) for k in range(3)]
    weights = [w_in, bbr, bbi, ar, ai, cr, ci, d_skip.reshape(1, E), w_glu, b_glu.reshape(1, E), w_out]
    return pl.pallas_call(
        functools.partial(_s5_kernel, scan_lanes=scan_lanes),
        grid=(S // ts,),
        in_specs=[pl.BlockSpec((ts, B, D), lambda i: (i, 0, 0))] + mod_specs
                 + [_resident(w.shape) for w in weights],
        out_specs=pl.BlockSpec((ts, B, D), lambda i: (i, 0, 0)),
        out_shape=jax.ShapeDtypeStruct((S, B, D), F32),
        scratch_shapes=[pltpu.VMEM((2, B, N), F32),
                        pltpu.VMEM((bbr.shape[0], ts * B, bbr.shape[2]), F32),
                        pltpu.VMEM((bbr.shape[0], ts * B, bbr.shape[2]), F32)],
        compiler_params=_cparams(("arbitrary",)),
        name="s5_layer",
    )(xt, mod, mod, mod, *weights)


def _s5_params(log_dt, lambda_re, lambda_im, b_re, b_im, c_re, c_im, B, groups_per_chunk=16):
    G, P = lambda_re.shape
    K = b_re.shape[2]
    dt = jnp.exp(log_dt.astype(F32))[:, None]
    lr = lambda_re.astype(F32)
    li = lambda_im.astype(F32)
    mag = jnp.exp(lr * dt)
    ar = mag * jnp.cos(li * dt)
    ai = mag * jnp.sin(li * dt)
    inv = 1.0 / (lr * lr + li * li)
    qr = ((ar - 1.0) * lr + ai * li) * inv
    qi = (ai * lr - (ar - 1.0) * li) * inv
    br = b_re.astype(F32)
    bi = b_im.astype(F32)
    bbr = qr[..., None] * br - qi[..., None] * bi
    bbi = qr[..., None] * bi + qi[..., None] * br
    gc = groups_per_chunk
    nc = G // gc
    eye = jnp.eye(gc, dtype=F32)

    def in_blocks(m):
        m = m.reshape(nc, gc, P, K)
        return jnp.einsum('cgpk,gh->cgkhp', m, eye).reshape(nc, gc * K, gc * P).astype(BF16)

    def out_blocks(m):
        m = m.astype(F32).reshape(nc, gc, K, P)
        return jnp.einsum('cgkp,gh->cgphk', m, eye).reshape(nc, gc * P, gc * K).astype(BF16)

    bcast = lambda a: jnp.broadcast_to(a.reshape(1, G * P), (B, G * P))
    return in_blocks(bbr), in_blocks(bbi), bcast(ar), bcast(ai), out_blocks(c_re), out_blocks(c_im)


def kernel(x, c, rel_bias, final_g, l0_ada_w, l0_ada_b, l0_w_in, l0_conv_w, l0_conv_b, l0_w_out, l1_ada_w, l1_ada_b, l1_w_in, l1_w_out, l2_ada_w, l2_ada_b, l2_w_in, l2_log_dt, l2_lambda_re, l2_lambda_im, l2_b_re, l2_b_im, l2_c_re, l2_c_im, l2_d_skip, l2_w_glu, l2_b_glu, l2_w_out, l3_ada_w, l3_ada_b, l3_w_in, l3_conv_w, l3_conv_b, l3_w_out):
    B, S, D = x.shape
    assert B == SUBLANES, "the S5 scan keeps the batch on the sublane axis"
    mods = _modulation(c, (l0_ada_w, l1_ada_w, l2_ada_w, l3_ada_w), (l0_ada_b, l1_ada_b, l2_ada_b, l3_ada_b))
    mod3 = [m.reshape(B, 1, 3 * D) for m in mods]

    x = _conv_layer(x, mod3[0], l0_w_in.astype(BF16), l0_conv_w, l0_conv_b, l0_w_out.astype(BF16),
                    time_major_in=False)

    n_groups = len(DILATION_PATTERNS)
    dils = tuple(dil for _, dil in DILATION_PATTERNS)
    E = l1_w_out.shape[0]
    H = E // HEAD_DIM
    w1 = l1_w_in.astype(BF16)
    qkv = _qkv_proj(x, mod3[1], w1[:, :3 * E * n_groups], dils, E)
    os_, ms, dens = [], [], []
    for g, (window, dil) in enumerate(DILATION_PATTERNS):
        o, m, den = _attention(*qkv[3 * g:3 * g + 3], rel_bias[:, g * H:(g + 1) * H], window, dil)
        os_.append(o)
        ms.append(m)
        dens.append(den)
    xt = _attn_out(x, mod3[1], os_, ms, dens, w1[:, 3 * E * n_groups:], l1_w_out.astype(BF16))

    ssm = _s5_params(l2_log_dt, l2_lambda_re, l2_lambda_im, l2_b_re, l2_b_im, l2_c_re, l2_c_im, B)
    xt = _s5_layer(xt, mods[2], l2_w_in.astype(BF16), *ssm, l2_d_skip, l2_w_glu.astype(BF16), l2_b_glu,
                   l2_w_out.astype(BF16))

    return _conv_layer(xt, mod3[3], l3_w_in.astype(BF16), l3_conv_w, l3_conv_b, l3_w_out.astype(BF16),
                       time_major_in=True, final_g=final_g)
```

```python
import functools
import math

import numpy as np
import jax
import jax.numpy as jnp
from jax import lax
from jax.experimental import pallas as pl
from jax.experimental.pallas import tpu as pltpu

EPS = 1e-6
HEAD_DIM = 64
ATT_BLOCK = 128
DILATION_PATTERNS = ((128, 1), (512, 4), (2048, 16))
REL_BUCKETS = 32
REL_MAX_DIST = 2048
NEG_INF = -1e30
LOG2E = math.log2(math.e)
LANES = 128
SUBLANES = 8
VMEM_LIMIT_BYTES = 56 * 1024 * 1024

F32 = jnp.float32
BF16 = jnp.bfloat16


def _cparams(semantics):
    return pltpu.CompilerParams(dimension_semantics=semantics, vmem_limit_bytes=VMEM_LIMIT_BYTES)


def _resident(shape):
    nd = len(shape)
    return pl.BlockSpec(shape, lambda *_: (0,) * nd, pipeline_mode=pl.Buffered(1))


def _silu(z):
    return z * (1.0 / (1.0 + jnp.exp(-z)))


def _sigmoid(z):
    return 1.0 / (1.0 + jnp.exp(-z))


def _mod_norm(x, scale, shift):
    ms = jnp.mean(x * x, axis=-1, keepdims=True)
    return (x * lax.rsqrt(ms + EPS)) * (1.0 + scale) + shift


def _dot(a, b):
    return jnp.dot(a, b, preferred_element_type=F32)


def _rows(start, size, stride):
    return pl.ds(start, size) if stride == 1 else pl.ds(start, size, stride=stride)


def _modulation_kernel(c_ref, w0, w1, w2, w3, b0, b1, b2, b3, o0, o1, o2, o3):
    sc = _silu(c_ref[...]).astype(BF16)
    for w, b, o in ((w0, b0, o0), (w1, b1, o1), (w2, b2, o2), (w3, b3, o3)):
        o[...] = _dot(sc, w[...].astype(BF16)) + b[...]


def _modulation(c, ws, bs, tn=512):
    B, D = c.shape
    N = ws[0].shape[1]
    w_spec = pl.BlockSpec((D, tn), lambda j: (0, j))
    b_spec = pl.BlockSpec((1, tn), lambda j: (0, j))
    o_spec = pl.BlockSpec((B, tn), lambda j: (0, j))
    return pl.pallas_call(
        _modulation_kernel,
        grid=(N // tn,),
        in_specs=[pl.BlockSpec((B, D), lambda j: (0, 0))] + [w_spec] * 4 + [b_spec] * 4,
        out_specs=[o_spec] * 4,
        out_shape=[jax.ShapeDtypeStruct((B, N), F32)] * 4,
        compiler_params=_cparams(("arbitrary",)),
        name="modulation",
    )(c, *ws, *[b.reshape(1, N) for b in bs])


def _conv_layer_kernel(*refs, ec, final):
    if final:
        (x_ref, shift_ref, scale_ref, gate_ref, w_in_ref, cw_ref, cb_ref, w_out_ref, g_ref,
         o_ref, carry_ref, vs_ref) = refs
    else:
        (x_ref, shift_ref, scale_ref, gate_ref, w_in_ref, cw_ref, cb_ref, w_out_ref,
         o_ref, carry_ref, vs_ref) = refs
    tm, D = x_ref.shape
    E = w_out_ref.shape[0]

    @pl.when(pl.program_id(1) == 0)
    def _():
        carry_ref[...] = jnp.zeros_like(carry_ref)

    x = x_ref[...]
    h = _mod_norm(x, scale_ref[...], shift_ref[...]).astype(BF16)
    acc = jnp.zeros((tm, D), F32)
    for j in range(E // ec):
        c0 = j * ec
        u = _dot(h, w_in_ref[:, c0:c0 + ec])
        gc = _dot(h, w_in_ref[:, E + c0:E + c0 + ec])
        gb = _dot(h, w_in_ref[:, 2 * E + c0:2 * E + c0 + ec])
        z = _dot(h, w_in_ref[:, 3 * E + c0:3 * E + c0 + ec])
        v = gc * u
        vs_ref[0:SUBLANES, :] = carry_ref[j]
        vs_ref[SUBLANES:SUBLANES + tm, :] = v
        carry_ref[j] = v[tm - SUBLANES:tm, :]
        v1 = vs_ref[SUBLANES - 1:SUBLANES - 1 + tm, :]
        v2 = vs_ref[SUBLANES - 2:SUBLANES - 2 + tm, :]
        cw = cw_ref[:, c0:c0 + ec]
        conv = cw[0:1, :] * v2 + cw[1:2, :] * v1 + cw[2:3, :] * v + cb_ref[:, c0:c0 + ec]
        y = gb * conv * _silu(z)
        acc = acc + _dot(y.astype(BF16), w_out_ref[c0:c0 + ec, :])
    out = x + gate_ref[...] * acc
    if final:
        ms = jnp.mean(out * out, axis=-1, keepdims=True)
        out = out * lax.rsqrt(ms + EPS) * g_ref[...]
    o_ref[...] = out


def _conv_layer(x, mod, w_in, conv_w, conv_b, w_out, *, time_major_in, final_g=None, tm=512, ec=512):
    if time_major_in:
        S, B, D = x.shape
        x = x.reshape(S, B * D)
        x_spec = pl.BlockSpec((tm, D), lambda b, i: (i, b))
    else:
        B, S, D = x.shape
        x_spec = pl.BlockSpec((None, tm, D), lambda b, i: (b, i, 0))
    E = w_out.shape[0]
    final = final_g is not None
    mod_specs = [pl.BlockSpec((None, 1, D), functools.partial(lambda b, i, k: (b, 0, k), k=k))
                 for k in range(3)]
    in_specs = [x_spec] + mod_specs + [_resident(w_in.shape), _resident(conv_w.shape),
                                       _resident((1, E)), _resident(w_out.shape)]
    args = [x, mod, mod, mod, w_in, conv_w, conv_b.reshape(1, E), w_out]
    if final:
        in_specs.append(_resident((1, D)))
        args.append(final_g.reshape(1, D))
    return pl.pallas_call(
        functools.partial(_conv_layer_kernel, ec=ec, final=final),
        grid=(B, S // tm),
        in_specs=in_specs,
        out_specs=pl.BlockSpec((None, tm, D), lambda b, i: (b, i, 0)),
        out_shape=jax.ShapeDtypeStruct((B, S, D), F32),
        scratch_shapes=[pltpu.VMEM((E // ec, SUBLANES, ec), F32),
                        pltpu.VMEM((SUBLANES + tm, ec), F32)],
        compiler_params=_cparams(("arbitrary", "arbitrary")),
        name="conv_layer_final" if final else "conv_layer",
    )(*args)


def _t5_bucket(dist):
    max_exact = REL_BUCKETS // 2
    d = np.maximum(dist, 0)
    ratio = np.log(np.maximum(d, 1) / max_exact) / math.log(REL_MAX_DIST / max_exact)
    large = np.minimum(max_exact + (ratio * (REL_BUCKETS - max_exact)).astype(np.int64), REL_BUCKETS - 1)
    return np.where(d < max_exact, d, large).astype(np.int32)


def _band_structure(window, dil):
    wsub = window // dil
    qi = np.arange(ATT_BLOCK)[:, None]
    ki = np.arange(2 * ATT_BLOCK)[None, :]
    dist = ATT_BLOCK + qi - ki
    in_band = (dist >= 0) & (dist <= wsub)
    first = in_band & (ki >= ATT_BLOCK)
    return _t5_bucket(dist * dil), np.stack([in_band, first]).astype(np.float32)


def _qkv_kernel(x_ref, shift_ref, scale_ref, w_ref, *refs, dils):
    out_refs, h_ref = refs[:-1], refs[-1]
    tn, D = x_ref.shape
    E = out_refs[0].shape[-1]
    h = _mod_norm(x_ref[...], scale_ref[...], shift_ref[...])
    n_col = D // LANES
    for c in range(n_col):
        h_ref[c] = h[:, c * LANES:(c + 1) * LANES]
    for g, dil in enumerate(dils):
        n = tn // dil
        if dil == 1:
            hp = h
        else:
            hp = jnp.concatenate(
                [jnp.concatenate([h_ref[c, pl.ds(r, n, stride=dil), :] for c in range(n_col)], axis=1)
                 for r in range(dil)], axis=0)
        hp = hp.astype(BF16)
        for j in range(3):
            c0 = (3 * g + j) * E
            res = _dot(hp, w_ref[:, c0:c0 + E])
            if j == 0:
                res = res * (LOG2E / math.sqrt(HEAD_DIM))
            out_refs[3 * g + j][...] = res.astype(BF16).reshape(dil, n, E)


def _qkv_proj(x, mod, w_qkv, dils, E, tn=512):
    B, S, D = x.shape
    mod_specs = [pl.BlockSpec((None, 1, D), functools.partial(lambda b, i, k: (b, 0, k), k=k))
                 for k in range(2)]
    out_specs, out_shape = [], []
    for dil in dils:
        out_specs += [pl.BlockSpec((None, dil, tn // dil, E), lambda b, i: (b, 0, i, 0))] * 3
        out_shape += [jax.ShapeDtypeStruct((B, dil, S // dil, E), BF16)] * 3
    return pl.pallas_call(
        functools.partial(_qkv_kernel, dils=dils),
        grid=(B, S // tn),
        in_specs=[pl.BlockSpec((None, tn, D), lambda b, i: (b, i, 0))] + mod_specs
                 + [_resident(w_qkv.shape)],
        out_specs=out_specs,
        out_shape=out_shape,
        scratch_shapes=[pltpu.VMEM((D // LANES, tn, LANES), F32)],
        compiler_params=_cparams(("arbitrary", "arbitrary")),
        name="qkv_proj",
    )(x, mod, mod, w_qkv)


def _attn_kernel(*refs, nr, nsub, has_prev, stride, n_heads):
    if has_prev:
        (q_ref, k_ref, v_ref, kp_ref, vp_ref, rb_ref, bucket_ref, mask_ref,
         o_ref, m_ref, den_ref, tbl_ref, s_ref, p_ref, stat_ref) = refs
    else:
        (q_ref, k_ref, v_ref, rb_ref, bucket_ref, mask_ref,
         o_ref, m_ref, den_ref, tbl_ref, s_ref, p_ref, stat_ref) = refs
    n_pairs = n_heads // 2
    pair_rows = 2 * ATT_BLOCK
    nk = s_ref.shape[1]
    step = pl.program_id(1)

    @pl.when((pl.program_id(0) == 0) & (step == 0))
    def _():
        def fill(b, carry):
            hit = bucket_ref[...] == b
            for h in range(n_heads):
                rows = slice(h * ATT_BLOCK, (h + 1) * ATT_BLOCK)
                s_ref[rows, :] = jnp.where(hit, rb_ref[b, h] * LOG2E, s_ref[rows, :])
            return carry
        s_ref[...] = jnp.zeros_like(s_ref)
        lax.fori_loop(0, REL_BUCKETS, fill, 0)
        for t in range(tbl_ref.shape[0]):
            for h in range(n_heads):
                rows = slice(h * ATT_BLOCK, (h + 1) * ATT_BLOCK)
                tbl_ref[t, rows, :] = jnp.where(mask_ref[t] > 0.0, s_ref[rows, :], NEG_INF)
        stat_ref[0] = jnp.zeros((ATT_BLOCK, LANES), F32)
        stat_ref[1] = jnp.ones((ATT_BLOCK, LANES), F32)

    lane = lax.broadcasted_iota(jnp.int32, (ATT_BLOCK, LANES), 1)
    lo = lane < HEAD_DIM

    def block(q_of, k_of, v_of, table, out_rows):
        for hp in range(n_pairs):
            cols = slice(hp * LANES, (hp + 1) * LANES)
            rows = slice(hp * pair_rows, (hp + 1) * pair_rows)
            q2 = q_of(cols)
            zero = jnp.zeros_like(q2)
            q_pair = jnp.concatenate([jnp.where(lo, q2, zero), jnp.where(lo, zero, q2)], axis=0)
            s_ref[rows, :] = lax.dot_general(q_pair, k_of(cols), (((1,), (1,)), ((), ())),
                                             preferred_element_type=F32)
        head_rows = [slice(h * ATT_BLOCK, (h + 1) * ATT_BLOCK) for h in range(n_heads)]
        if not has_prev:
            maxima = [jnp.max(s_ref[rows, :] + tbl_ref[table, rows, :], axis=-1, keepdims=True)
                      for rows in head_rows]
        for h, rows in enumerate(head_rows):
            s = s_ref[rows, :] + tbl_ref[table, rows, :]
            m = jnp.max(s, axis=-1, keepdims=True) if has_prev else maxima[h]
            p = jnp.exp2(s - m)
            den = jnp.sum(p, axis=-1, keepdims=True)
            p_ref[rows, :] = p.astype(BF16)
            stat_ref[0, :, h:h + 1] = m
            stat_ref[1, :, h:h + 1] = den
        m_ref[out_rows, :] = stat_ref[0]
        den_ref[out_rows, :] = stat_ref[1]
        for hp in range(n_pairs):
            cols = slice(hp * LANES, (hp + 1) * LANES)
            rows = slice(hp * pair_rows, (hp + 1) * pair_rows)
            res = _dot(p_ref[rows, :], v_of(cols))
            o_ref[hp, out_rows, :] = jnp.where(lo, res[:ATT_BLOCK], res[ATT_BLOCK:])

    for r in range(nr):
        for j in range(nsub):
            row = j * ATT_BLOCK
            q_of = lambda cols, r=r, row=row: q_ref[r, row:row + ATT_BLOCK, cols]
            if has_prev:
                out_rows = _rows(row * stride + r, ATT_BLOCK, stride)
                if j == 0:
                    k_of = lambda cols, r=r: jnp.concatenate([kp_ref[r, :, cols], k_ref[r, 0:ATT_BLOCK, cols]], axis=0)
                    v_of = lambda cols, r=r: jnp.concatenate([vp_ref[r, :, cols], v_ref[r, 0:ATT_BLOCK, cols]], axis=0)
                    table = jnp.where(step == 0, 1, 0)
                else:
                    k_of = lambda cols, r=r, row=row: k_ref[r, row - ATT_BLOCK:row + ATT_BLOCK, cols]
                    v_of = lambda cols, r=r, row=row: v_ref[r, row - ATT_BLOCK:row + ATT_BLOCK, cols]
                    table = 0
            else:
                out_rows = _rows(step * nr + r, ATT_BLOCK, stride)
                k_of = lambda cols, r=r: k_ref[r, :, cols]
                v_of = lambda cols, r=r: v_ref[r, :, cols]
                table = 0
            block(q_of, k_of, v_of, table, out_rows)


def _attention(q, k, v, rel_bias_g, window, dil, tokens_per_step=512):
    B, _, L, E = q.shape
    S = L * dil
    H = E // HEAD_DIM
    bucket, masks = _band_structure(window, dil)
    has_prev = L > ATT_BLOCK
    if has_prev:
        nr = dil
        nsub = tokens_per_step // (dil * ATT_BLOCK)
        tq = nsub * ATT_BLOCK
        n_steps = L // tq
        nk = 2 * ATT_BLOCK
        qkv_spec = pl.BlockSpec((None, nr, tq, E), lambda b, i: (b, 0, i, 0))
        prev_spec = pl.BlockSpec((None, nr, ATT_BLOCK, E),
                                 lambda b, i: (b, 0, jnp.maximum(i * nsub - 1, 0), 0))
        in_specs = [qkv_spec] * 3 + [prev_spec] * 2
        args = [q, k, v, k, v]
        out_block_rows = tokens_per_step
        out_step = lambda i: i
    else:
        nr = 4
        nsub = 1
        n_steps = dil // nr
        nk = ATT_BLOCK
        bucket, masks = bucket[:, ATT_BLOCK:], masks[1:, :, ATT_BLOCK:]
        qkv_spec = pl.BlockSpec((None, nr, ATT_BLOCK, E), lambda b, i: (b, i, 0, 0))
        in_specs = [qkv_spec] * 3
        args = [q, k, v]
        out_block_rows = S
        out_step = lambda i: 0
    in_specs += [pl.BlockSpec(memory_space=pltpu.SMEM), _resident(bucket.shape), _resident(masks.shape)]
    args += [rel_bias_g, jnp.asarray(bucket), jnp.asarray(masks)]
    stat_spec = pl.BlockSpec((None, out_block_rows, LANES), lambda b, i: (b, out_step(i), 0))
    stat_shape = jax.ShapeDtypeStruct((B, S, LANES), F32)
    return pl.pallas_call(
        functools.partial(_attn_kernel, nr=nr, nsub=nsub, has_prev=has_prev, stride=dil, n_heads=H),
        grid=(B, n_steps),
        in_specs=in_specs,
        out_specs=[pl.BlockSpec((None, E // LANES, out_block_rows, LANES), lambda b, i: (b, 0, out_step(i), 0)),
                   stat_spec, stat_spec],
        out_shape=[jax.ShapeDtypeStruct((B, E // LANES, S, LANES), F32), stat_shape, stat_shape],
        scratch_shapes=[pltpu.VMEM((masks.shape[0], H * ATT_BLOCK, nk), F32),
                        pltpu.VMEM((H * ATT_BLOCK, nk), F32),
                        pltpu.VMEM((H * ATT_BLOCK, nk), BF16),
                        pltpu.VMEM((2, ATT_BLOCK, LANES), F32)],
        compiler_params=_cparams(("arbitrary", "arbitrary")),
        name=f"attention_d{dil}",
    )(*args)


def _attn_out_kernel(x_ref, shift_ref, scale_ref, gate_ref, o0_ref, o1_ref, o2_ref,
                     m0_ref, m1_ref, m2_ref, d0_ref, d1_ref, d2_ref, wz_ref, w_out_ref, expand_ref,
                     out_ref):
    tm, D = x_ref.shape
    E = wz_ref.shape[1]
    n_heads = E // HEAD_DIM
    x = x_ref[...]
    h = _mod_norm(x, scale_ref[...], shift_ref[...]).astype(BF16)
    z = _dot(h, wz_ref[...])
    m0, m1, m2 = m0_ref[...], m1_ref[...], m2_ref[...]
    m = jnp.maximum(jnp.maximum(m0, m1), m2)
    e0, e1, e2 = jnp.exp2(m0 - m), jnp.exp2(m1 - m), jnp.exp2(m2 - m)
    den = e0 * d0_ref[...] + e1 * d1_ref[...] + e2 * d2_ref[...]
    head_lane = lax.broadcasted_iota(jnp.int32, (tm, LANES), 1) < n_heads
    packed = None
    for g, e in enumerate((e0, e1, e2)):
        w = jnp.where(head_lane, e / den, 0.0)
        hi = w.astype(BF16).astype(F32)
        for part, val in enumerate((hi, w - hi)):
            shift = (2 * g + part) * n_heads
            val = pltpu.roll(val, shift, axis=1) if shift else val
            packed = val if packed is None else packed + val
    wexp = _dot(packed.astype(BF16), expand_ref[...])
    o = None
    for g, o_ref in enumerate((o0_ref, o1_ref, o2_ref)):
        og = jnp.concatenate([o_ref[hp] for hp in range(E // LANES)], axis=-1)
        term = wexp[:, g * E:(g + 1) * E] * og
        o = term if o is None else o + term
    y = (o * _silu(z)).astype(BF16)
    out_ref[...] = x + gate_ref[...] * _dot(y, w_out_ref[...])


def _attn_out(x, mod, os_, ms, dens, w_z, w_out, tm=512):
    B, S, D = x.shape
    E = w_z.shape[1]
    tile = lambda w: pl.BlockSpec((None, tm, w), lambda b, i: (b, i, 0))
    mod_specs = [pl.BlockSpec((None, 1, D), functools.partial(lambda b, i, k: (b, 0, k), k=k))
                 for k in range(3)]
    n_groups, H = len(os_), E // HEAD_DIM
    expand = np.zeros((LANES, n_groups * E), np.float32)
    for g in range(n_groups):
        for part in range(2):
            for hd in range(H):
                expand[(2 * g + part) * H + hd, g * E + hd * HEAD_DIM:g * E + (hd + 1) * HEAD_DIM] = 1.0
    expand = jnp.asarray(expand, BF16)
    return pl.pallas_call(
        _attn_out_kernel,
        grid=(B, S // tm),
        in_specs=[tile(D)] + mod_specs
                 + [pl.BlockSpec((None, E // LANES, tm, LANES), lambda b, i: (b, 0, i, 0))] * 3
                 + [tile(LANES)] * 6
                 + [_resident(w_z.shape), _resident(w_out.shape), _resident(expand.shape)],
        out_specs=pl.BlockSpec((tm, D), lambda b, i: (i, b)),
        out_shape=jax.ShapeDtypeStruct((S, B * D), F32),
        compiler_params=_cparams(("arbitrary", "arbitrary")),
        name="attention_out",
    )(x, mod, mod, mod, *os_, *ms, *dens, w_z, w_out, expand).reshape(S, B, D)


def _s5_kernel(x_ref, shift_ref, scale_ref, gate_ref, w_in_ref, bbr_ref, bbi_ref, ar_ref, ai_ref,
               cr_ref, ci_ref, dskip_ref, w_glu_ref, b_glu_ref, w_out_ref,
               o_ref, state_ref, sr_ref, si_ref, *, scan_lanes):
    ts, B, D = x_ref.shape
    R = ts * B
    n_chunks, ck, cn = bbr_ref.shape
    E = n_chunks * ck

    @pl.when(pl.program_id(0) == 0)
    def _():
        state_ref[...] = jnp.zeros_like(state_ref)

    x3 = x_ref[...]
    ms = jnp.mean(x3 * x3, axis=-1, keepdims=True)
    h3 = (x3 * lax.rsqrt(ms + EPS)) * (1.0 + scale_ref[...][None]) + shift_ref[...][None]
    h = h3.reshape(R, D).astype(BF16)
    uz = _dot(h, w_in_ref[...])
    u, z = uz[:, :E], uz[:, E:]
    u16 = u.astype(BF16)

    def in_proj(c):
        uc = u16[:, c * ck:(c + 1) * ck]
        sr_ref[c] = _dot(uc, bbr_ref[c])
        si_ref[c] = _dot(uc, bbi_ref[c])

    def scan(c):
        for l0 in range(0, cn, scan_lanes):
            ls = slice(l0, l0 + scan_lanes)
            gl = slice(c * cn + l0, c * cn + l0 + scan_lanes)
            ar, ai = ar_ref[:, gl], ai_ref[:, gl]
            pr, pi = state_ref[0, :, gl], state_ref[1, :, gl]
            for t in range(ts):
                rows = slice(t * B, (t + 1) * B)
                nr = ar * pr - ai * pi + sr_ref[c, rows, ls]
                ni = ar * pi + ai * pr + si_ref[c, rows, ls]
                sr_ref[c, rows, ls] = nr
                si_ref[c, rows, ls] = ni
                pr, pi = nr, ni
            state_ref[0, :, gl] = pr
            state_ref[1, :, gl] = pi

    def out_proj(c):
        return _dot(sr_ref[c].astype(BF16), cr_ref[c]) - _dot(si_ref[c].astype(BF16), ci_ref[c])

    y_parts = []
    in_proj(0)
    for c in range(n_chunks):
        if c + 1 < n_chunks:
            in_proj(c + 1)
        scan(c)
        if c >= 1:
            y_parts.append(out_proj(c - 1))
    y_parts.append(out_proj(n_chunks - 1))
    y = jnp.concatenate(y_parts, axis=-1) + dskip_ref[...] * u
    g = y * (0.5 * (1.0 + jnp.tanh(math.sqrt(2.0 / math.pi) * (y + 0.044715 * (y * y * y)))))
    y = g * _sigmoid(_dot(g.astype(BF16), w_glu_ref[...]) + b_glu_ref[...])
    y = (y * _silu(z)).astype(BF16)
    out = _dot(y, w_out_ref[...]).reshape(ts, B, D)
    o_ref[...] = x3 + gate_ref[...][None] * out


def _s5_layer(xt, mod, w_in, bbr, bbi, ar, ai, cr, ci, d_skip, w_glu, b_glu, w_out, ts=64, scan_lanes=512):
    S, B, D = xt.shape
    E = w_glu.shape[0]
    N = ar.shape[1]
    mod_specs = [pl.BlockSpec((B, D), functools.partial(lambda i, k: (0, k), k=k)) for k in range(3)]
    weights = [w_in, bbr, bbi, ar, ai, cr, ci, d_skip.reshape(1, E), w_glu, b_glu.reshape(1, E), w_out]
    return pl.pallas_call(
        functools.partial(_s5_kernel, scan_lanes=scan_lanes),
        grid=(S // ts,),
        in_specs=[pl.BlockSpec((ts, B, D), lambda i: (i, 0, 0))] + mod_specs
                 + [_resident(w.shape) for w in weights],
        out_specs=pl.BlockSpec((ts, B, D), lambda i: (i, 0, 0)),
        out_shape=jax.ShapeDtypeStruct((S, B, D), F32),
        scratch_shapes=[pltpu.VMEM((2, B, N), F32),
                        pltpu.VMEM((bbr.shape[0], ts * B, bbr.shape[2]), F32),
                        pltpu.VMEM((bbr.shape[0], ts * B, bbr.shape[2]), F32)],
        compiler_params=_cparams(("arbitrary",)),
        name="s5_layer",
    )(xt, mod, mod, mod, *weights)


def _s5_params(log_dt, lambda_re, lambda_im, b_re, b_im, c_re, c_im, B, groups_per_chunk=16):
    G, P = lambda_re.shape
    K = b_re.shape[2]
    dt = jnp.exp(log_dt.astype(F32))[:, None]
    lr = lambda_re.astype(F32)
    li = lambda_im.astype(F32)
    mag = jnp.exp(lr * dt)
    ar = mag * jnp.cos(li * dt)
    ai = mag * jnp.sin(li * dt)
    inv = 1.0 / (lr * lr + li * li)
    qr = ((ar - 1.0) * lr + ai * li) * inv
    qi = (ai * lr - (ar - 1.0) * li) * inv
    br = b_re.astype(F32)
    bi = b_im.astype(F32)
    bbr = qr[..., None] * br - qi[..., None] * bi
    bbi = qr[..., None] * bi + qi[..., None] * br
    gc = groups_per_chunk
    nc = G // gc
    eye = jnp.eye(gc, dtype=F32)

    def in_blocks(m):
        m = m.reshape(nc, gc, P, K)
        return jnp.einsum('cgpk,gh->cgkhp', m, eye).reshape(nc, gc * K, gc * P).astype(BF16)

    def out_blocks(m):
        m = m.astype(F32).reshape(nc, gc, K, P)
        return jnp.einsum('cgkp,gh->cgphk', m, eye).reshape(nc, gc * P, gc * K).astype(BF16)

    bcast = lambda a: jnp.broadcast_to(a.reshape(1, G * P), (B, G * P))
    return in_blocks(bbr), in_blocks(bbi), bcast(ar), bcast(ai), out_blocks(c_re), out_blocks(c_im)


def kernel(x, c, rel_bias, final_g, l0_ada_w, l0_ada_b, l0_w_in, l0_conv_w, l0_conv_b, l0_w_out, l1_ada_w, l1_ada_b, l1_w_in, l1_w_out, l2_ada_w, l2_ada_b, l2_w_in, l2_log_dt, l2_lambda_re, l2_lambda_im, l2_b_re, l2_b_im, l2_c_re, l2_c_im, l2_d_skip, l2_w_glu, l2_b_glu, l2_w_out, l3_ada_w, l3_ada_b, l3_w_in, l3_conv_w, l3_conv_b, l3_w_out):
    B, S, D = x.shape
    assert B == SUBLANES, "the S5 scan keeps the batch on the sublane axis"
    mods = _modulation(c, (l0_ada_w, l1_ada_w, l2_ada_w, l3_ada_w), (l0_ada_b, l1_ada_b, l2_ada_b, l3_ada_b))
    mod3 = [m.reshape(B, 1, 3 * D) for m in mods]

    x = _conv_layer(x, mod3[0], l0_w_in.astype(BF16), l0_conv_w, l0_conv_b, l0_w_out.astype(BF16),
                    time_major_in=False)

    n_groups = len(DILATION_PATTERNS)
    dils = tuple(dil for _, dil in DILATION_PATTERNS)
    E = l1_w_out.shape[0]
    H = E // HEAD_DIM
    w1 = l1_w_in.astype(BF16)
    qkv = _qkv_proj(x, mod3[1], w1[:, :3 * E * n_groups], dils, E)
    os_, ms, dens = [], [], []
    for g, (window, dil) in enumerate(DILATION_PATTERNS):
        o, m, den = _attention(*qkv[3 * g:3 * g + 3], rel_bias[:, g * H:(g + 1) * H], window, dil)
        os_.append(o)
        ms.append(m)
        dens.append(den)
    xt = _attn_out(x, mod3[1], os_, ms, dens, w1[:, 3 * E * n_groups:], l1_w_out.astype(BF16))

    ssm = _s5_params(l2_log_dt, l2_lambda_re, l2_lambda_im, l2_b_re, l2_b_im, l2_c_re, l2_c_im, B)
    xt = _s5_layer(xt, mods[2], l2_w_in.astype(BF16), *ssm, l2_d_skip, l2_w_glu.astype(BF16), l2_b_glu,
                   l2_w_out.astype(BF16))

    return _conv_layer(xt, mod3[3], l3_w_in.astype(BF16), l3_conv_w, l3_conv_b, l3_w_out.astype(BF16),
                       time_major_in=True, final_g=final_g)
```

```python
import functools
import math

import numpy as np
import jax
import jax.numpy as jnp
from jax import lax
from jax.experimental import pallas as pl
from jax.experimental.pallas import tpu as pltpu

EPS = 1e-6
HEAD_DIM = 64
ATT_BLOCK = 128
DILATION_PATTERNS = ((128, 1), (512, 4), (2048, 16))
REL_BUCKETS = 32
REL_MAX_DIST = 2048
NEG_INF = -1e30
LOG2E = math.log2(math.e)
LANES = 128
SUBLANES = 8
VMEM_LIMIT_BYTES = 56 * 1024 * 1024

F32 = jnp.float32
BF16 = jnp.bfloat16


def _cparams(semantics):
    return pltpu.CompilerParams(dimension_semantics=semantics, vmem_limit_bytes=VMEM_LIMIT_BYTES)


def _resident(shape):
    nd = len(shape)
    return pl.BlockSpec(shape, lambda *_: (0,) * nd, pipeline_mode=pl.Buffered(1))


def _silu(z):
    return z * (1.0 / (1.0 + jnp.exp(-z)))


def _sigmoid(z):
    return 1.0 / (1.0 + jnp.exp(-z))


def _mod_norm(x, scale, shift):
    ms = jnp.mean(x * x, axis=-1, keepdims=True)
    return (x * lax.rsqrt(ms + EPS)) * (1.0 + scale) + shift


def _dot(a, b):
    return jnp.dot(a, b, preferred_element_type=F32)


def _rows(start, size, stride):
    return pl.ds(start, size) if stride == 1 else pl.ds(start, size, stride=stride)


def _modulation_kernel(c_ref, w0, w1, w2, w3, b0, b1, b2, b3, o0, o1, o2, o3):
    sc = _silu(c_ref[...]).astype(BF16)
    for w, b, o in ((w0, b0, o0), (w1, b1, o1), (w2, b2, o2), (w3, b3, o3)):
        o[...] = _dot(sc, w[...].astype(BF16)) + b[...]


def _modulation(c, ws, bs, tn=512):
    B, D = c.shape
    N = ws[0].shape[1]
    w_spec = pl.BlockSpec((D, tn), lambda j: (0, j))
    b_spec = pl.BlockSpec((1, tn), lambda j: (0, j))
    o_spec = pl.BlockSpec((B, tn), lambda j: (0, j))
    return pl.pallas_call(
        _modulation_kernel,
        grid=(N // tn,),
        in_specs=[pl.BlockSpec((B, D), lambda j: (0, 0))] + [w_spec] * 4 + [b_spec] * 4,
        out_specs=[o_spec] * 4,
        out_shape=[jax.ShapeDtypeStruct((B, N), F32)] * 4,
        compiler_params=_cparams(("arbitrary",)),
        name="modulation",
    )(c, *ws, *[b.reshape(1, N) for b in bs])


def _conv_layer_kernel(*refs, ec, sub, final):
    if final:
        (x_ref, shift_ref, scale_ref, gate_ref, w_in_ref, cw_ref, cb_ref, w_out_ref, g_ref,
         o_ref, vs_ref) = refs
    else:
        (x_ref, shift_ref, scale_ref, gate_ref, w_in_ref, cw_ref, cb_ref, w_out_ref,
         o_ref, vs_ref) = refs
    tm, D = x_ref.shape
    E = w_out_ref.shape[0]

    @pl.when(pl.program_id(1) == 0)
    def _():
        vs_ref[:, 0:SUBLANES, :] = jnp.zeros((E // ec, SUBLANES, ec), F32)

    for r0 in range(0, tm, sub):
        x = x_ref[r0:r0 + sub, :]
        h = _mod_norm(x, scale_ref[...], shift_ref[...]).astype(BF16)
        acc = jnp.zeros((sub, D), F32)
        for j in range(E // ec):
            c0 = j * ec
            u = _dot(h, w_in_ref[:, c0:c0 + ec])
            gc = _dot(h, w_in_ref[:, E + c0:E + c0 + ec])
            gb = _dot(h, w_in_ref[:, 2 * E + c0:2 * E + c0 + ec])
            z = _dot(h, w_in_ref[:, 3 * E + c0:3 * E + c0 + ec])
            v = gc * u
            vs_ref[j, SUBLANES:SUBLANES + sub, :] = v
            v1 = vs_ref[j, SUBLANES - 1:SUBLANES - 1 + sub, :]
            v2 = vs_ref[j, SUBLANES - 2:SUBLANES - 2 + sub, :]
            vs_ref[j, 0:SUBLANES, :] = v[sub - SUBLANES:sub, :]
            cw = cw_ref[:, c0:c0 + ec]
            conv = cw[0:1, :] * v2 + cw[1:2, :] * v1 + cw[2:3, :] * v + cb_ref[:, c0:c0 + ec]
            y = gb * conv * _silu(z)
            acc = acc + _dot(y.astype(BF16), w_out_ref[c0:c0 + ec, :])
        out = x + gate_ref[...] * acc
        if final:
            ms = jnp.mean(out * out, axis=-1, keepdims=True)
            out = out * lax.rsqrt(ms + EPS) * g_ref[...]
        o_ref[r0:r0 + sub, :] = out


def _conv_layer(x, mod, w_in, conv_w, conv_b, w_out, *, time_major_in, final_g=None,
                tm=1024, sub=1024, ec=512):
    if time_major_in:
        S, B, D = x.shape
        x = x.reshape(S, B * D)
        x_spec = pl.BlockSpec((tm, D), lambda b, i: (i, b))
    else:
        B, S, D = x.shape
        x_spec = pl.BlockSpec((None, tm, D), lambda b, i: (b, i, 0))
    E = w_out.shape[0]
    final = final_g is not None
    mod_specs = [pl.BlockSpec((None, 1, D), functools.partial(lambda b, i, k: (b, 0, k), k=k))
                 for k in range(3)]
    in_specs = [x_spec] + mod_specs + [_resident(w_in.shape), _resident(conv_w.shape),
                                       _resident((1, E)), _resident(w_out.shape)]
    args = [x, mod, mod, mod, w_in, conv_w, conv_b.reshape(1, E), w_out]
    if final:
        in_specs.append(_resident((1, D)))
        args.append(final_g.reshape(1, D))
    return pl.pallas_call(
        functools.partial(_conv_layer_kernel, ec=ec, sub=sub, final=final),
        grid=(B, S // tm),
        in_specs=in_specs,
        out_specs=pl.BlockSpec((None, tm, D), lambda b, i: (b, i, 0)),
        out_shape=jax.ShapeDtypeStruct((B, S, D), F32),
        scratch_shapes=[pltpu.VMEM((E // ec, SUBLANES + sub, ec), F32)],
        compiler_params=_cparams(("arbitrary", "arbitrary")),
        name="conv_layer_final" if final else "conv_layer",
    )(*args)


def _t5_bucket(dist):
    max_exact = REL_BUCKETS // 2
    d = np.maximum(dist, 0)
    ratio = np.log(np.maximum(d, 1) / max_exact) / math.log(REL_MAX_DIST / max_exact)
    large = np.minimum(max_exact + (ratio * (REL_BUCKETS - max_exact)).astype(np.int64), REL_BUCKETS - 1)
    return np.where(d < max_exact, d, large).astype(np.int32)


def _band_structure(window, dil):
    wsub = window // dil
    qi = np.arange(ATT_BLOCK)[:, None]
    ki = np.arange(2 * ATT_BLOCK)[None, :]
    dist = ATT_BLOCK + qi - ki
    in_band = (dist >= 0) & (dist <= wsub)
    first = in_band & (ki >= ATT_BLOCK)
    return _t5_bucket(dist * dil), np.stack([in_band, first]).astype(np.float32)


def _qkv_kernel(x_ref, shift_ref, scale_ref, w_ref, *refs, dils):
    out_refs, h_ref = refs[:-1], refs[-1]
    tn, D = x_ref.shape
    E = out_refs[0].shape[-1]
    h = _mod_norm(x_ref[...], scale_ref[...], shift_ref[...])
    n_col = D // LANES
    for c in range(n_col):
        h_ref[c] = h[:, c * LANES:(c + 1) * LANES]
    for g, dil in enumerate(dils):
        n = tn // dil
        if dil == 1:
            hp = h
        else:
            hp = jnp.concatenate(
                [jnp.concatenate([h_ref[c, pl.ds(r, n, stride=dil), :] for c in range(n_col)], axis=1)
                 for r in range(dil)], axis=0)
        hp = hp.astype(BF16)
        for j in range(3):
            c0 = (3 * g + j) * E
            res = _dot(hp, w_ref[:, c0:c0 + E])
            if j == 0:
                res = res * (LOG2E / math.sqrt(HEAD_DIM))
            out_refs[3 * g + j][...] = res.astype(BF16).reshape(dil, n, E)


def _qkv_proj(x, mod, w_qkv, dils, E, tn=512):
    B, S, D = x.shape
    mod_specs = [pl.BlockSpec((None, 1, D), functools.partial(lambda b, i, k: (b, 0, k), k=k))
                 for k in range(2)]
    out_specs, out_shape = [], []
    for dil in dils:
        out_specs += [pl.BlockSpec((None, dil, tn // dil, E), lambda b, i: (b, 0, i, 0))] * 3
        out_shape += [jax.ShapeDtypeStruct((B, dil, S // dil, E), BF16)] * 3
    return pl.pallas_call(
        functools.partial(_qkv_kernel, dils=dils),
        grid=(B, S // tn),
        in_specs=[pl.BlockSpec((None, tn, D), lambda b, i: (b, i, 0))] + mod_specs
                 + [_resident(w_qkv.shape)],
        out_specs=out_specs,
        out_shape=out_shape,
        scratch_shapes=[pltpu.VMEM((D // LANES, tn, LANES), F32)],
        compiler_params=_cparams(("arbitrary", "arbitrary")),
        name="qkv_proj",
    )(x, mod, mod, w_qkv)


def _attn_kernel(*refs, nr, nsub, has_prev, stride, n_heads):
    if has_prev:
        (q_ref, k_ref, v_ref, kp_ref, vp_ref, rb_ref, bucket_ref, mask_ref,
         o_ref, m_ref, den_ref, tbl_ref, s_ref, p_ref, stat_ref) = refs
    else:
        (q_ref, k_ref, v_ref, rb_ref, bucket_ref, mask_ref,
         o_ref, m_ref, den_ref, tbl_ref, s_ref, p_ref, stat_ref) = refs
    n_pairs = n_heads // 2
    pair_rows = 2 * ATT_BLOCK
    nk = s_ref.shape[1]
    step = pl.program_id(1)

    @pl.when((pl.program_id(0) == 0) & (step == 0))
    def _():
        def fill(b, carry):
            hit = bucket_ref[...] == b
            for h in range(n_heads):
                rows = slice(h * ATT_BLOCK, (h + 1) * ATT_BLOCK)
                s_ref[rows, :] = jnp.where(hit, rb_ref[b, h] * LOG2E, s_ref[rows, :])
            return carry
        s_ref[...] = jnp.zeros_like(s_ref)
        lax.fori_loop(0, REL_BUCKETS, fill, 0)
        for t in range(tbl_ref.shape[0]):
            for h in range(n_heads):
                rows = slice(h * ATT_BLOCK, (h + 1) * ATT_BLOCK)
                tbl_ref[t, rows, :] = jnp.where(mask_ref[t] > 0.0, s_ref[rows, :], NEG_INF)
        stat_ref[0] = jnp.zeros((ATT_BLOCK, LANES), F32)
        stat_ref[1] = jnp.ones((ATT_BLOCK, LANES), F32)

    lane = lax.broadcasted_iota(jnp.int32, (ATT_BLOCK, LANES), 1)
    lo = lane < HEAD_DIM

    def block(q_of, k_of, v_of, table, out_rows):
        for hp in range(n_pairs):
            cols = slice(hp * LANES, (hp + 1) * LANES)
            rows = slice(hp * pair_rows, (hp + 1) * pair_rows)
            q2 = q_of(cols)
            zero = jnp.zeros_like(q2)
            q_pair = jnp.concatenate([jnp.where(lo, q2, zero), jnp.where(lo, zero, q2)], axis=0)
            s_ref[rows, :] = lax.dot_general(q_pair, k_of(cols), (((1,), (1,)), ((), ())),
                                             preferred_element_type=F32)
        head_rows = [slice(h * ATT_BLOCK, (h + 1) * ATT_BLOCK) for h in range(n_heads)]
        if not has_prev:
            maxima = [jnp.max(s_ref[rows, :] + tbl_ref[table, rows, :], axis=-1, keepdims=True)
                      for rows in head_rows]
        for h, rows in enumerate(head_rows):
            s = s_ref[rows, :] + tbl_ref[table, rows, :]
            m = jnp.max(s, axis=-1, keepdims=True) if has_prev else maxima[h]
            p = jnp.exp2(s - m)
            den = jnp.sum(p, axis=-1, keepdims=True)
            p_ref[rows, :] = p.astype(BF16)
            stat_ref[0, :, h:h + 1] = m
            stat_ref[1, :, h:h + 1] = den
        m_ref[out_rows, :] = stat_ref[0]
        den_ref[out_rows, :] = stat_ref[1]
        for hp in range(n_pairs):
            cols = slice(hp * LANES, (hp + 1) * LANES)
            rows = slice(hp * pair_rows, (hp + 1) * pair_rows)
            res = _dot(p_ref[rows, :], v_of(cols))
            o_ref[hp, out_rows, :] = jnp.where(lo, res[:ATT_BLOCK], res[ATT_BLOCK:])

    for r in range(nr):
        for j in range(nsub):
            row = j * ATT_BLOCK
            q_of = lambda cols, r=r, row=row: q_ref[r, row:row + ATT_BLOCK, cols]
            if has_prev:
                out_rows = _rows(row * stride + r, ATT_BLOCK, stride)
                if j == 0:
                    k_of = lambda cols, r=r: jnp.concatenate([kp_ref[r, :, cols], k_ref[r, 0:ATT_BLOCK, cols]], axis=0)
                    v_of = lambda cols, r=r: jnp.concatenate([vp_ref[r, :, cols], v_ref[r, 0:ATT_BLOCK, cols]], axis=0)
                    table = jnp.where(step == 0, 1, 0)
                else:
                    k_of = lambda cols, r=r, row=row: k_ref[r, row - ATT_BLOCK:row + ATT_BLOCK, cols]
                    v_of = lambda cols, r=r, row=row: v_ref[r, row - ATT_BLOCK:row + ATT_BLOCK, cols]
                    table = 0
            else:
                out_rows = _rows(step * nr + r, ATT_BLOCK, stride)
                k_of = lambda cols, r=r: k_ref[r, :, cols]
                v_of = lambda cols, r=r: v_ref[r, :, cols]
                table = 0
            block(q_of, k_of, v_of, table, out_rows)


def _attention(q, k, v, rel_bias_g, window, dil, tokens_per_step=512):
    B, _, L, E = q.shape
    S = L * dil
    H = E // HEAD_DIM
    bucket, masks = _band_structure(window, dil)
    has_prev = L > ATT_BLOCK
    if has_prev:
        nr = dil
        nsub = tokens_per_step // (dil * ATT_BLOCK)
        tq = nsub * ATT_BLOCK
        n_steps = L // tq
        nk = 2 * ATT_BLOCK
        qkv_spec = pl.BlockSpec((None, nr, tq, E), lambda b, i: (b, 0, i, 0))
        prev_spec = pl.BlockSpec((None, nr, ATT_BLOCK, E),
                                 lambda b, i: (b, 0, jnp.maximum(i * nsub - 1, 0), 0))
        in_specs = [qkv_spec] * 3 + [prev_spec] * 2
        args = [q, k, v, k, v]
        out_block_rows = tokens_per_step
        out_step = lambda i: i
    else:
        nr = 4
        nsub = 1
        n_steps = dil // nr
        nk = ATT_BLOCK
        bucket, masks = bucket[:, ATT_BLOCK:], masks[1:, :, ATT_BLOCK:]
        qkv_spec = pl.BlockSpec((None, nr, ATT_BLOCK, E), lambda b, i: (b, i, 0, 0))
        in_specs = [qkv_spec] * 3
        args = [q, k, v]
        out_block_rows = S
        out_step = lambda i: 0
    in_specs += [pl.BlockSpec(memory_space=pltpu.SMEM), _resident(bucket.shape), _resident(masks.shape)]
    args += [rel_bias_g, jnp.asarray(bucket), jnp.asarray(masks)]
    stat_spec = pl.BlockSpec((None, out_block_rows, LANES), lambda b, i: (b, out_step(i), 0))
    stat_shape = jax.ShapeDtypeStruct((B, S, LANES), F32)
    return pl.pallas_call(
        functools.partial(_attn_kernel, nr=nr, nsub=nsub, has_prev=has_prev, stride=dil, n_heads=H),
        grid=(B, n_steps),
        in_specs=in_specs,
        out_specs=[pl.BlockSpec((None, E // LANES, out_block_rows, LANES), lambda b, i: (b, 0, out_step(i), 0)),
                   stat_spec, stat_spec],
        out_shape=[jax.ShapeDtypeStruct((B, E // LANES, S, LANES), F32), stat_shape, stat_shape],
        scratch_shapes=[pltpu.VMEM((masks.shape[0], H * ATT_BLOCK, nk), F32),
                        pltpu.VMEM((H * ATT_BLOCK, nk), F32),
                        pltpu.VMEM((H * ATT_BLOCK, nk), BF16),
                        pltpu.VMEM((2, ATT_BLOCK, LANES), F32)],
        compiler_params=_cparams(("arbitrary", "arbitrary")),
        name=f"attention_d{dil}",
    )(*args)


def _attn_out_kernel(x_ref, shift_ref, scale_ref, gate_ref, o0_ref, o1_ref, o2_ref,
                     m0_ref, m1_ref, m2_ref, d0_ref, d1_ref, d2_ref, wz_ref, w_out_ref, expand_ref,
                     out_ref):
    tm, D = x_ref.shape
    E = wz_ref.shape[1]
    n_heads = E // HEAD_DIM
    x = x_ref[...]
    h = _mod_norm(x, scale_ref[...], shift_ref[...]).astype(BF16)
    z = _dot(h, wz_ref[...])
    m0, m1, m2 = m0_ref[...], m1_ref[...], m2_ref[...]
    m = jnp.maximum(jnp.maximum(m0, m1), m2)
    e0, e1, e2 = jnp.exp2(m0 - m), jnp.exp2(m1 - m), jnp.exp2(m2 - m)
    den = e0 * d0_ref[...] + e1 * d1_ref[...] + e2 * d2_ref[...]
    head_lane = lax.broadcasted_iota(jnp.int32, (tm, LANES), 1) < n_heads
    packed = None
    for g, e in enumerate((e0, e1, e2)):
        w = jnp.where(head_lane, e / den, 0.0)
        hi = w.astype(BF16).astype(F32)
        for part, val in enumerate((hi, w - hi)):
            shift = (2 * g + part) * n_heads
            val = pltpu.roll(val, shift, axis=1) if shift else val
            packed = val if packed is None else packed + val
    wexp = _dot(packed.astype(BF16), expand_ref[...])
    o = None
    for g, o_ref in enumerate((o0_ref, o1_ref, o2_ref)):
        og = jnp.concatenate([o_ref[hp] for hp in range(E // LANES)], axis=-1)
        term = wexp[:, g * E:(g + 1) * E] * og
        o = term if o is None else o + term
    y = (o * _silu(z)).astype(BF16)
    out_ref[...] = x + gate_ref[...] * _dot(y, w_out_ref[...])


def _attn_out(x, mod, os_, ms, dens, w_z, w_out, tm=512):
    B, S, D = x.shape
    E = w_z.shape[1]
    tile = lambda w: pl.BlockSpec((None, tm, w), lambda b, i: (b, i, 0))
    mod_specs = [pl.BlockSpec((None, 1, D), functools.partial(lambda b, i, k: (b, 0, k), k=k))
                 for k in range(3)]
    n_groups, H = len(os_), E // HEAD_DIM
    expand = np.zeros((LANES, n_groups * E), np.float32)
    for g in range(n_groups):
        for part in range(2):
            for hd in range(H):
                expand[(2 * g + part) * H + hd, g * E + hd * HEAD_DIM:g * E + (hd + 1) * HEAD_DIM] = 1.0
    expand = jnp.asarray(expand, BF16)
    return pl.pallas_call(
        _attn_out_kernel,
        grid=(B, S // tm),
        in_specs=[tile(D)] + mod_specs
                 + [pl.BlockSpec((None, E // LANES, tm, LANES), lambda b, i: (b, 0, i, 0))] * 3
                 + [tile(LANES)] * 6
                 + [_resident(w_z.shape), _resident(w_out.shape), _resident(expand.shape)],
        out_specs=pl.BlockSpec((tm, D), lambda b, i: (i, b)),
        out_shape=jax.ShapeDtypeStruct((S, B * D), F32),
        compiler_params=_cparams(("arbitrary", "arbitrary")),
        name="attention_out",
    )(x, mod, mod, mod, *os_, *ms, *dens, w_z, w_out, expand).reshape(S, B, D)


def _s5_kernel(x_ref, shift_ref, scale_ref, gate_ref, w_in_ref, bbr_ref, bbi_ref, ar_ref, ai_ref,
               cr_ref, ci_ref, dskip_ref, w_glu_ref, b_glu_ref, w_out_ref,
               o_ref, state_ref, sr_ref, si_ref, *, scan_lanes):
    ts, B, D = x_ref.shape
    R = ts * B
    n_chunks, ck, cn = bbr_ref.shape
    E = n_chunks * ck

    @pl.when(pl.program_id(0) == 0)
    def _():
        state_ref[...] = jnp.zeros_like(state_ref)

    x3 = x_ref[...]
    ms = jnp.mean(x3 * x3, axis=-1, keepdims=True)
    h3 = (x3 * lax.rsqrt(ms + EPS)) * (1.0 + scale_ref[...][None]) + shift_ref[...][None]
    h = h3.reshape(R, D).astype(BF16)
    uz = _dot(h, w_in_ref[...])
    u, z = uz[:, :E], uz[:, E:]
    u16 = u.astype(BF16)

    def in_proj(c):
        uc = u16[:, c * ck:(c + 1) * ck]
        sr_ref[c] = _dot(uc, bbr_ref[c])
        si_ref[c] = _dot(uc, bbi_ref[c])

    def scan(c):
        for l0 in range(0, cn, scan_lanes):
            ls = slice(l0, l0 + scan_lanes)
            gl = slice(c * cn + l0, c * cn + l0 + scan_lanes)
            ar, ai = ar_ref[:, gl], ai_ref[:, gl]
            pr, pi = state_ref[0, :, gl], state_ref[1, :, gl]
            for t in range(ts):
                rows = slice(t * B, (t + 1) * B)
                nr = ar * pr - ai * pi + sr_ref[c, rows, ls]
                ni = ar * pi + ai * pr + si_ref[c, rows, ls]
                sr_ref[c, rows, ls] = nr
                si_ref[c, rows, ls] = ni
                pr, pi = nr, ni
            state_ref[0, :, gl] = pr
            state_ref[1, :, gl] = pi

    def out_proj(c):
        return _dot(sr_ref[c].astype(BF16), cr_ref[c]) - _dot(si_ref[c].astype(BF16), ci_ref[c])

    y_parts = []
    in_proj(0)
    for c in range(n_chunks):
        if c + 1 < n_chunks:
            in_proj(c + 1)
        scan(c)
        if c >= 1:
            y_parts.append(out_proj(c - 1))
    y_parts.append(out_proj(n_chunks - 1))
    y = jnp.concatenate(y_parts, axis=-1) + dskip_ref[...] * u
    g = y * (0.5 * (1.0 + jnp.tanh(math.sqrt(2.0 / math.pi) * (y + 0.044715 * (y * y * y)))))
    y = g * _sigmoid(_dot(g.astype(BF16), w_glu_ref[...]) + b_glu_ref[...])
    y = (y * _silu(z)).astype(BF16)
    out = _dot(y, w_out_ref[...]).reshape(ts, B, D)
    o_ref[...] = x3 + gate_ref[...][None] * out


def _s5_layer(xt, mod, w_in, bbr, bbi, ar, ai, cr, ci, d_skip, w_glu, b_glu, w_out, ts=64, scan_lanes=512):
    S, B, D = xt.shape
    E = w_glu.shape[0]
    N = ar.shape[1]
    mod_specs = [pl.BlockSpec((B, D), functools.partial(lambda i, k: (0, k), k=k)) for k in range(3)]
    weights = [w_in, bbr, bbi, ar, ai, cr, ci, d_skip.reshape(1, E), w_glu, b_glu.reshape(1, E), w_out]
    return pl.pallas_call(
        functools.partial(_s5_kernel, scan_lanes=scan_lanes),
        grid=(S // ts,),
        in_specs=[pl.BlockSpec((ts, B, D), lambda i: (i, 0, 0))] + mod_specs
                 + [_resident(w.shape) for w in weights],
        out_specs=pl.BlockSpec((ts, B, D), lambda i: (i, 0, 0)),
        out_shape=jax.ShapeDtypeStruct((S, B, D), F32),
        scratch_shapes=[pltpu.VMEM((2, B, N), F32),
                        pltpu.VMEM((bbr.shape[0], ts * B, bbr.shape[2]), F32),
                        pltpu.VMEM((bbr.shape[0], ts * B, bbr.shape[2]), F32)],
        compiler_params=_cparams(("arbitrary",)),
        name="s5_layer",
    )(xt, mod, mod, mod, *weights)


def _s5_params(log_dt, lambda_re, lambda_im, b_re, b_im, c_re, c_im, B, groups_per_chunk=16):
    G, P = lambda_re.shape
    K = b_re.shape[2]
    dt = jnp.exp(log_dt.astype(F32))[:, None]
    lr = lambda_re.astype(F32)
    li = lambda_im.astype(F32)
    mag = jnp.exp(lr * dt)
    ar = mag * jnp.cos(li * dt)
    ai = mag * jnp.sin(li * dt)
    inv = 1.0 / (lr * lr + li * li)
    qr = ((ar - 1.0) * lr + ai * li) * inv
    qi = (ai * lr - (ar - 1.0) * li) * inv
    br = b_re.astype(F32)
    bi = b_im.astype(F32)
    bbr = qr[..., None] * br - qi[..., None] * bi
    bbi = qr[..., None] * bi + qi[..., None] * br
    gc = groups_per_chunk
    nc = G // gc

    def block_diag(m):
        a, b = m.shape[1:]
        rows = m.astype(BF16).reshape(nc, gc * a, b)
        on_diag = np.kron(np.eye(gc, dtype=bool), np.ones((a, b), dtype=bool))
        return jnp.where(on_diag, jnp.tile(rows, (1, 1, gc)), jnp.zeros((), BF16))

    in_blocks = lambda m: block_diag(jnp.swapaxes(m, 1, 2))
    out_blocks = lambda m: block_diag(jnp.swapaxes(m.astype(F32), 1, 2))

    bcast = lambda a: jnp.broadcast_to(a.reshape(1, G * P), (B, G * P))
    return in_blocks(bbr), in_blocks(bbi), bcast(ar), bcast(ai), out_blocks(c_re), out_blocks(c_im)


def kernel(x, c, rel_bias, final_g, l0_ada_w, l0_ada_b, l0_w_in, l0_conv_w, l0_conv_b, l0_w_out, l1_ada_w, l1_ada_b, l1_w_in, l1_w_out, l2_ada_w, l2_ada_b, l2_w_in, l2_log_dt, l2_lambda_re, l2_lambda_im, l2_b_re, l2_b_im, l2_c_re, l2_c_im, l2_d_skip, l2_w_glu, l2_b_glu, l2_w_out, l3_ada_w, l3_ada_b, l3_w_in, l3_conv_w, l3_conv_b, l3_w_out):
    B, S, D = x.shape
    assert B == SUBLANES, "the S5 scan keeps the batch on the sublane axis"
    mods = _modulation(c, (l0_ada_w, l1_ada_w, l2_ada_w, l3_ada_w), (l0_ada_b, l1_ada_b, l2_ada_b, l3_ada_b))
    mod3 = [m.reshape(B, 1, 3 * D) for m in mods]

    x = _conv_layer(x, mod3[0], l0_w_in.astype(BF16), l0_conv_w, l0_conv_b, l0_w_out.astype(BF16),
                    time_major_in=False)

    n_groups = len(DILATION_PATTERNS)
    dils = tuple(dil for _, dil in DILATION_PATTERNS)
    E = l1_w_out.shape[0]
    H = E // HEAD_DIM
    w1 = l1_w_in.astype(BF16)
    qkv = _qkv_proj(x, mod3[1], w1[:, :3 * E * n_groups], dils, E)
    os_, ms, dens = [], [], []
    for g, (window, dil) in enumerate(DILATION_PATTERNS):
        o, m, den = _attention(*qkv[3 * g:3 * g + 3], rel_bias[:, g * H:(g + 1) * H], window, dil)
        os_.append(o)
        ms.append(m)
        dens.append(den)
    xt = _attn_out(x, mod3[1], os_, ms, dens, w1[:, 3 * E * n_groups:], l1_w_out.astype(BF16))

    ssm = _s5_params(l2_log_dt, l2_lambda_re, l2_lambda_im, l2_b_re, l2_b_im, l2_c_re, l2_c_im, B)
    xt = _s5_layer(xt, mods[2], l2_w_in.astype(BF16), *ssm, l2_d_skip, l2_w_glu.astype(BF16), l2_b_glu,
                   l2_w_out.astype(BF16))

    return _conv_layer(xt, mod3[3], l3_w_in.astype(BF16), l3_conv_w, l3_conv_b, l3_w_out.astype(BF16),
                       time_major_in=True, final_g=final_g)
```

```python
import functools
import math

import numpy as np
import jax
import jax.numpy as jnp
from jax import lax
from jax.experimental import pallas as pl
from jax.experimental.pallas import tpu as pltpu

EPS = 1e-6
HEAD_DIM = 64
ATT_BLOCK = 128
DILATION_PATTERNS = ((128, 1), (512, 4), (2048, 16))
REL_BUCKETS = 32
REL_MAX_DIST = 2048
NEG_INF = -1e30
LOG2E = math.log2(math.e)
LANES = 128
SUBLANES = 8
VMEM_LIMIT_BYTES = 60 * 1024 * 1024

F32 = jnp.float32
BF16 = jnp.bfloat16


def _cparams(semantics):
    return pltpu.CompilerParams(dimension_semantics=semantics, vmem_limit_bytes=VMEM_LIMIT_BYTES)


def _resident(shape):
    nd = len(shape)
    return pl.BlockSpec(shape, lambda *_: (0,) * nd, pipeline_mode=pl.Buffered(1))


def _silu(z):
    return z * (1.0 / (1.0 + jnp.exp(-z)))


def _sigmoid(z):
    return 1.0 / (1.0 + jnp.exp(-z))


def _mod_norm(x, scale, shift):
    ms = jnp.mean(x * x, axis=-1, keepdims=True)
    return (x * lax.rsqrt(ms + EPS)) * (1.0 + scale) + shift


def _dot(a, b):
    return jnp.dot(a, b, preferred_element_type=F32)


def _rows(start, size, stride):
    return pl.ds(start, size) if stride == 1 else pl.ds(start, size, stride=stride)


def _modulation_kernel(c_ref, w0, w1, w2, w3, b0, b1, b2, b3, o0, o1, o2, o3):
    sc = _silu(c_ref[...]).astype(BF16)
    for w, b, o in ((w0, b0, o0), (w1, b1, o1), (w2, b2, o2), (w3, b3, o3)):
        o[...] = _dot(sc, w[...].astype(BF16)) + b[...]


def _modulation(c, ws, bs, tn=512):
    B, D = c.shape
    N = ws[0].shape[1]
    w_spec = pl.BlockSpec((D, tn), lambda j: (0, j))
    b_spec = pl.BlockSpec((1, tn), lambda j: (0, j))
    o_spec = pl.BlockSpec((B, tn), lambda j: (0, j))
    return pl.pallas_call(
        _modulation_kernel,
        grid=(N // tn,),
        in_specs=[pl.BlockSpec((B, D), lambda j: (0, 0))] + [w_spec] * 4 + [b_spec] * 4,
        out_specs=[o_spec] * 4,
        out_shape=[jax.ShapeDtypeStruct((B, N), F32)] * 4,
        compiler_params=_cparams(("arbitrary",)),
        name="modulation",
    )(c, *ws, *[b.reshape(1, N) for b in bs])


def _cast_specs(arrays, n_steps, step_of):
    in_specs, out_specs, out_shape = [], [], []
    for a in arrays:
        rows, cols = a.shape
        spec = pl.BlockSpec((rows // n_steps, cols), lambda *ids: (step_of(*ids), 0))
        in_specs.append(spec)
        out_specs.append(spec)
        out_shape.append(jax.ShapeDtypeStruct(a.shape, BF16))
    return in_specs, out_specs, out_shape


def _cast_slabs(src_refs, dst_refs):
    for src, dst in zip(src_refs, dst_refs):
        dst[...] = src[...].astype(BF16)


def _conv_layer_kernel(*refs, ec, final, n_cast):
    n_in = 9 if final else 8
    (x_ref, shift_ref, scale_ref, gate_ref, w_in_ref, cw_ref, cb_ref, w_out_ref) = refs[:8]
    g_ref = refs[8] if final else None
    cast_src = refs[n_in:n_in + n_cast]
    o_ref = refs[n_in + n_cast]
    cast_dst = refs[n_in + n_cast + 1:n_in + 2 * n_cast + 1]
    carry_ref, vs_ref = refs[n_in + 2 * n_cast + 1:]
    tm, D = x_ref.shape
    E = w_out_ref.shape[0]

    @pl.when(pl.program_id(1) == 0)
    def _():
        carry_ref[...] = jnp.zeros_like(carry_ref)

    _cast_slabs(cast_src, cast_dst)
    x = x_ref[...]
    h = _mod_norm(x, scale_ref[...], shift_ref[...]).astype(BF16)
    acc = jnp.zeros((tm, D), F32)
    for j in range(E // ec):
        c0 = j * ec
        u = _dot(h, w_in_ref[:, c0:c0 + ec])
        gc = _dot(h, w_in_ref[:, E + c0:E + c0 + ec])
        gb = _dot(h, w_in_ref[:, 2 * E + c0:2 * E + c0 + ec])
        z = _dot(h, w_in_ref[:, 3 * E + c0:3 * E + c0 + ec])
        v = gc * u
        vs_ref[0:SUBLANES, :] = carry_ref[j]
        vs_ref[SUBLANES:SUBLANES + tm, :] = v
        carry_ref[j] = v[tm - SUBLANES:tm, :]
        v1 = vs_ref[SUBLANES - 1:SUBLANES - 1 + tm, :]
        v2 = vs_ref[SUBLANES - 2:SUBLANES - 2 + tm, :]
        cw = cw_ref[:, c0:c0 + ec]
        conv = cw[0:1, :] * v2 + cw[1:2, :] * v1 + cw[2:3, :] * v + cb_ref[:, c0:c0 + ec]
        y = gb * conv * _silu(z)
        acc = acc + _dot(y.astype(BF16), w_out_ref[c0:c0 + ec, :])
    out = x + gate_ref[...] * acc
    if final:
        ms = jnp.mean(out * out, axis=-1, keepdims=True)
        out = out * lax.rsqrt(ms + EPS) * g_ref[...]
    o_ref[...] = out


def _conv_layer(x, mod, w_in, conv_w, conv_b, w_out, *, time_major_in, final_g=None, cast=(),
                tm=1024, ec=512):
    if time_major_in:
        S, B, D = x.shape
        x = x.reshape(S, B * D)
        x_spec = pl.BlockSpec((tm, D), lambda b, i: (i, b))
    else:
        B, S, D = x.shape
        x_spec = pl.BlockSpec((None, tm, D), lambda b, i: (b, i, 0))
    E = w_out.shape[0]
    final = final_g is not None
    mod_specs = [pl.BlockSpec((None, 1, D), functools.partial(lambda b, i, k: (b, 0, k), k=k))
                 for k in range(3)]
    in_specs = [x_spec] + mod_specs + [_resident(w_in.shape), _resident(conv_w.shape),
                                       _resident((1, E)), _resident(w_out.shape)]
    args = [x, mod, mod, mod, w_in, conv_w, conv_b.reshape(1, E), w_out]
    if final:
        in_specs.append(_resident((1, D)))
        args.append(final_g.reshape(1, D))
    n_tiles = S // tm
    cast_in, cast_out, cast_shape = _cast_specs(cast, B * n_tiles, lambda b, i: b * n_tiles + i)
    res = pl.pallas_call(
        functools.partial(_conv_layer_kernel, ec=ec, final=final, n_cast=len(cast)),
        grid=(B, n_tiles),
        in_specs=in_specs + cast_in,
        out_specs=[pl.BlockSpec((None, tm, D), lambda b, i: (b, i, 0))] + cast_out,
        out_shape=[jax.ShapeDtypeStruct((B, S, D), F32)] + cast_shape,
        scratch_shapes=[pltpu.VMEM((E // ec, SUBLANES, ec), F32),
                        pltpu.VMEM((SUBLANES + tm, ec), F32)],
        compiler_params=_cparams(("arbitrary", "arbitrary")),
        name="conv_layer_final" if final else "conv_layer",
    )(*args, *cast)
    return res if cast else res[0]


def _t5_bucket(dist):
    max_exact = REL_BUCKETS // 2
    d = np.maximum(dist, 0)
    ratio = np.log(np.maximum(d, 1) / max_exact) / math.log(REL_MAX_DIST / max_exact)
    large = np.minimum(max_exact + (ratio * (REL_BUCKETS - max_exact)).astype(np.int64), REL_BUCKETS - 1)
    return np.where(d < max_exact, d, large).astype(np.int32)


def _band_structure(window, dil):
    wsub = window // dil
    qi = np.arange(ATT_BLOCK)[:, None]
    ki = np.arange(2 * ATT_BLOCK)[None, :]
    dist = ATT_BLOCK + qi - ki
    in_band = (dist >= 0) & (dist <= wsub)
    first = in_band & (ki >= ATT_BLOCK)
    return _t5_bucket(dist * dil), np.stack([in_band, first]).astype(np.float32)


def _qkv_kernel(x_ref, shift_ref, scale_ref, w_ref, *refs, dils, n_cast):
    n_out = 3 * len(dils)
    cast_src, out_refs = refs[:n_cast], refs[n_cast:n_cast + n_out]
    cast_dst, h_ref = refs[n_cast + n_out:-1], refs[-1]
    tn, D = x_ref.shape
    E = out_refs[0].shape[-1]
    _cast_slabs(cast_src, cast_dst)
    h = _mod_norm(x_ref[...], scale_ref[...], shift_ref[...])
    n_col = D // LANES
    for c in range(n_col):
        h_ref[c] = h[:, c * LANES:(c + 1) * LANES]
    for g, dil in enumerate(dils):
        n = tn // dil
        if dil == 1:
            hp = h
        else:
            hp = jnp.concatenate(
                [jnp.concatenate([h_ref[c, pl.ds(r, n, stride=dil), :] for c in range(n_col)], axis=1)
                 for r in range(dil)], axis=0)
        hp = hp.astype(BF16)
        for j in range(3):
            c0 = (3 * g + j) * E
            res = _dot(hp, w_ref[:, c0:c0 + E])
            if j == 0:
                res = res * (LOG2E / math.sqrt(HEAD_DIM))
            out_refs[3 * g + j][...] = res.astype(BF16).reshape(dil, n, E)


def _qkv_proj(x, mod, w_qkv, dils, E, cast=(), tn=512):
    B, S, D = x.shape
    n_tiles = S // tn
    cast_in, cast_out, cast_shape = _cast_specs(cast, B * n_tiles, lambda b, i: b * n_tiles + i)
    mod_specs = [pl.BlockSpec((None, 1, D), functools.partial(lambda b, i, k: (b, 0, k), k=k))
                 for k in range(2)]
    out_specs, out_shape = [], []
    for dil in dils:
        out_specs += [pl.BlockSpec((None, dil, tn // dil, E), lambda b, i: (b, 0, i, 0))] * 3
        out_shape += [jax.ShapeDtypeStruct((B, dil, S // dil, E), BF16)] * 3
    return pl.pallas_call(
        functools.partial(_qkv_kernel, dils=dils, n_cast=len(cast)),
        grid=(B, n_tiles),
        in_specs=[pl.BlockSpec((None, tn, D), lambda b, i: (b, i, 0))] + mod_specs
                 + [_resident(w_qkv.shape)] + cast_in,
        out_specs=out_specs + cast_out,
        out_shape=out_shape + cast_shape,
        scratch_shapes=[pltpu.VMEM((D // LANES, tn, LANES), F32)],
        compiler_params=_cparams(("arbitrary", "arbitrary")),
        name="qkv_proj",
    )(x, mod, mod, w_qkv, *cast)


def _attn_kernel(*refs, nr, nsub, has_prev, stride, n_heads):
    if has_prev:
        (q_ref, k_ref, v_ref, kp_ref, vp_ref, rb_ref, bucket_ref, mask_ref,
         o_ref, m_ref, den_ref, tbl_ref, s_ref, p_ref, stat_ref) = refs
    else:
        (q_ref, k_ref, v_ref, rb_ref, bucket_ref, mask_ref,
         o_ref, m_ref, den_ref, tbl_ref, s_ref, p_ref, stat_ref) = refs
    n_pairs = n_heads // 2
    pair_rows = 2 * ATT_BLOCK
    nk = s_ref.shape[1]
    step = pl.program_id(1)

    @pl.when((pl.program_id(0) == 0) & (step == 0))
    def _():
        def fill(b, carry):
            hit = bucket_ref[...] == b
            for h in range(n_heads):
                rows = slice(h * ATT_BLOCK, (h + 1) * ATT_BLOCK)
                s_ref[rows, :] = jnp.where(hit, rb_ref[b, h] * LOG2E, s_ref[rows, :])
            return carry
        s_ref[...] = jnp.zeros_like(s_ref)
        lax.fori_loop(0, REL_BUCKETS, fill, 0)
        for t in range(tbl_ref.shape[0]):
            for h in range(n_heads):
                rows = slice(h * ATT_BLOCK, (h + 1) * ATT_BLOCK)
                tbl_ref[t, rows, :] = jnp.where(mask_ref[t] > 0.0, s_ref[rows, :], NEG_INF)
        stat_ref[0] = jnp.zeros((ATT_BLOCK, LANES), F32)
        stat_ref[1] = jnp.ones((ATT_BLOCK, LANES), F32)

    lane = lax.broadcasted_iota(jnp.int32, (ATT_BLOCK, LANES), 1)
    lo = lane < HEAD_DIM

    def block(q_of, k_of, v_of, table, out_rows):
        for hp in range(n_pairs):
            cols = slice(hp * LANES, (hp + 1) * LANES)
            rows = slice(hp * pair_rows, (hp + 1) * pair_rows)
            q2 = q_of(cols)
            zero = jnp.zeros_like(q2)
            q_pair = jnp.concatenate([jnp.where(lo, q2, zero), jnp.where(lo, zero, q2)], axis=0)
            s_ref[rows, :] = lax.dot_general(q_pair, k_of(cols), (((1,), (1,)), ((), ())),
                                             preferred_element_type=F32)
        head_rows = [slice(h * ATT_BLOCK, (h + 1) * ATT_BLOCK) for h in range(n_heads)]
        if not has_prev:
            maxima = [jnp.max(s_ref[rows, :] + tbl_ref[table, rows, :], axis=-1, keepdims=True)
                      for rows in head_rows]
        for h, rows in enumerate(head_rows):
            s = s_ref[rows, :] + tbl_ref[table, rows, :]
            m = jnp.max(s, axis=-1, keepdims=True) if has_prev else maxima[h]
            p = jnp.exp2(s - m)
            den = jnp.sum(p, axis=-1, keepdims=True)
            p_ref[rows, :] = p.astype(BF16)
            stat_ref[0, :, h:h + 1] = m
            stat_ref[1, :, h:h + 1] = den
        m_ref[out_rows, :] = stat_ref[0]
        den_ref[out_rows, :] = stat_ref[1]
        for hp in range(n_pairs):
            cols = slice(hp * LANES, (hp + 1) * LANES)
            rows = slice(hp * pair_rows, (hp + 1) * pair_rows)
            res = _dot(p_ref[rows, :], v_of(cols))
            o_ref[hp, out_rows, :] = jnp.where(lo, res[:ATT_BLOCK], res[ATT_BLOCK:])

    for r in range(nr):
        for j in range(nsub):
            row = j * ATT_BLOCK
            q_of = lambda cols, r=r, row=row: q_ref[r, row:row + ATT_BLOCK, cols]
            if has_prev:
                out_rows = _rows(row * stride + r, ATT_BLOCK, stride)
                if j == 0:
                    k_of = lambda cols, r=r: jnp.concatenate([kp_ref[r, :, cols], k_ref[r, 0:ATT_BLOCK, cols]], axis=0)
                    v_of = lambda cols, r=r: jnp.concatenate([vp_ref[r, :, cols], v_ref[r, 0:ATT_BLOCK, cols]], axis=0)
                    table = jnp.where(step == 0, 1, 0)
                else:
                    k_of = lambda cols, r=r, row=row: k_ref[r, row - ATT_BLOCK:row + ATT_BLOCK, cols]
                    v_of = lambda cols, r=r, row=row: v_ref[r, row - ATT_BLOCK:row + ATT_BLOCK, cols]
                    table = 0
            else:
                out_rows = _rows(step * nr + r, ATT_BLOCK, stride)
                k_of = lambda cols, r=r: k_ref[r, :, cols]
                v_of = lambda cols, r=r: v_ref[r, :, cols]
                table = 0
            block(q_of, k_of, v_of, table, out_rows)


def _attention(q, k, v, rel_bias_g, window, dil, tokens_per_step=512):
    B, _, L, E = q.shape
    S = L * dil
    H = E // HEAD_DIM
    bucket, masks = _band_structure(window, dil)
    has_prev = L > ATT_BLOCK
    if has_prev:
        nr = dil
        nsub = tokens_per_step // (dil * ATT_BLOCK)
        tq = nsub * ATT_BLOCK
        n_steps = L // tq
        nk = 2 * ATT_BLOCK
        qkv_spec = pl.BlockSpec((None, nr, tq, E), lambda b, i: (b, 0, i, 0))
        prev_spec = pl.BlockSpec((None, nr, ATT_BLOCK, E),
                                 lambda b, i: (b, 0, jnp.maximum(i * nsub - 1, 0), 0))
        in_specs = [qkv_spec] * 3 + [prev_spec] * 2
        args = [q, k, v, k, v]
        out_block_rows = tokens_per_step
        out_step = lambda i: i
    else:
        nr = 4
        nsub = 1
        n_steps = dil // nr
        nk = ATT_BLOCK
        bucket, masks = bucket[:, ATT_BLOCK:], masks[1:, :, ATT_BLOCK:]
        qkv_spec = pl.BlockSpec((None, nr, ATT_BLOCK, E), lambda b, i: (b, i, 0, 0))
        in_specs = [qkv_spec] * 3
        args = [q, k, v]
        out_block_rows = S
        out_step = lambda i: 0
    in_specs += [pl.BlockSpec(memory_space=pltpu.SMEM), _resident(bucket.shape), _resident(masks.shape)]
    args += [rel_bias_g, jnp.asarray(bucket), jnp.asarray(masks)]
    stat_spec = pl.BlockSpec((None, out_block_rows, LANES), lambda b, i: (b, out_step(i), 0))
    stat_shape = jax.ShapeDtypeStruct((B, S, LANES), F32)
    return pl.pallas_call(
        functools.partial(_attn_kernel, nr=nr, nsub=nsub, has_prev=has_prev, stride=dil, n_heads=H),
        grid=(B, n_steps),
        in_specs=in_specs,
        out_specs=[pl.BlockSpec((None, E // LANES, out_block_rows, LANES), lambda b, i: (b, 0, out_step(i), 0)),
                   stat_spec, stat_spec],
        out_shape=[jax.ShapeDtypeStruct((B, E // LANES, S, LANES), F32), stat_shape, stat_shape],
        scratch_shapes=[pltpu.VMEM((masks.shape[0], H * ATT_BLOCK, nk), F32),
                        pltpu.VMEM((H * ATT_BLOCK, nk), F32),
                        pltpu.VMEM((H * ATT_BLOCK, nk), BF16),
                        pltpu.VMEM((2, ATT_BLOCK, LANES), F32)],
        compiler_params=_cparams(("arbitrary", "arbitrary")),
        name=f"attention_d{dil}",
    )(*args)


def _attn_out_kernel(x_ref, shift_ref, scale_ref, gate_ref, o0_ref, o1_ref, o2_ref,
                     m0_ref, m1_ref, m2_ref, d0_ref, d1_ref, d2_ref, wz_ref, w_out_ref, expand_ref,
                     out_ref):
    tm, D = x_ref.shape
    E = wz_ref.shape[1]
    n_heads = E // HEAD_DIM
    x = x_ref[...]
    h = _mod_norm(x, scale_ref[...], shift_ref[...]).astype(BF16)
    z = _dot(h, wz_ref[...])
    m0, m1, m2 = m0_ref[...], m1_ref[...], m2_ref[...]
    m = jnp.maximum(jnp.maximum(m0, m1), m2)
    e0, e1, e2 = jnp.exp2(m0 - m), jnp.exp2(m1 - m), jnp.exp2(m2 - m)
    den = e0 * d0_ref[...] + e1 * d1_ref[...] + e2 * d2_ref[...]
    head_lane = lax.broadcasted_iota(jnp.int32, (tm, LANES), 1) < n_heads
    packed = None
    for g, e in enumerate((e0, e1, e2)):
        w = jnp.where(head_lane, e / den, 0.0)
        hi = w.astype(BF16).astype(F32)
        for part, val in enumerate((hi, w - hi)):
            shift = (2 * g + part) * n_heads
            val = pltpu.roll(val, shift, axis=1) if shift else val
            packed = val if packed is None else packed + val
    wexp = _dot(packed.astype(BF16), expand_ref[...])
    o = None
    for g, o_ref in enumerate((o0_ref, o1_ref, o2_ref)):
        og = jnp.concatenate([o_ref[hp] for hp in range(E // LANES)], axis=-1)
        term = wexp[:, g * E:(g + 1) * E] * og
        o = term if o is None else o + term
    y = (o * _silu(z)).astype(BF16)
    out_ref[...] = x + gate_ref[...] * _dot(y, w_out_ref[...])


def _attn_out(x, mod, os_, ms, dens, w_in, w_out, tm=512):
    B, S, D = x.shape
    E = w_out.shape[0]
    z_block = w_in.shape[1] // E - 1
    tile = lambda w: pl.BlockSpec((None, tm, w), lambda b, i: (b, i, 0))
    mod_specs = [pl.BlockSpec((None, 1, D), functools.partial(lambda b, i, k: (b, 0, k), k=k))
                 for k in range(3)]
    n_groups, H = len(os_), E // HEAD_DIM
    expand = np.zeros((LANES, n_groups * E), np.float32)
    for g in range(n_groups):
        for part in range(2):
            for hd in range(H):
                expand[(2 * g + part) * H + hd, g * E + hd * HEAD_DIM:g * E + (hd + 1) * HEAD_DIM] = 1.0
    expand = jnp.asarray(expand, BF16)
    return pl.pallas_call(
        _attn_out_kernel,
        grid=(B, S // tm),
        in_specs=[tile(D)] + mod_specs
                 + [pl.BlockSpec((None, E // LANES, tm, LANES), lambda b, i: (b, 0, i, 0))] * 3
                 + [tile(LANES)] * 6
                 + [pl.BlockSpec((D, E), lambda b, i: (0, z_block), pipeline_mode=pl.Buffered(1)),
                    _resident(w_out.shape), _resident(expand.shape)],
        out_specs=pl.BlockSpec((tm, D), lambda b, i: (i, b)),
        out_shape=jax.ShapeDtypeStruct((S, B * D), F32),
        compiler_params=_cparams(("arbitrary", "arbitrary")),
        name="attention_out",
    )(x, mod, mod, mod, *os_, *ms, *dens, w_in, w_out, expand).reshape(S, B, D)


def _s5_kernel(x_ref, shift_ref, scale_ref, gate_ref, w_in_ref, bbr_ref, bbi_ref, ar_ref, ai_ref,
               cr_ref, ci_ref, dskip_ref, w_glu_ref, b_glu_ref, w_out_ref,
               o_ref, state_ref, sr_ref, si_ref, *, scan_lanes):
    ts, B, D = x_ref.shape
    R = ts * B
    n_chunks, ck, cn = bbr_ref.shape
    E = n_chunks * ck

    @pl.when(pl.program_id(0) == 0)
    def _():
        state_ref[...] = jnp.zeros_like(state_ref)

    x3 = x_ref[...]
    ms = jnp.mean(x3 * x3, axis=-1, keepdims=True)
    h3 = (x3 * lax.rsqrt(ms + EPS)) * (1.0 + scale_ref[...][None]) + shift_ref[...][None]
    h = h3.reshape(R, D).astype(BF16)
    uz = _dot(h, w_in_ref[...])
    u, z = uz[:, :E], uz[:, E:]
    u16 = u.astype(BF16)

    def in_proj(c):
        uc = u16[:, c * ck:(c + 1) * ck]
        sr_ref[c] = _dot(uc, bbr_ref[c])
        si_ref[c] = _dot(uc, bbi_ref[c])

    def scan(c):
        for l0 in range(0, cn, scan_lanes):
            ls = slice(l0, l0 + scan_lanes)
            gl = slice(c * cn + l0, c * cn + l0 + scan_lanes)
            ar, ai = ar_ref[:, gl], ai_ref[:, gl]
            pr, pi = state_ref[0, :, gl], state_ref[1, :, gl]
            for t in range(ts):
                rows = slice(t * B, (t + 1) * B)
                nr = ar * pr - ai * pi + sr_ref[c, rows, ls]
                ni = ar * pi + ai * pr + si_ref[c, rows, ls]
                sr_ref[c, rows, ls] = nr
                si_ref[c, rows, ls] = ni
                pr, pi = nr, ni
            state_ref[0, :, gl] = pr
            state_ref[1, :, gl] = pi

    def out_proj(c):
        return _dot(sr_ref[c].astype(BF16), cr_ref[c]) - _dot(si_ref[c].astype(BF16), ci_ref[c])

    y_parts = []
    in_proj(0)
    for c in range(n_chunks):
        if c + 1 < n_chunks:
            in_proj(c + 1)
        scan(c)
        if c >= 1:
            y_parts.append(out_proj(c - 1))
    y_parts.append(out_proj(n_chunks - 1))
    y = jnp.concatenate(y_parts, axis=-1) + dskip_ref[...] * u
    g = y * (0.5 * (1.0 + jnp.tanh(math.sqrt(2.0 / math.pi) * (y + 0.044715 * (y * y * y)))))
    y = g * _sigmoid(_dot(g.astype(BF16), w_glu_ref[...]) + b_glu_ref[...])
    y = (y * _silu(z)).astype(BF16)
    out = _dot(y, w_out_ref[...]).reshape(ts, B, D)
    o_ref[...] = x3 + gate_ref[...][None] * out


def _s5_layer(xt, mod, w_in, bbr, bbi, ar, ai, cr, ci, d_skip, w_glu, b_glu, w_out, ts=64, scan_lanes=512):
    S, B, D = xt.shape
    E = w_glu.shape[0]
    N = ar.shape[1]
    mod_specs = [pl.BlockSpec((B, D), functools.partial(lambda i, k: (0, k), k=k)) for k in range(3)]
    weights = [w_in, bbr, bbi, ar, ai, cr, ci, d_skip.reshape(1, E), w_glu, b_glu.reshape(1, E), w_out]
    return pl.pallas_call(
        functools.partial(_s5_kernel, scan_lanes=scan_lanes),
        grid=(S // ts,),
        in_specs=[pl.BlockSpec((ts, B, D), lambda i: (i, 0, 0))] + mod_specs
                 + [_resident(w.shape) for w in weights],
        out_specs=pl.BlockSpec((ts, B, D), lambda i: (i, 0, 0)),
        out_shape=jax.ShapeDtypeStruct((S, B, D), F32),
        scratch_shapes=[pltpu.VMEM((2, B, N), F32),
                        pltpu.VMEM((bbr.shape[0], ts * B, bbr.shape[2]), F32),
                        pltpu.VMEM((bbr.shape[0], ts * B, bbr.shape[2]), F32)],
        compiler_params=_cparams(("arbitrary",)),
        name="s5_layer",
    )(xt, mod, mod, mod, *weights)


def _s5_params(log_dt, lambda_re, lambda_im, b_re, b_im, c_re, c_im, B, groups_per_chunk=16):
    G, P = lambda_re.shape
    K = b_re.shape[2]
    dt = jnp.exp(log_dt.astype(F32))[:, None]
    lr = lambda_re.astype(F32)
    li = lambda_im.astype(F32)
    mag = jnp.exp(lr * dt)
    ar = mag * jnp.cos(li * dt)
    ai = mag * jnp.sin(li * dt)
    inv = 1.0 / (lr * lr + li * li)
    qr = ((ar - 1.0) * lr + ai * li) * inv
    qi = (ai * lr - (ar - 1.0) * li) * inv
    br = b_re.astype(F32)
    bi = b_im.astype(F32)
    bbr = qr[..., None] * br - qi[..., None] * bi
    bbi = qr[..., None] * bi + qi[..., None] * br
    gc = groups_per_chunk
    nc = G // gc

    def block_diag(m):
        a, b = m.shape[1:]
        rows = m.astype(BF16).reshape(nc, gc * a, b)
        on_diag = np.kron(np.eye(gc, dtype=bool), np.ones((a, b), dtype=bool))
        return jnp.where(on_diag, jnp.tile(rows, (1, 1, gc)), jnp.zeros((), BF16))

    in_blocks = lambda m: block_diag(jnp.swapaxes(m, 1, 2))
    out_blocks = lambda m: block_diag(jnp.swapaxes(m.astype(F32), 1, 2))

    bcast = lambda a: jnp.broadcast_to(a.reshape(1, G * P), (B, G * P))
    return in_blocks(bbr), in_blocks(bbi), bcast(ar), bcast(ai), out_blocks(c_re), out_blocks(c_im)


def kernel(x, c, rel_bias, final_g, l0_ada_w, l0_ada_b, l0_w_in, l0_conv_w, l0_conv_b, l0_w_out, l1_ada_w, l1_ada_b, l1_w_in, l1_w_out, l2_ada_w, l2_ada_b, l2_w_in, l2_log_dt, l2_lambda_re, l2_lambda_im, l2_b_re, l2_b_im, l2_c_re, l2_c_im, l2_d_skip, l2_w_glu, l2_b_glu, l2_w_out, l3_ada_w, l3_ada_b, l3_w_in, l3_conv_w, l3_conv_b, l3_w_out):
    B, S, D = x.shape
    assert B == SUBLANES, "the S5 scan keeps the batch on the sublane axis"
    mods = _modulation(c, (l0_ada_w, l1_ada_w, l2_ada_w, l3_ada_w), (l0_ada_b, l1_ada_b, l2_ada_b, l3_ada_b))
    mod3 = [m.reshape(B, 1, 3 * D) for m in mods]

    x, w1_in, w1_out = _conv_layer(x, mod3[0], l0_w_in.astype(BF16), l0_conv_w, l0_conv_b,
                                   l0_w_out.astype(BF16), time_major_in=False, cast=(l1_w_in, l1_w_out))

    dils = tuple(dil for _, dil in DILATION_PATTERNS)
    n_qkv = 3 * len(dils)
    E = l1_w_out.shape[0]
    H = E // HEAD_DIM
    res = _qkv_proj(x, mod3[1], w1_in, dils, E, cast=(l2_w_in, l2_w_glu, l2_w_out, l3_w_in, l3_w_out))
    qkv, (w2_in, w2_glu, w2_out, w3_in, w3_out) = res[:n_qkv], res[n_qkv:]
    os_, ms, dens = [], [], []
    for g, (window, dil) in enumerate(DILATION_PATTERNS):
        o, m, den = _attention(*qkv[3 * g:3 * g + 3], rel_bias[:, g * H:(g + 1) * H], window, dil)
        os_.append(o)
        ms.append(m)
        dens.append(den)
    xt = _attn_out(x, mod3[1], os_, ms, dens, w1_in, w1_out)

    ssm = _s5_params(l2_log_dt, l2_lambda_re, l2_lambda_im, l2_b_re, l2_b_im, l2_c_re, l2_c_im, B)
    xt = _s5_layer(xt, mods[2], w2_in, *ssm, l2_d_skip, w2_glu, l2_b_glu, w2_out)

    return _conv_layer(xt, mod3[3], w3_in, l3_conv_w, l3_conv_b, w3_out,
                       time_major_in=True, final_g=final_g)
```

```python
import functools
import math

import numpy as np
import jax
import jax.numpy as jnp
from jax import lax
from jax.experimental import pallas as pl
from jax.experimental.pallas import tpu as pltpu

EPS = 1e-6
HEAD_DIM = 64
ATT_BLOCK = 128
DILATION_PATTERNS = ((128, 1), (512, 4), (2048, 16))
REL_BUCKETS = 32
REL_MAX_DIST = 2048
NEG_INF = -1e30
LOG2E = math.log2(math.e)
LANES = 128
SUBLANES = 8
VMEM_LIMIT_BYTES = 60 * 1024 * 1024

F32 = jnp.float32
BF16 = jnp.bfloat16


def _cparams(semantics):
    return pltpu.CompilerParams(dimension_semantics=semantics, vmem_limit_bytes=VMEM_LIMIT_BYTES)


def _resident(shape):
    nd = len(shape)
    return pl.BlockSpec(shape, lambda *_: (0,) * nd, pipeline_mode=pl.Buffered(1))


def _silu(z):
    return z * (1.0 / (1.0 + jnp.exp(-z)))


def _sigmoid(z):
    return 1.0 / (1.0 + jnp.exp(-z))


def _mod_norm(x, scale, shift):
    ms = jnp.mean(x * x, axis=-1, keepdims=True)
    return (x * lax.rsqrt(ms + EPS)) * (1.0 + scale) + shift


def _dot(a, b):
    return jnp.dot(a, b, preferred_element_type=F32)


def _rows(start, size, stride):
    return pl.ds(start, size) if stride == 1 else pl.ds(start, size, stride=stride)


def _modulation_kernel(c_ref, w0, w1, w2, w3, b0, b1, b2, b3, o0, o1, o2, o3):
    sc = _silu(c_ref[...]).astype(BF16)
    for w, b, o in ((w0, b0, o0), (w1, b1, o1), (w2, b2, o2), (w3, b3, o3)):
        o[...] = _dot(sc, w[...].astype(BF16)) + b[...]


def _modulation(c, ws, bs, tn=512):
    B, D = c.shape
    N = ws[0].shape[1]
    w_spec = pl.BlockSpec((D, tn), lambda j: (0, j))
    b_spec = pl.BlockSpec((1, tn), lambda j: (0, j))
    o_spec = pl.BlockSpec((B, tn), lambda j: (0, j))
    return pl.pallas_call(
        _modulation_kernel,
        grid=(N // tn,),
        in_specs=[pl.BlockSpec((B, D), lambda j: (0, 0))] + [w_spec] * 4 + [b_spec] * 4,
        out_specs=[o_spec] * 4,
        out_shape=[jax.ShapeDtypeStruct((B, N), F32)] * 4,
        compiler_params=_cparams(("arbitrary",)),
        name="modulation",
    )(c, *ws, *[b.reshape(1, N) for b in bs])


def _cast_specs(arrays, n_steps, step_of):
    in_specs, out_specs, out_shape = [], [], []
    for a in arrays:
        rows, cols = a.shape
        spec = pl.BlockSpec((rows // n_steps, cols), lambda *ids: (step_of(*ids), 0))
        in_specs.append(spec)
        out_specs.append(spec)
        out_shape.append(jax.ShapeDtypeStruct(a.shape, BF16))
    return in_specs, out_specs, out_shape


def _cast_slabs(src_refs, dst_refs):
    for src, dst in zip(src_refs, dst_refs):
        dst[...] = src[...].astype(BF16)


def _conv_layer_kernel(*refs, ec, final, n_cast):
    n_in = 9 if final else 8
    (x_ref, shift_ref, scale_ref, gate_ref, w_in_ref, cw_ref, cb_ref, w_out_ref) = refs[:8]
    g_ref = refs[8] if final else None
    cast_src = refs[n_in:n_in + n_cast]
    o_ref = refs[n_in + n_cast]
    cast_dst = refs[n_in + n_cast + 1:n_in + 2 * n_cast + 1]
    carry_ref, vs_ref = refs[n_in + 2 * n_cast + 1:]
    tm, D = x_ref.shape
    E = w_out_ref.shape[0]

    @pl.when(pl.program_id(1) == 0)
    def _():
        carry_ref[...] = jnp.zeros_like(carry_ref)

    _cast_slabs(cast_src, cast_dst)
    x = x_ref[...]
    h = _mod_norm(x, scale_ref[...], shift_ref[...]).astype(BF16)
    acc = jnp.zeros((tm, D), F32)
    for j in range(E // ec):
        c0 = j * ec
        u = _dot(h, w_in_ref[:, c0:c0 + ec])
        gc = _dot(h, w_in_ref[:, E + c0:E + c0 + ec])
        gb = _dot(h, w_in_ref[:, 2 * E + c0:2 * E + c0 + ec])
        z = _dot(h, w_in_ref[:, 3 * E + c0:3 * E + c0 + ec])
        v = gc * u
        vs_ref[0:SUBLANES, :] = carry_ref[j]
        vs_ref[SUBLANES:SUBLANES + tm, :] = v
        carry_ref[j] = v[tm - SUBLANES:tm, :]
        v1 = vs_ref[SUBLANES - 1:SUBLANES - 1 + tm, :]
        v2 = vs_ref[SUBLANES - 2:SUBLANES - 2 + tm, :]
        cw = cw_ref[:, c0:c0 + ec]
        conv = cw[0:1, :] * v2 + cw[1:2, :] * v1 + cw[2:3, :] * v + cb_ref[:, c0:c0 + ec]
        y = gb * conv * _silu(z)
        acc = acc + _dot(y.astype(BF16), w_out_ref[c0:c0 + ec, :])
    out = x + gate_ref[...] * acc
    if final:
        ms = jnp.mean(out * out, axis=-1, keepdims=True)
        out = out * lax.rsqrt(ms + EPS) * g_ref[...]
    o_ref[...] = out


def _conv_layer(x, mod, w_in, conv_w, conv_b, w_out, *, final_g=None, cast=(), tm=1024, ec=512):
    B, S, D = x.shape
    x_spec = pl.BlockSpec((None, tm, D), lambda b, i: (b, i, 0))
    E = w_out.shape[0]
    final = final_g is not None
    mod_specs = [pl.BlockSpec((None, 1, D), functools.partial(lambda b, i, k: (b, 0, k), k=k))
                 for k in range(3)]
    in_specs = [x_spec] + mod_specs + [_resident(w_in.shape), _resident(conv_w.shape),
                                       _resident((1, E)), _resident(w_out.shape)]
    args = [x, mod, mod, mod, w_in, conv_w, conv_b.reshape(1, E), w_out]
    if final:
        in_specs.append(_resident((1, D)))
        args.append(final_g.reshape(1, D))
    n_tiles = S // tm
    cast_in, cast_out, cast_shape = _cast_specs(cast, B * n_tiles, lambda b, i: b * n_tiles + i)
    res = pl.pallas_call(
        functools.partial(_conv_layer_kernel, ec=ec, final=final, n_cast=len(cast)),
        grid=(B, n_tiles),
        in_specs=in_specs + cast_in,
        out_specs=[pl.BlockSpec((None, tm, D), lambda b, i: (b, i, 0))] + cast_out,
        out_shape=[jax.ShapeDtypeStruct((B, S, D), F32)] + cast_shape,
        scratch_shapes=[pltpu.VMEM((E // ec, SUBLANES, ec), F32),
                        pltpu.VMEM((SUBLANES + tm, ec), F32)],
        compiler_params=_cparams(("arbitrary", "arbitrary")),
        name="conv_layer_final" if final else "conv_layer",
    )(*args, *cast)
    return res if cast else res[0]


def _t5_bucket(dist):
    max_exact = REL_BUCKETS // 2
    d = np.maximum(dist, 0)
    ratio = np.log(np.maximum(d, 1) / max_exact) / math.log(REL_MAX_DIST / max_exact)
    large = np.minimum(max_exact + (ratio * (REL_BUCKETS - max_exact)).astype(np.int64), REL_BUCKETS - 1)
    return np.where(d < max_exact, d, large).astype(np.int32)


def _band_structure(window, dil):
    wsub = window // dil
    qi = np.arange(ATT_BLOCK)[:, None]
    ki = np.arange(2 * ATT_BLOCK)[None, :]
    dist = ATT_BLOCK + qi - ki
    in_band = (dist >= 0) & (dist <= wsub)
    first = in_band & (ki >= ATT_BLOCK)
    return _t5_bucket(dist * dil), np.stack([in_band, first]).astype(np.float32)


def _qkv_kernel(x_ref, shift_ref, scale_ref, w_ref, *refs, dils, n_cast):
    n_out = 3 * len(dils)
    cast_src, out_refs = refs[:n_cast], refs[n_cast:n_cast + n_out]
    cast_dst, h_ref = refs[n_cast + n_out:-1], refs[-1]
    tn, D = x_ref.shape
    E = out_refs[0].shape[-1]
    _cast_slabs(cast_src, cast_dst)
    h = _mod_norm(x_ref[...], scale_ref[...], shift_ref[...])
    n_col = D // LANES
    for c in range(n_col):
        h_ref[c] = h[:, c * LANES:(c + 1) * LANES]
    for g, dil in enumerate(dils):
        n = tn // dil
        if dil == 1:
            hp = h
        else:
            hp = jnp.concatenate(
                [jnp.concatenate([h_ref[c, pl.ds(r, n, stride=dil), :] for c in range(n_col)], axis=1)
                 for r in range(dil)], axis=0)
        hp = hp.astype(BF16)
        for j in range(3):
            c0 = (3 * g + j) * E
            res = _dot(hp, w_ref[:, c0:c0 + E])
            if j == 0:
                res = res * (LOG2E / math.sqrt(HEAD_DIM))
            out_refs[3 * g + j][...] = res.astype(BF16).reshape(dil, n, E)


def _qkv_proj(x, mod, w_qkv, dils, E, cast=(), tn=512):
    B, S, D = x.shape
    n_tiles = S // tn
    cast_in, cast_out, cast_shape = _cast_specs(cast, B * n_tiles, lambda b, i: b * n_tiles + i)
    mod_specs = [pl.BlockSpec((None, 1, D), functools.partial(lambda b, i, k: (b, 0, k), k=k))
                 for k in range(2)]
    out_specs, out_shape = [], []
    for dil in dils:
        out_specs += [pl.BlockSpec((None, dil, tn // dil, E), lambda b, i: (b, 0, i, 0))] * 3
        out_shape += [jax.ShapeDtypeStruct((B, dil, S // dil, E), BF16)] * 3
    return pl.pallas_call(
        functools.partial(_qkv_kernel, dils=dils, n_cast=len(cast)),
        grid=(B, n_tiles),
        in_specs=[pl.BlockSpec((None, tn, D), lambda b, i: (b, i, 0))] + mod_specs
                 + [_resident(w_qkv.shape)] + cast_in,
        out_specs=out_specs + cast_out,
        out_shape=out_shape + cast_shape,
        scratch_shapes=[pltpu.VMEM((D // LANES, tn, LANES), F32)],
        compiler_params=_cparams(("arbitrary", "arbitrary")),
        name="qkv_proj",
    )(x, mod, mod, w_qkv, *cast)


def _attn_kernel(*refs, nr, nsub, has_prev, stride, n_heads):
    if has_prev:
        (q_ref, k_ref, v_ref, kp_ref, vp_ref, rb_ref, bucket_ref, mask_ref,
         o_ref, m_ref, den_ref, tbl_ref, s_ref, p_ref, stat_ref) = refs
    else:
        (q_ref, k_ref, v_ref, rb_ref, bucket_ref, mask_ref,
         o_ref, m_ref, den_ref, tbl_ref, s_ref, p_ref, stat_ref) = refs
    n_pairs = n_heads // 2
    pair_rows = 2 * ATT_BLOCK
    nk = s_ref.shape[1]
    step = pl.program_id(1)

    @pl.when((pl.program_id(0) == 0) & (step == 0))
    def _():
        def fill(b, carry):
            hit = bucket_ref[...] == b
            for h in range(n_heads):
                rows = slice(h * ATT_BLOCK, (h + 1) * ATT_BLOCK)
                s_ref[rows, :] = jnp.where(hit, rb_ref[b, h] * LOG2E, s_ref[rows, :])
            return carry
        s_ref[...] = jnp.zeros_like(s_ref)
        lax.fori_loop(0, REL_BUCKETS, fill, 0)
        for t in range(tbl_ref.shape[0]):
            for h in range(n_heads):
                rows = slice(h * ATT_BLOCK, (h + 1) * ATT_BLOCK)
                tbl_ref[t, rows, :] = jnp.where(mask_ref[t] > 0.0, s_ref[rows, :], NEG_INF)
        stat_ref[0] = jnp.zeros((ATT_BLOCK, LANES), F32)
        stat_ref[1] = jnp.ones((ATT_BLOCK, LANES), F32)

    lane = lax.broadcasted_iota(jnp.int32, (ATT_BLOCK, LANES), 1)
    lo = lane < HEAD_DIM

    def block(q_of, k_of, v_of, table, out_rows):
        for hp in range(n_pairs):
            cols = slice(hp * LANES, (hp + 1) * LANES)
            rows = slice(hp * pair_rows, (hp + 1) * pair_rows)
            q2 = q_of(cols)
            zero = jnp.zeros_like(q2)
            q_pair = jnp.concatenate([jnp.where(lo, q2, zero), jnp.where(lo, zero, q2)], axis=0)
            s_ref[rows, :] = lax.dot_general(q_pair, k_of(cols), (((1,), (1,)), ((), ())),
                                             preferred_element_type=F32)
        head_rows = [slice(h * ATT_BLOCK, (h + 1) * ATT_BLOCK) for h in range(n_heads)]
        if not has_prev:
            maxima = [jnp.max(s_ref[rows, :] + tbl_ref[table, rows, :], axis=-1, keepdims=True)
                      for rows in head_rows]
        for h, rows in enumerate(head_rows):
            s = s_ref[rows, :] + tbl_ref[table, rows, :]
            m = jnp.max(s, axis=-1, keepdims=True) if has_prev else maxima[h]
            p = jnp.exp2(s - m)
            den = jnp.sum(p, axis=-1, keepdims=True)
            p_ref[rows, :] = p.astype(BF16)
            stat_ref[0, :, h:h + 1] = m
            stat_ref[1, :, h:h + 1] = den
        m_ref[out_rows, :] = stat_ref[0]
        den_ref[out_rows, :] = stat_ref[1]
        for hp in range(n_pairs):
            cols = slice(hp * LANES, (hp + 1) * LANES)
            rows = slice(hp * pair_rows, (hp + 1) * pair_rows)
            res = _dot(p_ref[rows, :], v_of(cols))
            o_ref[hp, out_rows, :] = jnp.where(lo, res[:ATT_BLOCK], res[ATT_BLOCK:])

    for r in range(nr):
        for j in range(nsub):
            row = j * ATT_BLOCK
            q_of = lambda cols, r=r, row=row: q_ref[r, row:row + ATT_BLOCK, cols]
            if has_prev:
                out_rows = _rows(row * stride + r, ATT_BLOCK, stride)
                if j == 0:
                    k_of = lambda cols, r=r: jnp.concatenate([kp_ref[r, :, cols], k_ref[r, 0:ATT_BLOCK, cols]], axis=0)
                    v_of = lambda cols, r=r: jnp.concatenate([vp_ref[r, :, cols], v_ref[r, 0:ATT_BLOCK, cols]], axis=0)
                    table = jnp.where(step == 0, 1, 0)
                else:
                    k_of = lambda cols, r=r, row=row: k_ref[r, row - ATT_BLOCK:row + ATT_BLOCK, cols]
                    v_of = lambda cols, r=r, row=row: v_ref[r, row - ATT_BLOCK:row + ATT_BLOCK, cols]
                    table = 0
            else:
                out_rows = _rows(step * nr + r, ATT_BLOCK, stride)
                k_of = lambda cols, r=r: k_ref[r, :, cols]
                v_of = lambda cols, r=r: v_ref[r, :, cols]
                table = 0
            block(q_of, k_of, v_of, table, out_rows)


def _attention(q, k, v, rel_bias_g, window, dil, tokens_per_step=512):
    B, _, L, E = q.shape
    S = L * dil
    H = E // HEAD_DIM
    bucket, masks = _band_structure(window, dil)
    has_prev = L > ATT_BLOCK
    if has_prev:
        nr = dil
        nsub = tokens_per_step // (dil * ATT_BLOCK)
        tq = nsub * ATT_BLOCK
        n_steps = L // tq
        nk = 2 * ATT_BLOCK
        qkv_spec = pl.BlockSpec((None, nr, tq, E), lambda b, i: (b, 0, i, 0))
        prev_spec = pl.BlockSpec((None, nr, ATT_BLOCK, E),
                                 lambda b, i: (b, 0, jnp.maximum(i * nsub - 1, 0), 0))
        in_specs = [qkv_spec] * 3 + [prev_spec] * 2
        args = [q, k, v, k, v]
        out_block_rows = tokens_per_step
        out_step = lambda i: i
    else:
        nr = 4
        nsub = 1
        n_steps = dil // nr
        nk = ATT_BLOCK
        bucket, masks = bucket[:, ATT_BLOCK:], masks[1:, :, ATT_BLOCK:]
        qkv_spec = pl.BlockSpec((None, nr, ATT_BLOCK, E), lambda b, i: (b, i, 0, 0))
        in_specs = [qkv_spec] * 3
        args = [q, k, v]
        out_block_rows = S
        out_step = lambda i: 0
    in_specs += [pl.BlockSpec(memory_space=pltpu.SMEM), _resident(bucket.shape), _resident(masks.shape)]
    args += [rel_bias_g, jnp.asarray(bucket), jnp.asarray(masks)]
    stat_spec = pl.BlockSpec((None, out_block_rows, LANES), lambda b, i: (b, out_step(i), 0))
    stat_shape = jax.ShapeDtypeStruct((B, S, LANES), F32)
    return pl.pallas_call(
        functools.partial(_attn_kernel, nr=nr, nsub=nsub, has_prev=has_prev, stride=dil, n_heads=H),
        grid=(B, n_steps),
        in_specs=in_specs,
        out_specs=[pl.BlockSpec((None, E // LANES, out_block_rows, LANES), lambda b, i: (b, 0, out_step(i), 0)),
                   stat_spec, stat_spec],
        out_shape=[jax.ShapeDtypeStruct((B, E // LANES, S, LANES), F32), stat_shape, stat_shape],
        scratch_shapes=[pltpu.VMEM((masks.shape[0], H * ATT_BLOCK, nk), F32),
                        pltpu.VMEM((H * ATT_BLOCK, nk), F32),
                        pltpu.VMEM((H * ATT_BLOCK, nk), BF16),
                        pltpu.VMEM((2, ATT_BLOCK, LANES), F32)],
        compiler_params=_cparams(("arbitrary", "arbitrary")),
        name=f"attention_d{dil}",
    )(*args)


def _attn_out_kernel(x_ref, shift_ref, scale_ref, gate_ref, o0_ref, o1_ref, o2_ref,
                     m0_ref, m1_ref, m2_ref, d0_ref, d1_ref, d2_ref, wz_ref, w_out_ref, expand_ref,
                     out_ref):
    tm, D = x_ref.shape
    E = wz_ref.shape[1]
    n_heads = E // HEAD_DIM
    x = x_ref[...]
    h = _mod_norm(x, scale_ref[...], shift_ref[...]).astype(BF16)
    z = _dot(h, wz_ref[...])
    m0, m1, m2 = m0_ref[...], m1_ref[...], m2_ref[...]
    m = jnp.maximum(jnp.maximum(m0, m1), m2)
    e0, e1, e2 = jnp.exp2(m0 - m), jnp.exp2(m1 - m), jnp.exp2(m2 - m)
    den = e0 * d0_ref[...] + e1 * d1_ref[...] + e2 * d2_ref[...]
    head_lane = lax.broadcasted_iota(jnp.int32, (tm, LANES), 1) < n_heads
    packed = None
    for g, e in enumerate((e0, e1, e2)):
        w = jnp.where(head_lane, e / den, 0.0)
        hi = w.astype(BF16).astype(F32)
        for part, val in enumerate((hi, w - hi)):
            shift = (2 * g + part) * n_heads
            val = pltpu.roll(val, shift, axis=1) if shift else val
            packed = val if packed is None else packed + val
    wexp = _dot(packed.astype(BF16), expand_ref[...])
    o = None
    for g, o_ref in enumerate((o0_ref, o1_ref, o2_ref)):
        og = jnp.concatenate([o_ref[hp] for hp in range(E // LANES)], axis=-1)
        term = wexp[:, g * E:(g + 1) * E] * og
        o = term if o is None else o + term
    y = (o * _silu(z)).astype(BF16)
    out_ref[...] = x + gate_ref[...] * _dot(y, w_out_ref[...])


def _attn_out(x, mod, os_, ms, dens, w_in, w_out, tm=512):
    B, S, D = x.shape
    E = w_out.shape[0]
    z_block = w_in.shape[1] // E - 1
    tile = lambda w: pl.BlockSpec((None, tm, w), lambda b, i: (b, i, 0))
    mod_specs = [pl.BlockSpec((None, 1, D), functools.partial(lambda b, i, k: (b, 0, k), k=k))
                 for k in range(3)]
    n_groups, H = len(os_), E // HEAD_DIM
    expand = np.zeros((LANES, n_groups * E), np.float32)
    for g in range(n_groups):
        for part in range(2):
            for hd in range(H):
                expand[(2 * g + part) * H + hd, g * E + hd * HEAD_DIM:g * E + (hd + 1) * HEAD_DIM] = 1.0
    expand = jnp.asarray(expand, BF16)
    return pl.pallas_call(
        _attn_out_kernel,
        grid=(B, S // tm),
        in_specs=[tile(D)] + mod_specs
                 + [pl.BlockSpec((None, E // LANES, tm, LANES), lambda b, i: (b, 0, i, 0))] * 3
                 + [tile(LANES)] * 6
                 + [pl.BlockSpec((D, E), lambda b, i: (0, z_block), pipeline_mode=pl.Buffered(1)),
                    _resident(w_out.shape), _resident(expand.shape)],
        out_specs=tile(D),
        out_shape=jax.ShapeDtypeStruct((B, S, D), F32),
        compiler_params=_cparams(("arbitrary", "arbitrary")),
        name="attention_out",
    )(x, mod, mod, mod, *os_, *ms, *dens, w_in, w_out, expand)


def _s5_kernel(x_ref, shift_ref, scale_ref, gate_ref, w_in_ref, bbr_ref, bbi_ref, ar_ref, ai_ref,
               cr_ref, ci_ref, dskip_ref, w_glu_ref, b_glu_ref, w_out_ref,
               o_ref, state_ref, sr_ref, si_ref, *, scan_lanes):
    B, ts, D = x_ref.shape
    R = ts * B
    n_chunks, cn, ck = bbr_ref.shape
    E = n_chunks * ck
    dot_t = lambda a, b: lax.dot_general(a, b, (((1,), (1,)), ((), ())), preferred_element_type=F32)

    @pl.when(pl.program_id(0) == 0)
    def _():
        state_ref[...] = jnp.zeros_like(state_ref)

    x3 = jnp.swapaxes(x_ref[...], 0, 1)
    ms = jnp.mean(x3 * x3, axis=-1, keepdims=True)
    h3 = (x3 * lax.rsqrt(ms + EPS)) * (1.0 + scale_ref[...][None]) + shift_ref[...][None]
    h = h3.reshape(R, D).astype(BF16)
    uz = _dot(h, w_in_ref[...])
    u, z = uz[:, :E], uz[:, E:]
    u16 = u.astype(BF16)

    def in_proj(c):
        uc = u16[:, c * ck:(c + 1) * ck]
        sr_ref[c] = dot_t(uc, bbr_ref[c])
        si_ref[c] = dot_t(uc, bbi_ref[c])

    def scan(c):
        for l0 in range(0, cn, scan_lanes):
            ls = slice(l0, l0 + scan_lanes)
            gl = slice(c * cn + l0, c * cn + l0 + scan_lanes)
            ar, ai = ar_ref[:, gl], ai_ref[:, gl]
            pr, pi = state_ref[0, :, gl], state_ref[1, :, gl]
            for t in range(ts):
                rows = slice(t * B, (t + 1) * B)
                nr = ar * pr - ai * pi + sr_ref[c, rows, ls]
                ni = ar * pi + ai * pr + si_ref[c, rows, ls]
                sr_ref[c, rows, ls] = nr
                si_ref[c, rows, ls] = ni
                pr, pi = nr, ni
            state_ref[0, :, gl] = pr
            state_ref[1, :, gl] = pi

    def out_proj(c):
        return dot_t(sr_ref[c].astype(BF16), cr_ref[c]) - dot_t(si_ref[c].astype(BF16), ci_ref[c])

    y_parts = []
    in_proj(0)
    for c in range(n_chunks):
        if c + 1 < n_chunks:
            in_proj(c + 1)
        scan(c)
        if c >= 1:
            y_parts.append(out_proj(c - 1))
    y_parts.append(out_proj(n_chunks - 1))
    y = jnp.concatenate(y_parts, axis=-1) + dskip_ref[...] * u
    g = y * (0.5 * (1.0 + jnp.tanh(math.sqrt(2.0 / math.pi) * (y + 0.044715 * (y * y * y)))))
    y = g * _sigmoid(_dot(g.astype(BF16), w_glu_ref[...]) + b_glu_ref[...])
    y = (y * _silu(z)).astype(BF16)
    out = _dot(y, w_out_ref[...]).reshape(ts, B, D)
    o_ref[...] = jnp.swapaxes(x3 + gate_ref[...][None] * out, 0, 1)


def _s5_layer(x, mod, w_in, bbr, bbi, ar, ai, cr, ci, d_skip, w_glu, b_glu, w_out, ts=64, scan_lanes=512):
    B, S, D = x.shape
    E = w_glu.shape[0]
    N = ar.shape[1]
    mod_specs = [pl.BlockSpec((B, D), functools.partial(lambda i, k: (0, k), k=k)) for k in range(3)]
    weights = [w_in, bbr, bbi, ar, ai, cr, ci, d_skip.reshape(1, E), w_glu, b_glu.reshape(1, E), w_out]
    return pl.pallas_call(
        functools.partial(_s5_kernel, scan_lanes=scan_lanes),
        grid=(S // ts,),
        in_specs=[pl.BlockSpec((B, ts, D), lambda i: (0, i, 0))] + mod_specs
                 + [_resident(w.shape) for w in weights],
        out_specs=pl.BlockSpec((B, ts, D), lambda i: (0, i, 0)),
        out_shape=jax.ShapeDtypeStruct((B, S, D), F32),
        scratch_shapes=[pltpu.VMEM((2, B, N), F32),
                        pltpu.VMEM((bbr.shape[0], ts * B, bbr.shape[1]), F32),
                        pltpu.VMEM((bbr.shape[0], ts * B, bbr.shape[1]), F32)],
        compiler_params=_cparams(("arbitrary",)),
        name="s5_layer",
    )(x, mod, mod, mod, *weights)


def _s5_params(log_dt, lambda_re, lambda_im, b_re, b_im, c_re, c_im, B, groups_per_chunk=16):
    G, P = lambda_re.shape
    K = b_re.shape[2]
    dt = jnp.exp(log_dt.astype(F32))[:, None]
    lr = lambda_re.astype(F32)
    li = lambda_im.astype(F32)
    mag = jnp.exp(lr * dt)
    ar = mag * jnp.cos(li * dt)
    ai = mag * jnp.sin(li * dt)
    inv = 1.0 / (lr * lr + li * li)
    qr = ((ar - 1.0) * lr + ai * li) * inv
    qi = (ai * lr - (ar - 1.0) * li) * inv
    br = b_re.astype(F32)
    bi = b_im.astype(F32)
    bbr = qr[..., None] * br - qi[..., None] * bi
    bbi = qr[..., None] * bi + qi[..., None] * br
    gc = groups_per_chunk
    nc = G // gc

    def block_diag(m):
        a, b = m.shape[1:]
        rows = m.astype(BF16).reshape(nc, gc * a, b)
        on_diag = np.kron(np.eye(gc, dtype=bool), np.ones((a, b), dtype=bool))
        return jnp.where(on_diag, jnp.tile(rows, (1, 1, gc)), jnp.zeros((), BF16))

    bcast = lambda a: jnp.broadcast_to(a.reshape(1, G * P), (B, G * P))
    return (block_diag(bbr), block_diag(bbi), bcast(ar), bcast(ai),
            block_diag(c_re.astype(F32)), block_diag(c_im.astype(F32)))


def kernel(x, c, rel_bias, final_g, l0_ada_w, l0_ada_b, l0_w_in, l0_conv_w, l0_conv_b, l0_w_out, l1_ada_w, l1_ada_b, l1_w_in, l1_w_out, l2_ada_w, l2_ada_b, l2_w_in, l2_log_dt, l2_lambda_re, l2_lambda_im, l2_b_re, l2_b_im, l2_c_re, l2_c_im, l2_d_skip, l2_w_glu, l2_b_glu, l2_w_out, l3_ada_w, l3_ada_b, l3_w_in, l3_conv_w, l3_conv_b, l3_w_out):
    B, S, D = x.shape
    assert B == SUBLANES, "the S5 scan keeps the batch on the sublane axis"
    mods = _modulation(c, (l0_ada_w, l1_ada_w, l2_ada_w, l3_ada_w), (l0_ada_b, l1_ada_b, l2_ada_b, l3_ada_b))
    mod3 = [m.reshape(B, 1, 3 * D) for m in mods]

    x, w1_in, w1_out = _conv_layer(x, mod3[0], l0_w_in.astype(BF16), l0_conv_w, l0_conv_b,
                                   l0_w_out.astype(BF16), cast=(l1_w_in, l1_w_out))

    dils = tuple(dil for _, dil in DILATION_PATTERNS)
    n_qkv = 3 * len(dils)
    E = l1_w_out.shape[0]
    H = E // HEAD_DIM
    res = _qkv_proj(x, mod3[1], w1_in, dils, E, cast=(l2_w_in, l2_w_glu, l2_w_out, l3_w_in, l3_w_out))
    qkv, (w2_in, w2_glu, w2_out, w3_in, w3_out) = res[:n_qkv], res[n_qkv:]
    os_, ms, dens = [], [], []
    for g, (window, dil) in enumerate(DILATION_PATTERNS):
        o, m, den = _attention(*qkv[3 * g:3 * g + 3], rel_bias[:, g * H:(g + 1) * H], window, dil)
        os_.append(o)
        ms.append(m)
        dens.append(den)
    x = _attn_out(x, mod3[1], os_, ms, dens, w1_in, w1_out)

    ssm = _s5_params(l2_log_dt, l2_lambda_re, l2_lambda_im, l2_b_re, l2_b_im, l2_c_re, l2_c_im, B)
    x = _s5_layer(x, mods[2], w2_in, *ssm, l2_d_skip, w2_glu, l2_b_glu, w2_out)

    return _conv_layer(x, mod3[3], w3_in, l3_conv_w, l3_conv_b, w3_out, final_g=final_g)
```

```python
import functools
import math

import numpy as np
import jax
import jax.numpy as jnp
from jax import lax
from jax.experimental import pallas as pl
from jax.experimental.pallas import tpu as pltpu

EPS = 1e-6
HEAD_DIM = 64
ATT_BLOCK = 128
DILATION_PATTERNS = ((128, 1), (512, 4), (2048, 16))
REL_BUCKETS = 32
REL_MAX_DIST = 2048
NEG_INF = -1e30
LOG2E = math.log2(math.e)
LANES = 128
SUBLANES = 8
VMEM_LIMIT_BYTES = 60 * 1024 * 1024

F32 = jnp.float32
BF16 = jnp.bfloat16


def _cparams(semantics):
    return pltpu.CompilerParams(dimension_semantics=semantics, vmem_limit_bytes=VMEM_LIMIT_BYTES)


def _resident(shape):
    nd = len(shape)
    return pl.BlockSpec(shape, lambda *_: (0,) * nd, pipeline_mode=pl.Buffered(1))


def _silu(z):
    return z * (1.0 / (1.0 + jnp.exp(-z)))


def _sigmoid(z):
    return 1.0 / (1.0 + jnp.exp(-z))


def _mod_norm(x, scale, shift):
    ms = jnp.mean(x * x, axis=-1, keepdims=True)
    return (x * lax.rsqrt(ms + EPS)) * (1.0 + scale) + shift


def _dot(a, b):
    return jnp.dot(a, b, preferred_element_type=F32)


def _rows(start, size, stride):
    return pl.ds(start, size) if stride == 1 else pl.ds(start, size, stride=stride)


def _modulation_kernel(c_ref, w0, w1, w2, w3, b0, b1, b2, b3, o0, o1, o2, o3):
    sc = _silu(c_ref[...]).astype(BF16)
    for w, b, o in ((w0, b0, o0), (w1, b1, o1), (w2, b2, o2), (w3, b3, o3)):
        o[...] = _dot(sc, w[...].astype(BF16)) + b[...]


def _modulation(c, ws, bs, tn=512):
    B, D = c.shape
    N = ws[0].shape[1]
    w_spec = pl.BlockSpec((D, tn), lambda j: (0, j))
    b_spec = pl.BlockSpec((1, tn), lambda j: (0, j))
    o_spec = pl.BlockSpec((B, tn), lambda j: (0, j))
    return pl.pallas_call(
        _modulation_kernel,
        grid=(N // tn,),
        in_specs=[pl.BlockSpec((B, D), lambda j: (0, 0))] + [w_spec] * 4 + [b_spec] * 4,
        out_specs=[o_spec] * 4,
        out_shape=[jax.ShapeDtypeStruct((B, N), F32)] * 4,
        compiler_params=_cparams(("arbitrary",)),
        name="modulation",
    )(c, *ws, *[b.reshape(1, N) for b in bs])


def _cast_specs(arrays, n_steps, step_of):
    in_specs, out_specs, out_shape = [], [], []
    for a in arrays:
        rows, cols = a.shape
        spec = pl.BlockSpec((rows // n_steps, cols), lambda *ids: (step_of(*ids), 0))
        in_specs.append(spec)
        out_specs.append(spec)
        out_shape.append(jax.ShapeDtypeStruct(a.shape, BF16))
    return in_specs, out_specs, out_shape


def _cast_slabs(src_refs, dst_refs):
    for src, dst in zip(src_refs, dst_refs):
        dst[...] = src[...].astype(BF16)


def _conv_layer_kernel(*refs, ec, final, n_cast):
    n_in = 9 if final else 8
    (x_ref, shift_ref, scale_ref, gate_ref, w_in_ref, cw_ref, cb_ref, w_out_ref) = refs[:8]
    g_ref = refs[8] if final else None
    cast_src = refs[n_in:n_in + n_cast]
    o_ref = refs[n_in + n_cast]
    cast_dst = refs[n_in + n_cast + 1:n_in + 2 * n_cast + 1]
    carry_ref, vs_ref = refs[n_in + 2 * n_cast + 1:]
    tm, D = x_ref.shape
    E = w_out_ref.shape[0]

    @pl.when(pl.program_id(1) == 0)
    def _():
        carry_ref[...] = jnp.zeros_like(carry_ref)

    _cast_slabs(cast_src, cast_dst)
    x = x_ref[...]
    h = _mod_norm(x, scale_ref[...], shift_ref[...]).astype(BF16)
    n_chunks = E // ec

    def in_proj(j):
        c0 = j * ec
        return tuple(_dot(h, w_in_ref[:, k * E + c0:k * E + c0 + ec]) for k in range(4))

    def gated_conv(j, proj):
        u, gc, gb, z = proj
        c0 = j * ec
        v = gc * u
        vs_ref[0:SUBLANES, :] = carry_ref[j]
        vs_ref[SUBLANES:SUBLANES + tm, :] = v
        carry_ref[j] = v[tm - SUBLANES:tm, :]
        v1 = vs_ref[SUBLANES - 1:SUBLANES - 1 + tm, :]
        v2 = vs_ref[SUBLANES - 2:SUBLANES - 2 + tm, :]
        cw = cw_ref[:, c0:c0 + ec]
        conv = cw[0:1, :] * v2 + cw[1:2, :] * v1 + cw[2:3, :] * v + cb_ref[:, c0:c0 + ec]
        return (gb * conv * _silu(z)).astype(BF16)

    acc = jnp.zeros((tm, D), F32)
    proj = in_proj(0)
    for j in range(n_chunks):
        nxt = in_proj(j + 1) if j + 1 < n_chunks else None
        y = gated_conv(j, proj)
        acc = acc + _dot(y, w_out_ref[j * ec:(j + 1) * ec, :])
        proj = nxt
    out = x + gate_ref[...] * acc
    if final:
        ms = jnp.mean(out * out, axis=-1, keepdims=True)
        out = out * lax.rsqrt(ms + EPS) * g_ref[...]
    o_ref[...] = out


def _conv_layer(x, mod, w_in, conv_w, conv_b, w_out, *, final_g=None, cast=(), tm=1024, ec=256):
    B, S, D = x.shape
    x_spec = pl.BlockSpec((None, tm, D), lambda b, i: (b, i, 0))
    E = w_out.shape[0]
    final = final_g is not None
    mod_specs = [pl.BlockSpec((None, 1, D), functools.partial(lambda b, i, k: (b, 0, k), k=k))
                 for k in range(3)]
    in_specs = [x_spec] + mod_specs + [_resident(w_in.shape), _resident(conv_w.shape),
                                       _resident((1, E)), _resident(w_out.shape)]
    args = [x, mod, mod, mod, w_in, conv_w, conv_b.reshape(1, E), w_out]
    if final:
        in_specs.append(_resident((1, D)))
        args.append(final_g.reshape(1, D))
    n_tiles = S // tm
    cast_in, cast_out, cast_shape = _cast_specs(cast, B * n_tiles, lambda b, i: b * n_tiles + i)
    res = pl.pallas_call(
        functools.partial(_conv_layer_kernel, ec=ec, final=final, n_cast=len(cast)),
        grid=(B, n_tiles),
        in_specs=in_specs + cast_in,
        out_specs=[pl.BlockSpec((None, tm, D), lambda b, i: (b, i, 0))] + cast_out,
        out_shape=[jax.ShapeDtypeStruct((B, S, D), F32)] + cast_shape,
        scratch_shapes=[pltpu.VMEM((E // ec, SUBLANES, ec), F32),
                        pltpu.VMEM((SUBLANES + tm, ec), F32)],
        compiler_params=_cparams(("arbitrary", "arbitrary")),
        name="conv_layer_final" if final else "conv_layer",
    )(*args, *cast)
    return res if cast else res[0]


def _t5_bucket(dist):
    max_exact = REL_BUCKETS // 2
    d = np.maximum(dist, 0)
    ratio = np.log(np.maximum(d, 1) / max_exact) / math.log(REL_MAX_DIST / max_exact)
    large = np.minimum(max_exact + (ratio * (REL_BUCKETS - max_exact)).astype(np.int64), REL_BUCKETS - 1)
    return np.where(d < max_exact, d, large).astype(np.int32)


def _band_structure(window, dil):
    wsub = window // dil
    qi = np.arange(ATT_BLOCK)[:, None]
    ki = np.arange(2 * ATT_BLOCK)[None, :]
    dist = ATT_BLOCK + qi - ki
    in_band = (dist >= 0) & (dist <= wsub)
    first = in_band & (ki >= ATT_BLOCK)
    return _t5_bucket(dist * dil), np.stack([in_band, first]).astype(np.float32)


def _qkv_kernel(x_ref, shift_ref, scale_ref, w_ref, *refs, dils, n_cast):
    n_out = 3 * len(dils)
    cast_src, out_refs = refs[:n_cast], refs[n_cast:n_cast + n_out]
    cast_dst, h_ref = refs[n_cast + n_out:-1], refs[-1]
    tn, D = x_ref.shape
    E = out_refs[0].shape[-1]
    _cast_slabs(cast_src, cast_dst)
    h = _mod_norm(x_ref[...], scale_ref[...], shift_ref[...])
    n_col = D // LANES
    for c in range(n_col):
        h_ref[c] = h[:, c * LANES:(c + 1) * LANES]
    for g, dil in enumerate(dils):
        n = tn // dil
        if dil == 1:
            hp = h
        else:
            hp = jnp.concatenate(
                [jnp.concatenate([h_ref[c, pl.ds(r, n, stride=dil), :] for c in range(n_col)], axis=1)
                 for r in range(dil)], axis=0)
        hp = hp.astype(BF16)
        for j in range(3):
            c0 = (3 * g + j) * E
            res = _dot(hp, w_ref[:, c0:c0 + E])
            if j == 0:
                res = res * (LOG2E / math.sqrt(HEAD_DIM))
            out_refs[3 * g + j][...] = res.astype(BF16).reshape(dil, n, E)


def _qkv_proj(x, mod, w_qkv, dils, E, cast=(), tn=512):
    B, S, D = x.shape
    n_tiles = S // tn
    cast_in, cast_out, cast_shape = _cast_specs(cast, B * n_tiles, lambda b, i: b * n_tiles + i)
    mod_specs = [pl.BlockSpec((None, 1, D), functools.partial(lambda b, i, k: (b, 0, k), k=k))
                 for k in range(2)]
    out_specs, out_shape = [], []
    for dil in dils:
        out_specs += [pl.BlockSpec((None, dil, tn // dil, E), lambda b, i: (b, 0, i, 0))] * 3
        out_shape += [jax.ShapeDtypeStruct((B, dil, S // dil, E), BF16)] * 3
    return pl.pallas_call(
        functools.partial(_qkv_kernel, dils=dils, n_cast=len(cast)),
        grid=(B, n_tiles),
        in_specs=[pl.BlockSpec((None, tn, D), lambda b, i: (b, i, 0))] + mod_specs
                 + [_resident(w_qkv.shape)] + cast_in,
        out_specs=out_specs + cast_out,
        out_shape=out_shape + cast_shape,
        scratch_shapes=[pltpu.VMEM((D // LANES, tn, LANES), F32)],
        compiler_params=_cparams(("arbitrary", "arbitrary")),
        name="qkv_proj",
    )(x, mod, mod, w_qkv, *cast)


def _attn_kernel(*refs, nr, nsub, has_prev, stride, n_heads):
    if has_prev:
        (q_ref, k_ref, v_ref, kp_ref, vp_ref, rb_ref, bucket_ref, mask_ref,
         o_ref, m_ref, den_ref, tbl_ref, s_ref, p_ref, stat_ref) = refs
    else:
        (q_ref, k_ref, v_ref, rb_ref, bucket_ref, mask_ref,
         o_ref, m_ref, den_ref, tbl_ref, s_ref, p_ref, stat_ref) = refs
    n_pairs = n_heads // 2
    pair_rows = 2 * ATT_BLOCK
    nk = s_ref.shape[1]
    step = pl.program_id(1)

    @pl.when((pl.program_id(0) == 0) & (step == 0))
    def _():
        def fill(b, carry):
            hit = bucket_ref[...] == b
            for h in range(n_heads):
                rows = slice(h * ATT_BLOCK, (h + 1) * ATT_BLOCK)
                s_ref[rows, :] = jnp.where(hit, rb_ref[b, h] * LOG2E, s_ref[rows, :])
            return carry
        s_ref[...] = jnp.zeros_like(s_ref)
        lax.fori_loop(0, REL_BUCKETS, fill, 0)
        for t in range(tbl_ref.shape[0]):
            for h in range(n_heads):
                rows = slice(h * ATT_BLOCK, (h + 1) * ATT_BLOCK)
                tbl_ref[t, rows, :] = jnp.where(mask_ref[t] > 0.0, s_ref[rows, :], NEG_INF)
        stat_ref[0] = jnp.zeros((ATT_BLOCK, LANES), F32)
        stat_ref[1] = jnp.ones((ATT_BLOCK, LANES), F32)

    lane = lax.broadcasted_iota(jnp.int32, (ATT_BLOCK, LANES), 1)
    lo = lane < HEAD_DIM

    def block(q_of, k_of, v_of, table, out_rows):
        for hp in range(n_pairs):
            cols = slice(hp * LANES, (hp + 1) * LANES)
            rows = slice(hp * pair_rows, (hp + 1) * pair_rows)
            q2 = q_of(cols)
            zero = jnp.zeros_like(q2)
            q_pair = jnp.concatenate([jnp.where(lo, q2, zero), jnp.where(lo, zero, q2)], axis=0)
            s_ref[rows, :] = lax.dot_general(q_pair, k_of(cols), (((1,), (1,)), ((), ())),
                                             preferred_element_type=F32)
        head_rows = [slice(h * ATT_BLOCK, (h + 1) * ATT_BLOCK) for h in range(n_heads)]
        if not has_prev:
            maxima = []
            for rows in head_rows:
                s = s_ref[rows, :] + tbl_ref[table, rows, :]
                s_ref[rows, :] = s
                maxima.append(jnp.max(s, axis=-1, keepdims=True))
        for h, rows in enumerate(head_rows):
            if has_prev:
                s = s_ref[rows, :] + tbl_ref[table, rows, :]
                m = jnp.max(s, axis=-1, keepdims=True)
            else:
                s, m = s_ref[rows, :], maxima[h]
            p = jnp.exp2(s - m)
            den = jnp.sum(p, axis=-1, keepdims=True)
            p_ref[rows, :] = p.astype(BF16)
            stat_ref[0, :, h:h + 1] = m
            stat_ref[1, :, h:h + 1] = den
        m_ref[out_rows, :] = stat_ref[0]
        den_ref[out_rows, :] = stat_ref[1]
        for hp in range(n_pairs):
            cols = slice(hp * LANES, (hp + 1) * LANES)
            rows = slice(hp * pair_rows, (hp + 1) * pair_rows)
            res = _dot(p_ref[rows, :], v_of(cols))
            o_ref[hp, out_rows, :] = jnp.where(lo, res[:ATT_BLOCK], res[ATT_BLOCK:])

    for r in range(nr):
        for j in range(nsub):
            row = j * ATT_BLOCK
            q_of = lambda cols, r=r, row=row: q_ref[r, row:row + ATT_BLOCK, cols]
            if has_prev:
                out_rows = _rows(row * stride + r, ATT_BLOCK, stride)
                if j == 0:
                    k_of = lambda cols, r=r: jnp.concatenate([kp_ref[r, :, cols], k_ref[r, 0:ATT_BLOCK, cols]], axis=0)
                    v_of = lambda cols, r=r: jnp.concatenate([vp_ref[r, :, cols], v_ref[r, 0:ATT_BLOCK, cols]], axis=0)
                    table = jnp.where(step == 0, 1, 0)
                else:
                    k_of = lambda cols, r=r, row=row: k_ref[r, row - ATT_BLOCK:row + ATT_BLOCK, cols]
                    v_of = lambda cols, r=r, row=row: v_ref[r, row - ATT_BLOCK:row + ATT_BLOCK, cols]
                    table = 0
            else:
                out_rows = _rows(step * nr + r, ATT_BLOCK, stride)
                k_of = lambda cols, r=r: k_ref[r, :, cols]
                v_of = lambda cols, r=r: v_ref[r, :, cols]
                table = 0
            block(q_of, k_of, v_of, table, out_rows)


def _attention(q, k, v, rel_bias_g, window, dil, tokens_per_step=512):
    B, _, L, E = q.shape
    S = L * dil
    H = E // HEAD_DIM
    bucket, masks = _band_structure(window, dil)
    has_prev = L > ATT_BLOCK
    if has_prev:
        nr = dil
        nsub = tokens_per_step // (dil * ATT_BLOCK)
        tq = nsub * ATT_BLOCK
        n_steps = L // tq
        nk = 2 * ATT_BLOCK
        qkv_spec = pl.BlockSpec((None, nr, tq, E), lambda b, i: (b, 0, i, 0))
        prev_spec = pl.BlockSpec((None, nr, ATT_BLOCK, E),
                                 lambda b, i: (b, 0, jnp.maximum(i * nsub - 1, 0), 0))
        in_specs = [qkv_spec] * 3 + [prev_spec] * 2
        args = [q, k, v, k, v]
        out_block_rows = tokens_per_step
        out_step = lambda i: i
    else:
        nr = 4
        nsub = 1
        n_steps = dil // nr
        nk = ATT_BLOCK
        bucket, masks = bucket[:, ATT_BLOCK:], masks[1:, :, ATT_BLOCK:]
        qkv_spec = pl.BlockSpec((None, nr, ATT_BLOCK, E), lambda b, i: (b, i, 0, 0))
        in_specs = [qkv_spec] * 3
        args = [q, k, v]
        out_block_rows = S
        out_step = lambda i: 0
    in_specs += [pl.BlockSpec(memory_space=pltpu.SMEM), _resident(bucket.shape), _resident(masks.shape)]
    args += [rel_bias_g, jnp.asarray(bucket), jnp.asarray(masks)]
    stat_spec = pl.BlockSpec((None, out_block_rows, LANES), lambda b, i: (b, out_step(i), 0))
    stat_shape = jax.ShapeDtypeStruct((B, S, LANES), F32)
    return pl.pallas_call(
        functools.partial(_attn_kernel, nr=nr, nsub=nsub, has_prev=has_prev, stride=dil, n_heads=H),
        grid=(B, n_steps),
        in_specs=in_specs,
        out_specs=[pl.BlockSpec((None, E // LANES, out_block_rows, LANES), lambda b, i: (b, 0, out_step(i), 0)),
                   stat_spec, stat_spec],
        out_shape=[jax.ShapeDtypeStruct((B, E // LANES, S, LANES), F32), stat_shape, stat_shape],
        scratch_shapes=[pltpu.VMEM((masks.shape[0], H * ATT_BLOCK, nk), F32),
                        pltpu.VMEM((H * ATT_BLOCK, nk), F32),
                        pltpu.VMEM((H * ATT_BLOCK, nk), BF16),
                        pltpu.VMEM((2, ATT_BLOCK, LANES), F32)],
        compiler_params=_cparams(("arbitrary", "arbitrary")),
        name=f"attention_d{dil}",
    )(*args)


def _attn_out_kernel(x_ref, shift_ref, scale_ref, gate_ref, o0_ref, o1_ref, o2_ref,
                     m0_ref, m1_ref, m2_ref, d0_ref, d1_ref, d2_ref, wz_ref, w_out_ref, expand_ref,
                     out_ref):
    tm, D = x_ref.shape
    E = wz_ref.shape[1]
    n_heads = E // HEAD_DIM
    x = x_ref[...]
    h = _mod_norm(x, scale_ref[...], shift_ref[...]).astype(BF16)
    z = _dot(h, wz_ref[...])
    m0, m1, m2 = m0_ref[...], m1_ref[...], m2_ref[...]
    m = jnp.maximum(jnp.maximum(m0, m1), m2)
    e0, e1, e2 = jnp.exp2(m0 - m), jnp.exp2(m1 - m), jnp.exp2(m2 - m)
    den = e0 * d0_ref[...] + e1 * d1_ref[...] + e2 * d2_ref[...]
    head_lane = lax.broadcasted_iota(jnp.int32, (tm, LANES), 1) < n_heads
    packed = None
    for g, e in enumerate((e0, e1, e2)):
        w = jnp.where(head_lane, e / den, 0.0)
        hi = w.astype(BF16).astype(F32)
        for part, val in enumerate((hi, w - hi)):
            shift = (2 * g + part) * n_heads
            val = pltpu.roll(val, shift, axis=1) if shift else val
            packed = val if packed is None else packed + val
    wexp = _dot(packed.astype(BF16), expand_ref[...])
    o = None
    for g, o_ref in enumerate((o0_ref, o1_ref, o2_ref)):
        og = jnp.concatenate([o_ref[hp] for hp in range(E // LANES)], axis=-1)
        term = wexp[:, g * E:(g + 1) * E] * og
        o = term if o is None else o + term
    y = (o * _silu(z)).astype(BF16)
    out_ref[...] = x + gate_ref[...] * _dot(y, w_out_ref[...])


def _attn_out(x, mod, os_, ms, dens, w_in, w_out, tm=512):
    B, S, D = x.shape
    E = w_out.shape[0]
    z_block = w_in.shape[1] // E - 1
    tile = lambda w: pl.BlockSpec((None, tm, w), lambda b, i: (b, i, 0))
    mod_specs = [pl.BlockSpec((None, 1, D), functools.partial(lambda b, i, k: (b, 0, k), k=k))
                 for k in range(3)]
    n_groups, H = len(os_), E // HEAD_DIM
    expand = np.zeros((LANES, n_groups * E), np.float32)
    for g in range(n_groups):
        for part in range(2):
            for hd in range(H):
                expand[(2 * g + part) * H + hd, g * E + hd * HEAD_DIM:g * E + (hd + 1) * HEAD_DIM] = 1.0
    expand = jnp.asarray(expand, BF16)
    return pl.pallas_call(
        _attn_out_kernel,
        grid=(B, S // tm),
        in_specs=[tile(D)] + mod_specs
                 + [pl.BlockSpec((None, E // LANES, tm, LANES), lambda b, i: (b, 0, i, 0))] * 3
                 + [tile(LANES)] * 6
                 + [pl.BlockSpec((D, E), lambda b, i: (0, z_block), pipeline_mode=pl.Buffered(1)),
                    _resident(w_out.shape), _resident(expand.shape)],
        out_specs=tile(D),
        out_shape=jax.ShapeDtypeStruct((B, S, D), F32),
        compiler_params=_cparams(("arbitrary", "arbitrary")),
        name="attention_out",
    )(x, mod, mod, mod, *os_, *ms, *dens, w_in, w_out, expand)


def _s5_kernel(x_ref, shift_ref, scale_ref, gate_ref, w_in_ref, bbr_ref, bbi_ref, ar_ref, ai_ref,
               cr_ref, ci_ref, dskip_ref, w_glu_ref, b_glu_ref, w_out_ref,
               o_ref, state_ref, sr_ref, si_ref, *, scan_lanes):
    B, ts, D = x_ref.shape
    R = ts * B
    n_chunks, cn, ck = bbr_ref.shape
    E = n_chunks * ck
    dot_t = lambda a, b: lax.dot_general(a, b, (((1,), (1,)), ((), ())), preferred_element_type=F32)

    @pl.when(pl.program_id(0) == 0)
    def _():
        state_ref[...] = jnp.zeros_like(state_ref)

    x3 = jnp.swapaxes(x_ref[...], 0, 1)
    ms = jnp.mean(x3 * x3, axis=-1, keepdims=True)
    h3 = (x3 * lax.rsqrt(ms + EPS)) * (1.0 + scale_ref[...][None]) + shift_ref[...][None]
    h = h3.reshape(R, D).astype(BF16)
    uz = _dot(h, w_in_ref[...])
    u, z = uz[:, :E], uz[:, E:]
    u16 = u.astype(BF16)

    def in_proj(c):
        uc = u16[:, c * ck:(c + 1) * ck]
        sr_ref[c] = dot_t(uc, bbr_ref[c])
        si_ref[c] = dot_t(uc, bbi_ref[c])

    def scan(c):
        for l0 in range(0, cn, scan_lanes):
            ls = slice(l0, l0 + scan_lanes)
            gl = slice(c * cn + l0, c * cn + l0 + scan_lanes)
            ar, ai = ar_ref[:, gl], ai_ref[:, gl]
            pr, pi = state_ref[0, :, gl], state_ref[1, :, gl]
            for t in range(ts):
                rows = slice(t * B, (t + 1) * B)
                nr = ar * pr - ai * pi + sr_ref[c, rows, ls]
                ni = ar * pi + ai * pr + si_ref[c, rows, ls]
                sr_ref[c, rows, ls] = nr
                si_ref[c, rows, ls] = ni
                pr, pi = nr, ni
            state_ref[0, :, gl] = pr
            state_ref[1, :, gl] = pi

    def out_proj(c):
        cols = slice(c * ck, (c + 1) * ck)
        y = (dot_t(sr_ref[c].astype(BF16), cr_ref[c]) - dot_t(si_ref[c].astype(BF16), ci_ref[c])
             + dskip_ref[:, cols] * u[:, cols])
        return y * (0.5 * (1.0 + jnp.tanh(math.sqrt(2.0 / math.pi) * (y + 0.044715 * (y * y * y)))))

    g_parts = []
    in_proj(0)
    for c in range(n_chunks):
        if c + 1 < n_chunks:
            in_proj(c + 1)
        scan(c)
        if c >= 1:
            g_parts.append(out_proj(c - 1))
    g_parts.append(out_proj(n_chunks - 1))
    g = jnp.concatenate(g_parts, axis=-1)
    y = g * _sigmoid(_dot(g.astype(BF16), w_glu_ref[...]) + b_glu_ref[...])
    y = (y * _silu(z)).astype(BF16)
    out = _dot(y, w_out_ref[...]).reshape(ts, B, D)
    o_ref[...] = jnp.swapaxes(x3 + gate_ref[...][None] * out, 0, 1)


def _s5_layer(x, mod, w_in, bbr, bbi, ar, ai, cr, ci, d_skip, w_glu, b_glu, w_out, ts=64, scan_lanes=512):
    B, S, D = x.shape
    E = w_glu.shape[0]
    N = ar.shape[1]
    mod_specs = [pl.BlockSpec((B, D), functools.partial(lambda i, k: (0, k), k=k)) for k in range(3)]
    weights = [w_in, bbr, bbi, ar, ai, cr, ci, d_skip.reshape(1, E), w_glu, b_glu.reshape(1, E), w_out]
    return pl.pallas_call(
        functools.partial(_s5_kernel, scan_lanes=scan_lanes),
        grid=(S // ts,),
        in_specs=[pl.BlockSpec((B, ts, D), lambda i: (0, i, 0))] + mod_specs
                 + [_resident(w.shape) for w in weights],
        out_specs=pl.BlockSpec((B, ts, D), lambda i: (0, i, 0)),
        out_shape=jax.ShapeDtypeStruct((B, S, D), F32),
        scratch_shapes=[pltpu.VMEM((2, B, N), F32),
                        pltpu.VMEM((bbr.shape[0], ts * B, bbr.shape[1]), F32),
                        pltpu.VMEM((bbr.shape[0], ts * B, bbr.shape[1]), F32)],
        compiler_params=_cparams(("arbitrary",)),
        name="s5_layer",
    )(x, mod, mod, mod, *weights)


def _s5_params(log_dt, lambda_re, lambda_im, b_re, b_im, c_re, c_im, B, groups_per_chunk=16):
    G, P = lambda_re.shape
    K = b_re.shape[2]
    dt = jnp.exp(log_dt.astype(F32))[:, None]
    lr = lambda_re.astype(F32)
    li = lambda_im.astype(F32)
    mag = jnp.exp(lr * dt)
    ar = mag * jnp.cos(li * dt)
    ai = mag * jnp.sin(li * dt)
    inv = 1.0 / (lr * lr + li * li)
    qr = ((ar - 1.0) * lr + ai * li) * inv
    qi = (ai * lr - (ar - 1.0) * li) * inv
    br = b_re.astype(F32)
    bi = b_im.astype(F32)
    bbr = qr[..., None] * br - qi[..., None] * bi
    bbi = qr[..., None] * bi + qi[..., None] * br
    gc = groups_per_chunk
    nc = G // gc

    def block_diag(m):
        a, b = m.shape[1:]
        rows = m.astype(BF16).reshape(nc, gc * a, b)
        on_diag = np.kron(np.eye(gc, dtype=bool), np.ones((a, b), dtype=bool))
        return jnp.where(on_diag, jnp.tile(rows, (1, 1, gc)), jnp.zeros((), BF16))

    bcast = lambda a: jnp.broadcast_to(a.reshape(1, G * P), (B, G * P))
    return (block_diag(bbr), block_diag(bbi), bcast(ar), bcast(ai),
            block_diag(c_re.astype(F32)), block_diag(c_im.astype(F32)))


def kernel(x, c, rel_bias, final_g, l0_ada_w, l0_ada_b, l0_w_in, l0_conv_w, l0_conv_b, l0_w_out, l1_ada_w, l1_ada_b, l1_w_in, l1_w_out, l2_ada_w, l2_ada_b, l2_w_in, l2_log_dt, l2_lambda_re, l2_lambda_im, l2_b_re, l2_b_im, l2_c_re, l2_c_im, l2_d_skip, l2_w_glu, l2_b_glu, l2_w_out, l3_ada_w, l3_ada_b, l3_w_in, l3_conv_w, l3_conv_b, l3_w_out):
    B, S, D = x.shape
    assert B == SUBLANES, "the S5 scan keeps the batch on the sublane axis"
    mods = _modulation(c, (l0_ada_w, l1_ada_w, l2_ada_w, l3_ada_w), (l0_ada_b, l1_ada_b, l2_ada_b, l3_ada_b))
    mod3 = [m.reshape(B, 1, 3 * D) for m in mods]

    x, w1_in, w1_out = _conv_layer(x, mod3[0], l0_w_in.astype(BF16), l0_conv_w, l0_conv_b,
                                   l0_w_out.astype(BF16), cast=(l1_w_in, l1_w_out))

    dils = tuple(dil for _, dil in DILATION_PATTERNS)
    n_qkv = 3 * len(dils)
    E = l1_w_out.shape[0]
    H = E // HEAD_DIM
    res = _qkv_proj(x, mod3[1], w1_in, dils, E, cast=(l2_w_in, l2_w_glu, l2_w_out, l3_w_in, l3_w_out))
    qkv, (w2_in, w2_glu, w2_out, w3_in, w3_out) = res[:n_qkv], res[n_qkv:]
    os_, ms, dens = [], [], []
    for g, (window, dil) in enumerate(DILATION_PATTERNS):
        o, m, den = _attention(*qkv[3 * g:3 * g + 3], rel_bias[:, g * H:(g + 1) * H], window, dil)
        os_.append(o)
        ms.append(m)
        dens.append(den)
    x = _attn_out(x, mod3[1], os_, ms, dens, w1_in, w1_out)

    ssm = _s5_params(l2_log_dt, l2_lambda_re, l2_lambda_im, l2_b_re, l2_b_im, l2_c_re, l2_c_im, B)
    x = _s5_layer(x, mods[2], w2_in, *ssm, l2_d_skip, w2_glu, l2_b_glu, w2_out)

    return _conv_layer(x, mod3[3], w3_in, l3_conv_w, l3_conv_b, w3_out, final_g=final_g)
```

```python
import functools
import math

import numpy as np
import jax
import jax.numpy as jnp
from jax import lax
from jax.experimental import pallas as pl
from jax.experimental.pallas import tpu as pltpu

EPS = 1e-6
HEAD_DIM = 64
ATT_BLOCK = 128
DILATION_PATTERNS = ((128, 1), (512, 4), (2048, 16))
REL_BUCKETS = 32
REL_MAX_DIST = 2048
NEG_INF = -1e30
LOG2E = math.log2(math.e)
LANES = 128
SUBLANES = 8
VMEM_LIMIT_BYTES = 60 * 1024 * 1024

F32 = jnp.float32
BF16 = jnp.bfloat16


def _cparams(semantics):
    return pltpu.CompilerParams(dimension_semantics=semantics, vmem_limit_bytes=VMEM_LIMIT_BYTES)


def _resident(shape):
    nd = len(shape)
    return pl.BlockSpec(shape, lambda *_: (0,) * nd, pipeline_mode=pl.Buffered(1))


def _silu(z):
    return z * (1.0 / (1.0 + jnp.exp(-z)))


def _sigmoid(z):
    return 1.0 / (1.0 + jnp.exp(-z))


def _mod_norm(x, scale, shift):
    ms = jnp.mean(x * x, axis=-1, keepdims=True)
    return (x * lax.rsqrt(ms + EPS)) * (1.0 + scale) + shift


def _dot(a, b):
    return jnp.dot(a, b, preferred_element_type=F32)


def _rows(start, size, stride):
    return pl.ds(start, size) if stride == 1 else pl.ds(start, size, stride=stride)


def _modulation_kernel(c_ref, w0, w1, w2, w3, b0, b1, b2, b3, o0, o1, o2, o3):
    sc = _silu(c_ref[...]).astype(BF16)
    for w, b, o in ((w0, b0, o0), (w1, b1, o1), (w2, b2, o2), (w3, b3, o3)):
        o[...] = _dot(sc, w[...].astype(BF16)) + b[...]


def _modulation(c, ws, bs, tn=512):
    B, D = c.shape
    N = ws[0].shape[1]
    w_spec = pl.BlockSpec((D, tn), lambda j: (0, j))
    b_spec = pl.BlockSpec((1, tn), lambda j: (0, j))
    o_spec = pl.BlockSpec((B, tn), lambda j: (0, j))
    return pl.pallas_call(
        _modulation_kernel,
        grid=(N // tn,),
        in_specs=[pl.BlockSpec((B, D), lambda j: (0, 0))] + [w_spec] * 4 + [b_spec] * 4,
        out_specs=[o_spec] * 4,
        out_shape=[jax.ShapeDtypeStruct((B, N), F32)] * 4,
        compiler_params=_cparams(("arbitrary",)),
        name="modulation",
    )(c, *ws, *[b.reshape(1, N) for b in bs])


def _cast_specs(arrays, n_steps, step_of):
    in_specs, out_specs, out_shape = [], [], []
    for a in arrays:
        rows, cols = a.shape
        spec = pl.BlockSpec((rows // n_steps, cols), lambda *ids: (step_of(*ids), 0))
        in_specs.append(spec)
        out_specs.append(spec)
        out_shape.append(jax.ShapeDtypeStruct(a.shape, BF16))
    return in_specs, out_specs, out_shape


def _cast_slabs(src_refs, dst_refs):
    for src, dst in zip(src_refs, dst_refs):
        dst[...] = src[...].astype(BF16)


def _conv_layer_kernel(*refs, ec, final, n_cast):
    n_in = 9 if final else 8
    (x_ref, shift_ref, scale_ref, gate_ref, w_in_ref, cw_ref, cb_ref, w_out_ref) = refs[:8]
    g_ref = refs[8] if final else None
    cast_src = refs[n_in:n_in + n_cast]
    o_ref = refs[n_in + n_cast]
    cast_dst = refs[n_in + n_cast + 1:n_in + 2 * n_cast + 1]
    carry_ref, vs_ref = refs[n_in + 2 * n_cast + 1:]
    tm, D = x_ref.shape
    E = w_out_ref.shape[0]

    @pl.when(pl.program_id(1) == 0)
    def _():
        carry_ref[...] = jnp.zeros_like(carry_ref)

    _cast_slabs(cast_src, cast_dst)
    x = x_ref[...]
    h = _mod_norm(x, scale_ref[...], shift_ref[...]).astype(BF16)
    n_chunks = E // ec

    def in_proj(j):
        c0 = j * ec
        return tuple(_dot(h, w_in_ref[:, k * E + c0:k * E + c0 + ec]) for k in range(4))

    def gated_conv(j, proj):
        u, gc, gb, z = proj
        c0 = j * ec
        v = gc * u
        vs_ref[0:SUBLANES, :] = carry_ref[j]
        vs_ref[SUBLANES:SUBLANES + tm, :] = v
        carry_ref[j] = v[tm - SUBLANES:tm, :]
        v1 = vs_ref[SUBLANES - 1:SUBLANES - 1 + tm, :]
        v2 = vs_ref[SUBLANES - 2:SUBLANES - 2 + tm, :]
        cw = cw_ref[:, c0:c0 + ec]
        conv = cw[0:1, :] * v2 + cw[1:2, :] * v1 + cw[2:3, :] * v + cb_ref[:, c0:c0 + ec]
        return (gb * conv * _silu(z)).astype(BF16)

    acc = jnp.zeros((tm, D), F32)
    proj = in_proj(0)
    for j in range(n_chunks):
        nxt = in_proj(j + 1) if j + 1 < n_chunks else None
        y = gated_conv(j, proj)
        acc = acc + _dot(y, w_out_ref[j * ec:(j + 1) * ec, :])
        proj = nxt
    out = x + gate_ref[...] * acc
    if final:
        ms = jnp.mean(out * out, axis=-1, keepdims=True)
        out = out * lax.rsqrt(ms + EPS) * g_ref[...]
    o_ref[...] = out


def _conv_layer(x, mod, w_in, conv_w, conv_b, w_out, *, final_g=None, cast=(), tm=1024, ec=256):
    B, S, D = x.shape
    x_spec = pl.BlockSpec((None, tm, D), lambda b, i: (b, i, 0))
    E = w_out.shape[0]
    final = final_g is not None
    mod_specs = [pl.BlockSpec((None, 1, D), functools.partial(lambda b, i, k: (b, 0, k), k=k))
                 for k in range(3)]
    in_specs = [x_spec] + mod_specs + [_resident(w_in.shape), _resident(conv_w.shape),
                                       _resident((1, E)), _resident(w_out.shape)]
    args = [x, mod, mod, mod, w_in, conv_w, conv_b.reshape(1, E), w_out]
    if final:
        in_specs.append(_resident((1, D)))
        args.append(final_g.reshape(1, D))
    n_tiles = S // tm
    cast_in, cast_out, cast_shape = _cast_specs(cast, B * n_tiles, lambda b, i: b * n_tiles + i)
    res = pl.pallas_call(
        functools.partial(_conv_layer_kernel, ec=ec, final=final, n_cast=len(cast)),
        grid=(B, n_tiles),
        in_specs=in_specs + cast_in,
        out_specs=[pl.BlockSpec((None, tm, D), lambda b, i: (b, i, 0))] + cast_out,
        out_shape=[jax.ShapeDtypeStruct((B, S, D), F32)] + cast_shape,
        scratch_shapes=[pltpu.VMEM((E // ec, SUBLANES, ec), F32),
                        pltpu.VMEM((SUBLANES + tm, ec), F32)],
        compiler_params=_cparams(("arbitrary", "arbitrary")),
        name="conv_layer_final" if final else "conv_layer",
    )(*args, *cast)
    return res if cast else res[0]


def _t5_bucket(dist):
    max_exact = REL_BUCKETS // 2
    d = np.maximum(dist, 0)
    ratio = np.log(np.maximum(d, 1) / max_exact) / math.log(REL_MAX_DIST / max_exact)
    large = np.minimum(max_exact + (ratio * (REL_BUCKETS - max_exact)).astype(np.int64), REL_BUCKETS - 1)
    return np.where(d < max_exact, d, large).astype(np.int32)


def _band_structure(window, dil):
    wsub = window // dil
    qi = np.arange(ATT_BLOCK)[:, None]
    ki = np.arange(2 * ATT_BLOCK)[None, :]
    dist = ATT_BLOCK + qi - ki
    in_band = (dist >= 0) & (dist <= wsub)
    first = in_band & (ki >= ATT_BLOCK)
    return _t5_bucket(dist * dil), np.stack([in_band, first]).astype(np.float32)


def _qkv_kernel(x_ref, shift_ref, scale_ref, w_ref, *refs, dils, n_cast):
    n_out = 3 * len(dils)
    cast_src, out_refs = refs[:n_cast], refs[n_cast:n_cast + n_out]
    cast_dst, h_ref = refs[n_cast + n_out:-1], refs[-1]
    tn, D = x_ref.shape
    E = out_refs[0].shape[-1]
    _cast_slabs(cast_src, cast_dst)
    h = _mod_norm(x_ref[...], scale_ref[...], shift_ref[...])
    n_col = D // LANES
    for c in range(n_col):
        h_ref[c] = h[:, c * LANES:(c + 1) * LANES]
    for g, dil in enumerate(dils):
        n = tn // dil
        if dil == 1:
            hp = h
        else:
            hp = jnp.concatenate(
                [jnp.concatenate([h_ref[c, pl.ds(r, n, stride=dil), :] for c in range(n_col)], axis=1)
                 for r in range(dil)], axis=0)
        hp = hp.astype(BF16)
        for j in range(3):
            c0 = (3 * g + j) * E
            res = _dot(hp, w_ref[:, c0:c0 + E])
            if j == 0:
                res = res * (LOG2E / math.sqrt(HEAD_DIM))
            out_refs[3 * g + j][...] = res.astype(BF16).reshape(dil, n, E)


def _qkv_proj(x, mod, w_qkv, dils, E, cast=(), tn=512):
    B, S, D = x.shape
    n_tiles = S // tn
    cast_in, cast_out, cast_shape = _cast_specs(cast, B * n_tiles, lambda b, i: b * n_tiles + i)
    mod_specs = [pl.BlockSpec((None, 1, D), functools.partial(lambda b, i, k: (b, 0, k), k=k))
                 for k in range(2)]
    out_specs, out_shape = [], []
    for dil in dils:
        out_specs += [pl.BlockSpec((None, dil, tn // dil, E), lambda b, i: (b, 0, i, 0))] * 3
        out_shape += [jax.ShapeDtypeStruct((B, dil, S // dil, E), BF16)] * 3
    return pl.pallas_call(
        functools.partial(_qkv_kernel, dils=dils, n_cast=len(cast)),
        grid=(B, n_tiles),
        in_specs=[pl.BlockSpec((None, tn, D), lambda b, i: (b, i, 0))] + mod_specs
                 + [_resident(w_qkv.shape)] + cast_in,
        out_specs=out_specs + cast_out,
        out_shape=out_shape + cast_shape,
        scratch_shapes=[pltpu.VMEM((D // LANES, tn, LANES), F32)],
        compiler_params=_cparams(("arbitrary", "arbitrary")),
        name="qkv_proj",
    )(x, mod, mod, w_qkv, *cast)


def _attn_kernel(*refs, nr, nsub, has_prev, stride, n_heads):
    if has_prev:
        (q_ref, k_ref, v_ref, kp_ref, vp_ref, rb_ref, bucket_ref, mask_ref,
         o_ref, m_ref, den_ref, tbl_ref, s_ref, p_ref, stat_ref) = refs
    else:
        (q_ref, k_ref, v_ref, rb_ref, bucket_ref, mask_ref,
         o_ref, m_ref, den_ref, tbl_ref, s_ref, p_ref, stat_ref) = refs
    n_pairs = n_heads // 2
    pair_rows = 2 * ATT_BLOCK
    nk = s_ref.shape[1]
    step = pl.program_id(1)

    @pl.when((pl.program_id(0) == 0) & (step == 0))
    def _():
        def fill(b, carry):
            hit = bucket_ref[...] == b
            for h in range(n_heads):
                rows = slice(h * ATT_BLOCK, (h + 1) * ATT_BLOCK)
                s_ref[rows, :] = jnp.where(hit, rb_ref[b, h] * LOG2E, s_ref[rows, :])
            return carry
        s_ref[...] = jnp.zeros_like(s_ref)
        lax.fori_loop(0, REL_BUCKETS, fill, 0)
        for t in range(tbl_ref.shape[0]):
            for h in range(n_heads):
                rows = slice(h * ATT_BLOCK, (h + 1) * ATT_BLOCK)
                tbl_ref[t, rows, :] = jnp.where(mask_ref[t] > 0.0, s_ref[rows, :], NEG_INF)
        stat_ref[0] = jnp.zeros((ATT_BLOCK, LANES), F32)
        stat_ref[1] = jnp.ones((ATT_BLOCK, LANES), F32)

    lane = lax.broadcasted_iota(jnp.int32, (ATT_BLOCK, LANES), 1)
    lo = lane < HEAD_DIM

    def block(q_of, k_of, v_of, table, out_rows):
        for hp in range(n_pairs):
            cols = slice(hp * LANES, (hp + 1) * LANES)
            rows = slice(hp * pair_rows, (hp + 1) * pair_rows)
            q2 = q_of(cols)
            zero = jnp.zeros_like(q2)
            q_pair = jnp.concatenate([jnp.where(lo, q2, zero), jnp.where(lo, zero, q2)], axis=0)
            s_ref[rows, :] = lax.dot_general(q_pair, k_of(cols), (((1,), (1,)), ((), ())),
                                             preferred_element_type=F32)
        head_rows = [slice(h * ATT_BLOCK, (h + 1) * ATT_BLOCK) for h in range(n_heads)]
        if not has_prev:
            maxima = []
            for rows in head_rows:
                s = s_ref[rows, :] + tbl_ref[table, rows, :]
                s_ref[rows, :] = s
                maxima.append(jnp.max(s, axis=-1, keepdims=True))
        for h, rows in enumerate(head_rows):
            if has_prev:
                s = s_ref[rows, :] + tbl_ref[table, rows, :]
                m = jnp.max(s, axis=-1, keepdims=True)
            else:
                s, m = s_ref[rows, :], maxima[h]
            p = jnp.exp2(s - m)
            den = jnp.sum(p, axis=-1, keepdims=True)
            p_ref[rows, :] = p.astype(BF16)
            stat_ref[0, :, h:h + 1] = m
            stat_ref[1, :, h:h + 1] = den
        m_ref[out_rows, :] = stat_ref[0]
        den_ref[out_rows, :] = stat_ref[1]
        for hp in range(n_pairs):
            cols = slice(hp * LANES, (hp + 1) * LANES)
            rows = slice(hp * pair_rows, (hp + 1) * pair_rows)
            res = _dot(p_ref[rows, :], v_of(cols))
            o_ref[hp, out_rows, :] = jnp.where(lo, res[:ATT_BLOCK], res[ATT_BLOCK:])

    for r in range(nr):
        for j in range(nsub):
            row = j * ATT_BLOCK
            q_of = lambda cols, r=r, row=row: q_ref[r, row:row + ATT_BLOCK, cols]
            if has_prev:
                out_rows = _rows(row * stride + r, ATT_BLOCK, stride)
                if j == 0:
                    k_of = lambda cols, r=r: jnp.concatenate([kp_ref[r, :, cols], k_ref[r, 0:ATT_BLOCK, cols]], axis=0)
                    v_of = lambda cols, r=r: jnp.concatenate([vp_ref[r, :, cols], v_ref[r, 0:ATT_BLOCK, cols]], axis=0)
                    table = jnp.where(step == 0, 1, 0)
                else:
                    k_of = lambda cols, r=r, row=row: k_ref[r, row - ATT_BLOCK:row + ATT_BLOCK, cols]
                    v_of = lambda cols, r=r, row=row: v_ref[r, row - ATT_BLOCK:row + ATT_BLOCK, cols]
                    table = 0
            else:
                out_rows = _rows(step * nr + r, ATT_BLOCK, stride)
                k_of = lambda cols, r=r: k_ref[r, :, cols]
                v_of = lambda cols, r=r: v_ref[r, :, cols]
                table = 0
            block(q_of, k_of, v_of, table, out_rows)


def _attention(q, k, v, rel_bias_g, window, dil, tokens_per_step=1024):
    B, _, L, E = q.shape
    S = L * dil
    H = E // HEAD_DIM
    bucket, masks = _band_structure(window, dil)
    has_prev = L > ATT_BLOCK
    if has_prev:
        nr = dil
        nsub = tokens_per_step // (dil * ATT_BLOCK)
        tq = nsub * ATT_BLOCK
        n_steps = L // tq
        nk = 2 * ATT_BLOCK
        qkv_spec = pl.BlockSpec((None, nr, tq, E), lambda b, i: (b, 0, i, 0))
        prev_spec = pl.BlockSpec((None, nr, ATT_BLOCK, E),
                                 lambda b, i: (b, 0, jnp.maximum(i * nsub - 1, 0), 0))
        in_specs = [qkv_spec] * 3 + [prev_spec] * 2
        args = [q, k, v, k, v]
        out_block_rows = tokens_per_step
        out_step = lambda i: i
    else:
        nr = tokens_per_step // ATT_BLOCK
        nsub = 1
        n_steps = dil // nr
        nk = ATT_BLOCK
        bucket, masks = bucket[:, ATT_BLOCK:], masks[1:, :, ATT_BLOCK:]
        qkv_spec = pl.BlockSpec((None, nr, ATT_BLOCK, E), lambda b, i: (b, i, 0, 0))
        in_specs = [qkv_spec] * 3
        args = [q, k, v]
        out_block_rows = S
        out_step = lambda i: 0
    in_specs += [pl.BlockSpec(memory_space=pltpu.SMEM), _resident(bucket.shape), _resident(masks.shape)]
    args += [rel_bias_g, jnp.asarray(bucket), jnp.asarray(masks)]
    stat_spec = pl.BlockSpec((None, out_block_rows, LANES), lambda b, i: (b, out_step(i), 0))
    stat_shape = jax.ShapeDtypeStruct((B, S, LANES), F32)
    return pl.pallas_call(
        functools.partial(_attn_kernel, nr=nr, nsub=nsub, has_prev=has_prev, stride=dil, n_heads=H),
        grid=(B, n_steps),
        in_specs=in_specs,
        out_specs=[pl.BlockSpec((None, E // LANES, out_block_rows, LANES), lambda b, i: (b, 0, out_step(i), 0)),
                   stat_spec, stat_spec],
        out_shape=[jax.ShapeDtypeStruct((B, E // LANES, S, LANES), F32), stat_shape, stat_shape],
        scratch_shapes=[pltpu.VMEM((masks.shape[0], H * ATT_BLOCK, nk), F32),
                        pltpu.VMEM((H * ATT_BLOCK, nk), F32),
                        pltpu.VMEM((H * ATT_BLOCK, nk), BF16),
                        pltpu.VMEM((2, ATT_BLOCK, LANES), F32)],
        compiler_params=_cparams(("arbitrary", "arbitrary")),
        name=f"attention_d{dil}",
    )(*args)


def _attn_out_kernel(x_ref, shift_ref, scale_ref, gate_ref, o0_ref, o1_ref, o2_ref,
                     m0_ref, m1_ref, m2_ref, d0_ref, d1_ref, d2_ref, wz_ref, w_out_ref, expand_ref,
                     out_ref):
    tm, D = x_ref.shape
    E = wz_ref.shape[1]
    n_heads = E // HEAD_DIM
    x = x_ref[...]
    h = _mod_norm(x, scale_ref[...], shift_ref[...]).astype(BF16)
    z = _dot(h, wz_ref[...])
    m0, m1, m2 = m0_ref[...], m1_ref[...], m2_ref[...]
    m = jnp.maximum(jnp.maximum(m0, m1), m2)
    e0, e1, e2 = jnp.exp2(m0 - m), jnp.exp2(m1 - m), jnp.exp2(m2 - m)
    den = e0 * d0_ref[...] + e1 * d1_ref[...] + e2 * d2_ref[...]
    head_lane = lax.broadcasted_iota(jnp.int32, (tm, LANES), 1) < n_heads
    packed = None
    for g, e in enumerate((e0, e1, e2)):
        w = jnp.where(head_lane, e / den, 0.0)
        hi = w.astype(BF16).astype(F32)
        for part, val in enumerate((hi, w - hi)):
            shift = (2 * g + part) * n_heads
            val = pltpu.roll(val, shift, axis=1) if shift else val
            packed = val if packed is None else packed + val
    wexp = _dot(packed.astype(BF16), expand_ref[...])
    o = None
    for g, o_ref in enumerate((o0_ref, o1_ref, o2_ref)):
        og = jnp.concatenate([o_ref[hp] for hp in range(E // LANES)], axis=-1)
        term = wexp[:, g * E:(g + 1) * E] * og
        o = term if o is None else o + term
    y = (o * _silu(z)).astype(BF16)
    out_ref[...] = x + gate_ref[...] * _dot(y, w_out_ref[...])


def _attn_out(x, mod, os_, ms, dens, w_in, w_out, tm=512):
    B, S, D = x.shape
    E = w_out.shape[0]
    z_block = w_in.shape[1] // E - 1
    tile = lambda w: pl.BlockSpec((None, tm, w), lambda b, i: (b, i, 0))
    mod_specs = [pl.BlockSpec((None, 1, D), functools.partial(lambda b, i, k: (b, 0, k), k=k))
                 for k in range(3)]
    n_groups, H = len(os_), E // HEAD_DIM
    expand = np.zeros((LANES, n_groups * E), np.float32)
    for g in range(n_groups):
        for part in range(2):
            for hd in range(H):
                expand[(2 * g + part) * H + hd, g * E + hd * HEAD_DIM:g * E + (hd + 1) * HEAD_DIM] = 1.0
    expand = jnp.asarray(expand, BF16)
    return pl.pallas_call(
        _attn_out_kernel,
        grid=(B, S // tm),
        in_specs=[tile(D)] + mod_specs
                 + [pl.BlockSpec((None, E // LANES, tm, LANES), lambda b, i: (b, 0, i, 0))] * 3
                 + [tile(LANES)] * 6
                 + [pl.BlockSpec((D, E), lambda b, i: (0, z_block), pipeline_mode=pl.Buffered(1)),
                    _resident(w_out.shape), _resident(expand.shape)],
        out_specs=tile(D),
        out_shape=jax.ShapeDtypeStruct((B, S, D), F32),
        compiler_params=_cparams(("arbitrary", "arbitrary")),
        name="attention_out",
    )(x, mod, mod, mod, *os_, *ms, *dens, w_in, w_out, expand)


def _s5_kernel(x_ref, shift_ref, scale_ref, gate_ref, w_in_ref, bbr_ref, bbi_ref, ar_ref, ai_ref,
               cr_ref, ci_ref, dskip_ref, w_glu_ref, b_glu_ref, w_out_ref,
               o_ref, state_ref, sr_ref, si_ref, *, scan_lanes):
    B, ts, D = x_ref.shape
    R = ts * B
    n_chunks, cn, ck = bbr_ref.shape
    E = n_chunks * ck
    dot_t = lambda a, b: lax.dot_general(a, b, (((1,), (1,)), ((), ())), preferred_element_type=F32)

    @pl.when(pl.program_id(0) == 0)
    def _():
        state_ref[...] = jnp.zeros_like(state_ref)

    x3 = jnp.swapaxes(x_ref[...], 0, 1)
    ms = jnp.mean(x3 * x3, axis=-1, keepdims=True)
    h3 = (x3 * lax.rsqrt(ms + EPS)) * (1.0 + scale_ref[...][None]) + shift_ref[...][None]
    h = h3.reshape(R, D).astype(BF16)
    uz = _dot(h, w_in_ref[...])
    u, z = uz[:, :E], uz[:, E:]
    u16 = u.astype(BF16)

    def in_proj(c):
        uc = u16[:, c * ck:(c + 1) * ck]
        sr_ref[c] = dot_t(uc, bbr_ref[c])
        si_ref[c] = dot_t(uc, bbi_ref[c])

    def scan(c):
        for l0 in range(0, cn, scan_lanes):
            ls = slice(l0, l0 + scan_lanes)
            gl = slice(c * cn + l0, c * cn + l0 + scan_lanes)
            ar, ai = ar_ref[:, gl], ai_ref[:, gl]
            pr, pi = state_ref[0, :, gl], state_ref[1, :, gl]
            for t in range(ts):
                rows = slice(t * B, (t + 1) * B)
                nr = ar * pr - ai * pi + sr_ref[c, rows, ls]
                ni = ar * pi + ai * pr + si_ref[c, rows, ls]
                sr_ref[c, rows, ls] = nr
                si_ref[c, rows, ls] = ni
                pr, pi = nr, ni
            state_ref[0, :, gl] = pr
            state_ref[1, :, gl] = pi

    def out_proj(c):
        cols = slice(c * ck, (c + 1) * ck)
        y = (dot_t(sr_ref[c].astype(BF16), cr_ref[c]) - dot_t(si_ref[c].astype(BF16), ci_ref[c])
             + dskip_ref[:, cols] * u[:, cols])
        return y * (0.5 * (1.0 + jnp.tanh(math.sqrt(2.0 / math.pi) * (y + 0.044715 * (y * y * y)))))

    g_parts = []
    in_proj(0)
    for c in range(n_chunks):
        if c + 1 < n_chunks:
            in_proj(c + 1)
        scan(c)
        if c >= 1:
            g_parts.append(out_proj(c - 1))
    g_parts.append(out_proj(n_chunks - 1))
    g = jnp.concatenate(g_parts, axis=-1)
    y = g * _sigmoid(_dot(g.astype(BF16), w_glu_ref[...]) + b_glu_ref[...])
    y = (y * _silu(z)).astype(BF16)
    out = _dot(y, w_out_ref[...]).reshape(ts, B, D)
    o_ref[...] = jnp.swapaxes(x3 + gate_ref[...][None] * out, 0, 1)


def _s5_layer(x, mod, w_in, bbr, bbi, ar, ai, cr, ci, d_skip, w_glu, b_glu, w_out, ts=64, scan_lanes=512):
    B, S, D = x.shape
    E = w_glu.shape[0]
    N = ar.shape[1]
    mod_specs = [pl.BlockSpec((B, D), functools.partial(lambda i, k: (0, k), k=k)) for k in range(3)]
    weights = [w_in, bbr, bbi, ar, ai, cr, ci, d_skip.reshape(1, E), w_glu, b_glu.reshape(1, E), w_out]
    return pl.pallas_call(
        functools.partial(_s5_kernel, scan_lanes=scan_lanes),
        grid=(S // ts,),
        in_specs=[pl.BlockSpec((B, ts, D), lambda i: (0, i, 0))] + mod_specs
                 + [_resident(w.shape) for w in weights],
        out_specs=pl.BlockSpec((B, ts, D), lambda i: (0, i, 0)),
        out_shape=jax.ShapeDtypeStruct((B, S, D), F32),
        scratch_shapes=[pltpu.VMEM((2, B, N), F32),
                        pltpu.VMEM((bbr.shape[0], ts * B, bbr.shape[1]), F32),
                        pltpu.VMEM((bbr.shape[0], ts * B, bbr.shape[1]), F32)],
        compiler_params=_cparams(("arbitrary",)),
        name="s5_layer",
    )(x, mod, mod, mod, *weights)


def _s5_params(log_dt, lambda_re, lambda_im, b_re, b_im, c_re, c_im, B, groups_per_chunk=16):
    G, P = lambda_re.shape
    K = b_re.shape[2]
    dt = jnp.exp(log_dt.astype(F32))[:, None]
    lr = lambda_re.astype(F32)
    li = lambda_im.astype(F32)
    mag = jnp.exp(lr * dt)
    ar = mag * jnp.cos(li * dt)
    ai = mag * jnp.sin(li * dt)
    inv = 1.0 / (lr * lr + li * li)
    qr = ((ar - 1.0) * lr + ai * li) * inv
    qi = (ai * lr - (ar - 1.0) * li) * inv
    br = b_re.astype(F32)
    bi = b_im.astype(F32)
    bbr = qr[..., None] * br - qi[..., None] * bi
    bbi = qr[..., None] * bi + qi[..., None] * br
    gc = groups_per_chunk
    nc = G // gc

    def block_diag(m):
        a, b = m.shape[1:]
        rows = m.astype(BF16).reshape(nc, gc * a, b)
        on_diag = np.kron(np.eye(gc, dtype=bool), np.ones((a, b), dtype=bool))
        return jnp.where(on_diag, jnp.tile(rows, (1, 1, gc)), jnp.zeros((), BF16))

    bcast = lambda a: jnp.broadcast_to(a.reshape(1, G * P), (B, G * P))
    return (block_diag(bbr), block_diag(bbi), bcast(ar), bcast(ai),
            block_diag(c_re.astype(F32)), block_diag(c_im.astype(F32)))


def kernel(x, c, rel_bias, final_g, l0_ada_w, l0_ada_b, l0_w_in, l0_conv_w, l0_conv_b, l0_w_out, l1_ada_w, l1_ada_b, l1_w_in, l1_w_out, l2_ada_w, l2_ada_b, l2_w_in, l2_log_dt, l2_lambda_re, l2_lambda_im, l2_b_re, l2_b_im, l2_c_re, l2_c_im, l2_d_skip, l2_w_glu, l2_b_glu, l2_w_out, l3_ada_w, l3_ada_b, l3_w_in, l3_conv_w, l3_conv_b, l3_w_out):
    B, S, D = x.shape
    assert B == SUBLANES, "the S5 scan keeps the batch on the sublane axis"
    mods = _modulation(c, (l0_ada_w, l1_ada_w, l2_ada_w, l3_ada_w), (l0_ada_b, l1_ada_b, l2_ada_b, l3_ada_b))
    mod3 = [m.reshape(B, 1, 3 * D) for m in mods]

    x, w1_in, w1_out = _conv_layer(x, mod3[0], l0_w_in.astype(BF16), l0_conv_w, l0_conv_b,
                                   l0_w_out.astype(BF16), cast=(l1_w_in, l1_w_out))

    dils = tuple(dil for _, dil in DILATION_PATTERNS)
    n_qkv = 3 * len(dils)
    E = l1_w_out.shape[0]
    H = E // HEAD_DIM
    res = _qkv_proj(x, mod3[1], w1_in, dils, E, cast=(l2_w_in, l2_w_glu, l2_w_out, l3_w_in, l3_w_out))
    qkv, (w2_in, w2_glu, w2_out, w3_in, w3_out) = res[:n_qkv], res[n_qkv:]
    os_, ms, dens = [], [], []
    for g, (window, dil) in enumerate(DILATION_PATTERNS):
        o, m, den = _attention(*qkv[3 * g:3 * g + 3], rel_bias[:, g * H:(g + 1) * H], window, dil)
        os_.append(o)
        ms.append(m)
        dens.append(den)
    x = _attn_out(x, mod3[1], os_, ms, dens, w1_in, w1_out)

    ssm = _s5_params(l2_log_dt, l2_lambda_re, l2_lambda_im, l2_b_re, l2_b_im, l2_c_re, l2_c_im, B)
    x = _s5_layer(x, mods[2], w2_in, *ssm, l2_d_skip, w2_glu, l2_b_glu, w2_out)

    return _conv_layer(x, mod3[3], w3_in, l3_conv_w, l3_conv_b, w3_out, final_g=final_g)
```

```python
import functools
import math

import numpy as np
import jax
import jax.numpy as jnp
from jax import lax
from jax.experimental import pallas as pl
from jax.experimental.pallas import tpu as pltpu

EPS = 1e-6
HEAD_DIM = 64
ATT_BLOCK = 128
DILATION_PATTERNS = ((128, 1), (512, 4), (2048, 16))
REL_BUCKETS = 32
REL_MAX_DIST = 2048
NEG_INF = -1e30
LOG2E = math.log2(math.e)
LANES = 128
SUBLANES = 8
VMEM_LIMIT_BYTES = 60 * 1024 * 1024

F32 = jnp.float32
BF16 = jnp.bfloat16


def _cparams(semantics):
    return pltpu.CompilerParams(dimension_semantics=semantics, vmem_limit_bytes=VMEM_LIMIT_BYTES)


def _resident(shape):
    nd = len(shape)
    return pl.BlockSpec(shape, lambda *_: (0,) * nd, pipeline_mode=pl.Buffered(1))


def _silu(z):
    return z * (1.0 / (1.0 + jnp.exp(-z)))


def _sigmoid(z):
    return 1.0 / (1.0 + jnp.exp(-z))


def _mod_norm(x, scale, shift):
    ms = jnp.mean(x * x, axis=-1, keepdims=True)
    return (x * lax.rsqrt(ms + EPS)) * (1.0 + scale) + shift


def _dot(a, b):
    return jnp.dot(a, b, preferred_element_type=F32)


def _rows(start, size, stride):
    return pl.ds(start, size) if stride == 1 else pl.ds(start, size, stride=stride)


def _modulation_kernel(c_ref, w0, w1, w2, w3, b0, b1, b2, b3, o0, o1, o2, o3):
    sc = _silu(c_ref[...]).astype(BF16)
    for w, b, o in ((w0, b0, o0), (w1, b1, o1), (w2, b2, o2), (w3, b3, o3)):
        o[...] = _dot(sc, w[...].astype(BF16)) + b[...]


def _modulation(c, ws, bs, tn=512):
    B, D = c.shape
    N = ws[0].shape[1]
    w_spec = pl.BlockSpec((D, tn), lambda j: (0, j))
    b_spec = pl.BlockSpec((1, tn), lambda j: (0, j))
    o_spec = pl.BlockSpec((B, tn), lambda j: (0, j))
    return pl.pallas_call(
        _modulation_kernel,
        grid=(N // tn,),
        in_specs=[pl.BlockSpec((B, D), lambda j: (0, 0))] + [w_spec] * 4 + [b_spec] * 4,
        out_specs=[o_spec] * 4,
        out_shape=[jax.ShapeDtypeStruct((B, N), F32)] * 4,
        compiler_params=_cparams(("arbitrary",)),
        name="modulation",
    )(c, *ws, *[b.reshape(1, N) for b in bs])


def _cast_specs(arrays, n_steps, step_of):
    in_specs, out_specs, out_shape = [], [], []
    for a in arrays:
        rows, cols = a.shape
        spec = pl.BlockSpec((rows // n_steps, cols), lambda *ids: (step_of(*ids), 0))
        in_specs.append(spec)
        out_specs.append(spec)
        out_shape.append(jax.ShapeDtypeStruct(a.shape, BF16))
    return in_specs, out_specs, out_shape


def _cast_slabs(src_refs, dst_refs):
    for src, dst in zip(src_refs, dst_refs):
        dst[...] = src[...].astype(BF16)


def _conv_layer_kernel(*refs, ec, final, n_cast):
    n_in = 9 if final else 8
    (x_ref, shift_ref, scale_ref, gate_ref, w_in_ref, cw_ref, cb_ref, w_out_ref) = refs[:8]
    g_ref = refs[8] if final else None
    cast_src = refs[n_in:n_in + n_cast]
    o_ref = refs[n_in + n_cast]
    cast_dst = refs[n_in + n_cast + 1:n_in + 2 * n_cast + 1]
    carry_ref, vs_ref = refs[n_in + 2 * n_cast + 1:]
    tm, D = x_ref.shape
    E = w_out_ref.shape[0]

    @pl.when(pl.program_id(1) == 0)
    def _():
        carry_ref[...] = jnp.zeros_like(carry_ref)

    _cast_slabs(cast_src, cast_dst)
    x = x_ref[...]
    h = _mod_norm(x, scale_ref[...], shift_ref[...]).astype(BF16)
    n_chunks = E // ec

    def in_proj(j):
        c0 = j * ec
        return tuple(_dot(h, w_in_ref[:, k * E + c0:k * E + c0 + ec]) for k in range(4))

    def gated_conv(j, proj):
        u, gc, gb, z = proj
        c0 = j * ec
        v = gc * u
        vs_ref[0:SUBLANES, :] = carry_ref[j]
        vs_ref[SUBLANES:SUBLANES + tm, :] = v
        carry_ref[j] = v[tm - SUBLANES:tm, :]
        v1 = vs_ref[SUBLANES - 1:SUBLANES - 1 + tm, :]
        v2 = vs_ref[SUBLANES - 2:SUBLANES - 2 + tm, :]
        cw = cw_ref[:, c0:c0 + ec]
        conv = cw[0:1, :] * v2 + cw[1:2, :] * v1 + cw[2:3, :] * v + cb_ref[:, c0:c0 + ec]
        return (gb * conv * _silu(z)).astype(BF16)

    acc = jnp.zeros((tm, D), F32)
    proj = in_proj(0)
    for j in range(n_chunks):
        nxt = in_proj(j + 1) if j + 1 < n_chunks else None
        y = gated_conv(j, proj)
        acc = acc + _dot(y, w_out_ref[j * ec:(j + 1) * ec, :])
        proj = nxt
    out = x + gate_ref[...] * acc
    if final:
        ms = jnp.mean(out * out, axis=-1, keepdims=True)
        out = out * lax.rsqrt(ms + EPS) * g_ref[...]
    o_ref[...] = out


def _conv_layer(x, mod, w_in, conv_w, conv_b, w_out, *, final_g=None, cast=(), tm=1024, ec=256):
    B, S, D = x.shape
    x_spec = pl.BlockSpec((None, tm, D), lambda b, i: (b, i, 0))
    E = w_out.shape[0]
    final = final_g is not None
    mod_specs = [pl.BlockSpec((None, 1, D), functools.partial(lambda b, i, k: (b, 0, k), k=k))
                 for k in range(3)]
    in_specs = [x_spec] + mod_specs + [_resident(w_in.shape), _resident(conv_w.shape),
                                       _resident((1, E)), _resident(w_out.shape)]
    args = [x, mod, mod, mod, w_in, conv_w, conv_b.reshape(1, E), w_out]
    if final:
        in_specs.append(_resident((1, D)))
        args.append(final_g.reshape(1, D))
    n_tiles = S // tm
    cast_in, cast_out, cast_shape = _cast_specs(cast, B * n_tiles, lambda b, i: b * n_tiles + i)
    res = pl.pallas_call(
        functools.partial(_conv_layer_kernel, ec=ec, final=final, n_cast=len(cast)),
        grid=(B, n_tiles),
        in_specs=in_specs + cast_in,
        out_specs=[pl.BlockSpec((None, tm, D), lambda b, i: (b, i, 0))] + cast_out,
        out_shape=[jax.ShapeDtypeStruct((B, S, D), F32)] + cast_shape,
        scratch_shapes=[pltpu.VMEM((E // ec, SUBLANES, ec), F32),
                        pltpu.VMEM((SUBLANES + tm, ec), F32)],
        compiler_params=_cparams(("arbitrary", "arbitrary")),
        name="conv_layer_final" if final else "conv_layer",
    )(*args, *cast)
    return res if cast else res[0]


def _t5_bucket(dist):
    max_exact = REL_BUCKETS // 2
    d = np.maximum(dist, 0)
    ratio = np.log(np.maximum(d, 1) / max_exact) / math.log(REL_MAX_DIST / max_exact)
    large = np.minimum(max_exact + (ratio * (REL_BUCKETS - max_exact)).astype(np.int64), REL_BUCKETS - 1)
    return np.where(d < max_exact, d, large).astype(np.int32)


def _band_structure(window, dil):
    wsub = window // dil
    qi = np.arange(ATT_BLOCK)[:, None]
    ki = np.arange(2 * ATT_BLOCK)[None, :]
    dist = ATT_BLOCK + qi - ki
    in_band = (dist >= 0) & (dist <= wsub)
    first = in_band & (ki >= ATT_BLOCK)
    return _t5_bucket(dist * dil), np.stack([in_band, first]).astype(np.float32)


def _qkv_kernel(x_ref, shift_ref, scale_ref, w_ref, *refs, dils, n_cast):
    n_out = 3 * len(dils)
    cast_src, out_refs = refs[:n_cast], refs[n_cast:n_cast + n_out]
    cast_dst, h_ref = refs[n_cast + n_out:-1], refs[-1]
    tn, D = x_ref.shape
    E = out_refs[0].shape[-1]
    _cast_slabs(cast_src, cast_dst)
    h = _mod_norm(x_ref[...], scale_ref[...], shift_ref[...])
    n_col = D // LANES
    for c in range(n_col):
        h_ref[c] = h[:, c * LANES:(c + 1) * LANES]
    for g, dil in enumerate(dils):
        n = tn // dil
        if dil == 1:
            hp = h
        else:
            hp = jnp.concatenate(
                [jnp.concatenate([h_ref[c, pl.ds(r, n, stride=dil), :] for c in range(n_col)], axis=1)
                 for r in range(dil)], axis=0)
        hp = hp.astype(BF16)
        for j in range(3):
            c0 = (3 * g + j) * E
            res = _dot(hp, w_ref[:, c0:c0 + E])
            if j == 0:
                res = res * (LOG2E / math.sqrt(HEAD_DIM))
            out_refs[3 * g + j][...] = res.astype(BF16).reshape(dil, n, E)


def _qkv_proj(x, mod, w_qkv, dils, E, cast=(), tn=512):
    B, S, D = x.shape
    n_tiles = S // tn
    cast_in, cast_out, cast_shape = _cast_specs(cast, B * n_tiles, lambda b, i: b * n_tiles + i)
    mod_specs = [pl.BlockSpec((None, 1, D), functools.partial(lambda b, i, k: (b, 0, k), k=k))
                 for k in range(2)]
    out_specs, out_shape = [], []
    for dil in dils:
        out_specs += [pl.BlockSpec((None, dil, tn // dil, E), lambda b, i: (b, 0, i, 0))] * 3
        out_shape += [jax.ShapeDtypeStruct((B, dil, S // dil, E), BF16)] * 3
    return pl.pallas_call(
        functools.partial(_qkv_kernel, dils=dils, n_cast=len(cast)),
        grid=(B, n_tiles),
        in_specs=[pl.BlockSpec((None, tn, D), lambda b, i: (b, i, 0))] + mod_specs
                 + [_resident(w_qkv.shape)] + cast_in,
        out_specs=out_specs + cast_out,
        out_shape=out_shape + cast_shape,
        scratch_shapes=[pltpu.VMEM((D // LANES, tn, LANES), F32)],
        compiler_params=_cparams(("arbitrary", "arbitrary")),
        name="qkv_proj",
    )(x, mod, mod, w_qkv, *cast)


def _attn_kernel(*refs, nr, nsub, has_prev, stride, n_heads):
    if has_prev:
        (q_ref, k_ref, v_ref, kp_ref, vp_ref, rb_ref, bucket_ref, mask_ref,
         o_ref, m_ref, den_ref, tbl_ref, s_ref, p_ref, stat_ref) = refs
    else:
        (q_ref, k_ref, v_ref, rb_ref, bucket_ref, mask_ref,
         o_ref, m_ref, den_ref, tbl_ref, s_ref, p_ref, stat_ref) = refs
    n_pairs = n_heads // 2
    pair_rows = 2 * ATT_BLOCK
    nk = s_ref.shape[1]
    step = pl.program_id(1)

    @pl.when((pl.program_id(0) == 0) & (step == 0))
    def _():
        def fill(b, carry):
            hit = bucket_ref[...] == b
            for h in range(n_heads):
                rows = slice(h * ATT_BLOCK, (h + 1) * ATT_BLOCK)
                s_ref[rows, :] = jnp.where(hit, rb_ref[b, h] * LOG2E, s_ref[rows, :])
            return carry
        s_ref[...] = jnp.zeros_like(s_ref)
        lax.fori_loop(0, REL_BUCKETS, fill, 0)
        for t in range(tbl_ref.shape[0]):
            for h in range(n_heads):
                rows = slice(h * ATT_BLOCK, (h + 1) * ATT_BLOCK)
                tbl_ref[t, rows, :] = jnp.where(mask_ref[t] > 0.0, s_ref[rows, :], NEG_INF)
        stat_ref[0] = jnp.zeros((ATT_BLOCK, LANES), F32)
        stat_ref[1] = jnp.ones((ATT_BLOCK, LANES), F32)

    lane = lax.broadcasted_iota(jnp.int32, (ATT_BLOCK, LANES), 1)
    lo = lane < HEAD_DIM

    def block(q_of, k_of, v_of, table, out_rows):
        for hp in range(n_pairs):
            cols = slice(hp * LANES, (hp + 1) * LANES)
            rows = slice(hp * pair_rows, (hp + 1) * pair_rows)
            q2 = q_of(cols)
            zero = jnp.zeros_like(q2)
            q_pair = jnp.concatenate([jnp.where(lo, q2, zero), jnp.where(lo, zero, q2)], axis=0)
            s_ref[rows, :] = lax.dot_general(q_pair, k_of(cols), (((1,), (1,)), ((), ())),
                                             preferred_element_type=F32)
        head_rows = [slice(h * ATT_BLOCK, (h + 1) * ATT_BLOCK) for h in range(n_heads)]
        if not has_prev:
            maxima = []
            for rows in head_rows:
                s = s_ref[rows, :] + tbl_ref[table, rows, :]
                s_ref[rows, :] = s
                maxima.append(jnp.max(s, axis=-1, keepdims=True))
        for h, rows in enumerate(head_rows):
            if has_prev:
                s = s_ref[rows, :] + tbl_ref[table, rows, :]
                m = jnp.max(s, axis=-1, keepdims=True)
            else:
                s, m = s_ref[rows, :], maxima[h]
            p = jnp.exp2(s - m)
            den = jnp.sum(p, axis=-1, keepdims=True)
            p_ref[rows, :] = p.astype(BF16)
            stat_ref[0, :, h:h + 1] = m
            stat_ref[1, :, h:h + 1] = den
        m_ref[out_rows, :] = stat_ref[0]
        den_ref[out_rows, :] = stat_ref[1]
        for hp in range(n_pairs):
            cols = slice(hp * LANES, (hp + 1) * LANES)
            rows = slice(hp * pair_rows, (hp + 1) * pair_rows)
            res = _dot(p_ref[rows, :], v_of(cols))
            o_pair = jnp.where(lo, res[:ATT_BLOCK], res[ATT_BLOCK:])
            bits = lax.bitcast_convert_type(o_pair.astype(BF16).astype(F32), jnp.uint32)
            if hp % 2 == 0:
                high = bits
            else:
                o_ref[hp // 2, out_rows, :] = high | (bits >> 16)

    for r in range(nr):
        for j in range(nsub):
            row = j * ATT_BLOCK
            q_of = lambda cols, r=r, row=row: q_ref[r, row:row + ATT_BLOCK, cols]
            if has_prev:
                out_rows = _rows(row * stride + r, ATT_BLOCK, stride)
                if j == 0:
                    k_of = lambda cols, r=r: jnp.concatenate([kp_ref[r, :, cols], k_ref[r, 0:ATT_BLOCK, cols]], axis=0)
                    v_of = lambda cols, r=r: jnp.concatenate([vp_ref[r, :, cols], v_ref[r, 0:ATT_BLOCK, cols]], axis=0)
                    table = jnp.where(step == 0, 1, 0)
                else:
                    k_of = lambda cols, r=r, row=row: k_ref[r, row - ATT_BLOCK:row + ATT_BLOCK, cols]
                    v_of = lambda cols, r=r, row=row: v_ref[r, row - ATT_BLOCK:row + ATT_BLOCK, cols]
                    table = 0
            else:
                out_rows = _rows(step * nr + r, ATT_BLOCK, stride)
                k_of = lambda cols, r=r: k_ref[r, :, cols]
                v_of = lambda cols, r=r: v_ref[r, :, cols]
                table = 0
            block(q_of, k_of, v_of, table, out_rows)


def _attention(q, k, v, rel_bias_g, window, dil, tokens_per_step=1024):
    B, _, L, E = q.shape
    S = L * dil
    H = E // HEAD_DIM
    bucket, masks = _band_structure(window, dil)
    has_prev = L > ATT_BLOCK
    if has_prev:
        nr = dil
        nsub = tokens_per_step // (dil * ATT_BLOCK)
        tq = nsub * ATT_BLOCK
        n_steps = L // tq
        nk = 2 * ATT_BLOCK
        qkv_spec = pl.BlockSpec((None, nr, tq, E), lambda b, i: (b, 0, i, 0))
        prev_spec = pl.BlockSpec((None, nr, ATT_BLOCK, E),
                                 lambda b, i: (b, 0, jnp.maximum(i * nsub - 1, 0), 0))
        in_specs = [qkv_spec] * 3 + [prev_spec] * 2
        args = [q, k, v, k, v]
        out_block_rows = tokens_per_step
        out_step = lambda i: i
    else:
        nr = tokens_per_step // ATT_BLOCK
        nsub = 1
        n_steps = dil // nr
        nk = ATT_BLOCK
        bucket, masks = bucket[:, ATT_BLOCK:], masks[1:, :, ATT_BLOCK:]
        qkv_spec = pl.BlockSpec((None, nr, ATT_BLOCK, E), lambda b, i: (b, i, 0, 0))
        in_specs = [qkv_spec] * 3
        args = [q, k, v]
        out_block_rows = S
        out_step = lambda i: 0
    in_specs += [pl.BlockSpec(memory_space=pltpu.SMEM), _resident(bucket.shape), _resident(masks.shape)]
    args += [rel_bias_g, jnp.asarray(bucket), jnp.asarray(masks)]
    stat_spec = pl.BlockSpec((None, out_block_rows, LANES), lambda b, i: (b, out_step(i), 0))
    stat_shape = jax.ShapeDtypeStruct((B, S, LANES), F32)
    return pl.pallas_call(
        functools.partial(_attn_kernel, nr=nr, nsub=nsub, has_prev=has_prev, stride=dil, n_heads=H),
        grid=(B, n_steps),
        in_specs=in_specs,
        out_specs=[pl.BlockSpec((None, E // LANES // 2, out_block_rows, LANES),
                                lambda b, i: (b, 0, out_step(i), 0)),
                   stat_spec, stat_spec],
        out_shape=[jax.ShapeDtypeStruct((B, E // LANES // 2, S, LANES), jnp.uint32),
                   stat_shape, stat_shape],
        scratch_shapes=[pltpu.VMEM((masks.shape[0], H * ATT_BLOCK, nk), F32),
                        pltpu.VMEM((H * ATT_BLOCK, nk), F32),
                        pltpu.VMEM((H * ATT_BLOCK, nk), BF16),
                        pltpu.VMEM((2, ATT_BLOCK, LANES), F32)],
        compiler_params=_cparams(("arbitrary", "arbitrary")),
        name=f"attention_d{dil}",
    )(*args)


def _attn_out_kernel(x_ref, shift_ref, scale_ref, gate_ref, o0_ref, o1_ref, o2_ref,
                     m0_ref, m1_ref, m2_ref, d0_ref, d1_ref, d2_ref, wz_ref, w_out_ref, expand_ref,
                     out_ref):
    tm, D = x_ref.shape
    E = wz_ref.shape[1]
    n_heads = E // HEAD_DIM
    x = x_ref[...]
    h = _mod_norm(x, scale_ref[...], shift_ref[...]).astype(BF16)
    z = _dot(h, wz_ref[...])
    m0, m1, m2 = m0_ref[...], m1_ref[...], m2_ref[...]
    m = jnp.maximum(jnp.maximum(m0, m1), m2)
    e0, e1, e2 = jnp.exp2(m0 - m), jnp.exp2(m1 - m), jnp.exp2(m2 - m)
    den = e0 * d0_ref[...] + e1 * d1_ref[...] + e2 * d2_ref[...]
    head_lane = lax.broadcasted_iota(jnp.int32, (tm, LANES), 1) < n_heads
    packed = None
    for g, e in enumerate((e0, e1, e2)):
        w = jnp.where(head_lane, e / den, 0.0)
        hi = w.astype(BF16).astype(F32)
        for part, val in enumerate((hi, w - hi)):
            shift = (2 * g + part) * n_heads
            val = pltpu.roll(val, shift, axis=1) if shift else val
            packed = val if packed is None else packed + val
    wexp = _dot(packed.astype(BF16), expand_ref[...])
    o = None
    for g, o_ref in enumerate((o0_ref, o1_ref, o2_ref)):
        parts = []
        for w in range(o_ref.shape[0]):
            word = o_ref[w]
            parts.append(lax.bitcast_convert_type(word & jnp.uint32(0xFFFF0000), F32))
            parts.append(lax.bitcast_convert_type(word << 16, F32))
        og = jnp.concatenate(parts, axis=-1)
        term = wexp[:, g * E:(g + 1) * E] * og
        o = term if o is None else o + term
    y = (o * _silu(z)).astype(BF16)
    out_ref[...] = x + gate_ref[...] * _dot(y, w_out_ref[...])


def _attn_out(x, mod, os_, ms, dens, w_in, w_out, tm=512):
    B, S, D = x.shape
    E = w_out.shape[0]
    z_block = w_in.shape[1] // E - 1
    tile = lambda w: pl.BlockSpec((None, tm, w), lambda b, i: (b, i, 0))
    mod_specs = [pl.BlockSpec((None, 1, D), functools.partial(lambda b, i, k: (b, 0, k), k=k))
                 for k in range(3)]
    n_groups, H = len(os_), E // HEAD_DIM
    expand = np.zeros((LANES, n_groups * E), np.float32)
    for g in range(n_groups):
        for part in range(2):
            for hd in range(H):
                expand[(2 * g + part) * H + hd, g * E + hd * HEAD_DIM:g * E + (hd + 1) * HEAD_DIM] = 1.0
    expand = jnp.asarray(expand, BF16)
    return pl.pallas_call(
        _attn_out_kernel,
        grid=(B, S // tm),
        in_specs=[tile(D)] + mod_specs
                 + [pl.BlockSpec((None, E // LANES // 2, tm, LANES), lambda b, i: (b, 0, i, 0))] * 3
                 + [tile(LANES)] * 6
                 + [pl.BlockSpec((D, E), lambda b, i: (0, z_block), pipeline_mode=pl.Buffered(1)),
                    _resident(w_out.shape), _resident(expand.shape)],
        out_specs=tile(D),
        out_shape=jax.ShapeDtypeStruct((B, S, D), F32),
        compiler_params=_cparams(("arbitrary", "arbitrary")),
        name="attention_out",
    )(x, mod, mod, mod, *os_, *ms, *dens, w_in, w_out, expand)


def _s5_kernel(x_ref, shift_ref, scale_ref, gate_ref, w_in_ref, bbr_ref, bbi_ref, ar_ref, ai_ref,
               cr_ref, ci_ref, dskip_ref, w_glu_ref, b_glu_ref, w_out_ref,
               o_ref, state_ref, sr_ref, si_ref, *, scan_lanes):
    B, ts, D = x_ref.shape
    R = ts * B
    n_chunks, cn, ck = bbr_ref.shape
    E = n_chunks * ck
    dot_t = lambda a, b: lax.dot_general(a, b, (((1,), (1,)), ((), ())), preferred_element_type=F32)

    @pl.when(pl.program_id(0) == 0)
    def _():
        state_ref[...] = jnp.zeros_like(state_ref)

    x3 = jnp.swapaxes(x_ref[...], 0, 1)
    ms = jnp.mean(x3 * x3, axis=-1, keepdims=True)
    h3 = (x3 * lax.rsqrt(ms + EPS)) * (1.0 + scale_ref[...][None]) + shift_ref[...][None]
    h = h3.reshape(R, D).astype(BF16)
    uz = _dot(h, w_in_ref[...])
    u, z = uz[:, :E], uz[:, E:]
    u16 = u.astype(BF16)

    def in_proj(c):
        uc = u16[:, c * ck:(c + 1) * ck]
        sr_ref[c] = dot_t(uc, bbr_ref[c])
        si_ref[c] = dot_t(uc, bbi_ref[c])

    def scan(c):
        for l0 in range(0, cn, scan_lanes):
            ls = slice(l0, l0 + scan_lanes)
            gl = slice(c * cn + l0, c * cn + l0 + scan_lanes)
            ar, ai = ar_ref[:, gl], ai_ref[:, gl]
            pr, pi = state_ref[0, :, gl], state_ref[1, :, gl]
            for t in range(ts):
                rows = slice(t * B, (t + 1) * B)
                nr = ar * pr - ai * pi + sr_ref[c, rows, ls]
                ni = ar * pi + ai * pr + si_ref[c, rows, ls]
                sr_ref[c, rows, ls] = nr
                si_ref[c, rows, ls] = ni
                pr, pi = nr, ni
            state_ref[0, :, gl] = pr
            state_ref[1, :, gl] = pi

    def out_proj(c):
        cols = slice(c * ck, (c + 1) * ck)
        y = (dot_t(sr_ref[c].astype(BF16), cr_ref[c]) - dot_t(si_ref[c].astype(BF16), ci_ref[c])
             + dskip_ref[:, cols] * u[:, cols])
        return y * (0.5 * (1.0 + jnp.tanh(math.sqrt(2.0 / math.pi) * (y + 0.044715 * (y * y * y)))))

    g_parts = []
    in_proj(0)
    for c in range(n_chunks):
        if c + 1 < n_chunks:
            in_proj(c + 1)
        scan(c)
        if c >= 1:
            g_parts.append(out_proj(c - 1))
    g_parts.append(out_proj(n_chunks - 1))
    g = jnp.concatenate(g_parts, axis=-1)
    y = g * _sigmoid(_dot(g.astype(BF16), w_glu_ref[...]) + b_glu_ref[...])
    y = (y * _silu(z)).astype(BF16)
    out = _dot(y, w_out_ref[...]).reshape(ts, B, D)
    o_ref[...] = jnp.swapaxes(x3 + gate_ref[...][None] * out, 0, 1)


def _s5_layer(x, mod, w_in, bbr, bbi, ar, ai, cr, ci, d_skip, w_glu, b_glu, w_out, ts=64, scan_lanes=512):
    B, S, D = x.shape
    E = w_glu.shape[0]
    N = ar.shape[1]
    mod_specs = [pl.BlockSpec((B, D), functools.partial(lambda i, k: (0, k), k=k)) for k in range(3)]
    weights = [w_in, bbr, bbi, ar, ai, cr, ci, d_skip.reshape(1, E), w_glu, b_glu.reshape(1, E), w_out]
    return pl.pallas_call(
        functools.partial(_s5_kernel, scan_lanes=scan_lanes),
        grid=(S // ts,),
        in_specs=[pl.BlockSpec((B, ts, D), lambda i: (0, i, 0))] + mod_specs
                 + [_resident(w.shape) for w in weights],
        out_specs=pl.BlockSpec((B, ts, D), lambda i: (0, i, 0)),
        out_shape=jax.ShapeDtypeStruct((B, S, D), F32),
        scratch_shapes=[pltpu.VMEM((2, B, N), F32),
                        pltpu.VMEM((bbr.shape[0], ts * B, bbr.shape[1]), F32),
                        pltpu.VMEM((bbr.shape[0], ts * B, bbr.shape[1]), F32)],
        compiler_params=_cparams(("arbitrary",)),
        name="s5_layer",
    )(x, mod, mod, mod, *weights)


def _s5_params(log_dt, lambda_re, lambda_im, b_re, b_im, c_re, c_im, B, groups_per_chunk=16):
    G, P = lambda_re.shape
    K = b_re.shape[2]
    dt = jnp.exp(log_dt.astype(F32))[:, None]
    lr = lambda_re.astype(F32)
    li = lambda_im.astype(F32)
    mag = jnp.exp(lr * dt)
    ar = mag * jnp.cos(li * dt)
    ai = mag * jnp.sin(li * dt)
    inv = 1.0 / (lr * lr + li * li)
    qr = ((ar - 1.0) * lr + ai * li) * inv
    qi = (ai * lr - (ar - 1.0) * li) * inv
    br = b_re.astype(F32)
    bi = b_im.astype(F32)
    bbr = qr[..., None] * br - qi[..., None] * bi
    bbi = qr[..., None] * bi + qi[..., None] * br
    gc = groups_per_chunk
    nc = G // gc

    def block_diag(m):
        a, b = m.shape[1:]
        rows = m.astype(BF16).reshape(nc, gc * a, b)
        on_diag = np.kron(np.eye(gc, dtype=bool), np.ones((a, b), dtype=bool))
        return jnp.where(on_diag, jnp.tile(rows, (1, 1, gc)), jnp.zeros((), BF16))

    bcast = lambda a: jnp.broadcast_to(a.reshape(1, G * P), (B, G * P))
    return (block_diag(bbr), block_diag(bbi), bcast(ar), bcast(ai),
            block_diag(c_re.astype(F32)), block_diag(c_im.astype(F32)))


def kernel(x, c, rel_bias, final_g, l0_ada_w, l0_ada_b, l0_w_in, l0_conv_w, l0_conv_b, l0_w_out, l1_ada_w, l1_ada_b, l1_w_in, l1_w_out, l2_ada_w, l2_ada_b, l2_w_in, l2_log_dt, l2_lambda_re, l2_lambda_im, l2_b_re, l2_b_im, l2_c_re, l2_c_im, l2_d_skip, l2_w_glu, l2_b_glu, l2_w_out, l3_ada_w, l3_ada_b, l3_w_in, l3_conv_w, l3_conv_b, l3_w_out):
    B, S, D = x.shape
    assert B == SUBLANES, "the S5 scan keeps the batch on the sublane axis"
    mods = _modulation(c, (l0_ada_w, l1_ada_w, l2_ada_w, l3_ada_w), (l0_ada_b, l1_ada_b, l2_ada_b, l3_ada_b))
    mod3 = [m.reshape(B, 1, 3 * D) for m in mods]

    x, w1_in, w1_out = _conv_layer(x, mod3[0], l0_w_in.astype(BF16), l0_conv_w, l0_conv_b,
                                   l0_w_out.astype(BF16), cast=(l1_w_in, l1_w_out))

    dils = tuple(dil for _, dil in DILATION_PATTERNS)
    n_qkv = 3 * len(dils)
    E = l1_w_out.shape[0]
    H = E // HEAD_DIM
    res = _qkv_proj(x, mod3[1], w1_in, dils, E, cast=(l2_w_in, l2_w_glu, l2_w_out, l3_w_in, l3_w_out))
    qkv, (w2_in, w2_glu, w2_out, w3_in, w3_out) = res[:n_qkv], res[n_qkv:]
    os_, ms, dens = [], [], []
    for g, (window, dil) in enumerate(DILATION_PATTERNS):
        o, m, den = _attention(*qkv[3 * g:3 * g + 3], rel_bias[:, g * H:(g + 1) * H], window, dil)
        os_.append(o)
        ms.append(m)
        dens.append(den)
    x = _attn_out(x, mod3[1], os_, ms, dens, w1_in, w1_out)

    ssm = _s5_params(l2_log_dt, l2_lambda_re, l2_lambda_im, l2_b_re, l2_b_im, l2_c_re, l2_c_im, B)
    x = _s5_layer(x, mods[2], w2_in, *ssm, l2_d_skip, w2_glu, l2_b_glu, w2_out)

    return _conv_layer(x, mod3[3], w3_in, l3_conv_w, l3_conv_b, w3_out, final_g=final_g)
```

```python
import functools
import math

import numpy as np
import jax
import jax.numpy as jnp
from jax import lax
from jax.experimental import pallas as pl
from jax.experimental.pallas import tpu as pltpu

EPS = 1e-6
HEAD_DIM = 64
ATT_BLOCK = 128
DILATION_PATTERNS = ((128, 1), (512, 4), (2048, 16))
REL_BUCKETS = 32
REL_MAX_DIST = 2048
NEG_INF = -1e30
LOG2E = math.log2(math.e)
LANES = 128
SUBLANES = 8
VMEM_LIMIT_BYTES = 60 * 1024 * 1024

F32 = jnp.float32
BF16 = jnp.bfloat16


def _cparams(semantics):
    return pltpu.CompilerParams(dimension_semantics=semantics, vmem_limit_bytes=VMEM_LIMIT_BYTES)


def _resident(shape):
    nd = len(shape)
    return pl.BlockSpec(shape, lambda *_: (0,) * nd, pipeline_mode=pl.Buffered(1))


def _silu(z):
    return z * (1.0 / (1.0 + jnp.exp(-z)))


def _sigmoid(z):
    return 1.0 / (1.0 + jnp.exp(-z))


def _mod_norm(x, scale, shift):
    ms = jnp.mean(x * x, axis=-1, keepdims=True)
    return (x * lax.rsqrt(ms + EPS)) * (1.0 + scale) + shift


def _dot(a, b):
    return jnp.dot(a, b, preferred_element_type=F32)


def _rows(start, size, stride):
    return pl.ds(start, size) if stride == 1 else pl.ds(start, size, stride=stride)


def _modulation_kernel(c_ref, w0, w1, w2, w3, b0, b1, b2, b3, o0, o1, o2, o3):
    sc = _silu(c_ref[...]).astype(BF16)
    for w, b, o in ((w0, b0, o0), (w1, b1, o1), (w2, b2, o2), (w3, b3, o3)):
        o[...] = _dot(sc, w[...].astype(BF16)) + b[...]


def _modulation(c, ws, bs, tn=512):
    B, D = c.shape
    N = ws[0].shape[1]
    w_spec = pl.BlockSpec((D, tn), lambda j: (0, j))
    b_spec = pl.BlockSpec((1, tn), lambda j: (0, j))
    o_spec = pl.BlockSpec((B, tn), lambda j: (0, j))
    return pl.pallas_call(
        _modulation_kernel,
        grid=(N // tn,),
        in_specs=[pl.BlockSpec((B, D), lambda j: (0, 0))] + [w_spec] * 4 + [b_spec] * 4,
        out_specs=[o_spec] * 4,
        out_shape=[jax.ShapeDtypeStruct((B, N), F32)] * 4,
        compiler_params=_cparams(("arbitrary",)),
        name="modulation",
    )(c, *ws, *[b.reshape(1, N) for b in bs])


def _cast_specs(arrays, n_steps, step_of):
    in_specs, out_specs, out_shape = [], [], []
    for a in arrays:
        rows, cols = a.shape
        spec = pl.BlockSpec((rows // n_steps, cols), lambda *ids: (step_of(*ids), 0))
        in_specs.append(spec)
        out_specs.append(spec)
        out_shape.append(jax.ShapeDtypeStruct(a.shape, BF16))
    return in_specs, out_specs, out_shape


def _cast_slabs(src_refs, dst_refs):
    for src, dst in zip(src_refs, dst_refs):
        dst[...] = src[...].astype(BF16)


def _conv_layer_kernel(*refs, ec, final, n_cast):
    n_in = 9 if final else 8
    (x_ref, shift_ref, scale_ref, gate_ref, w_in_ref, cw_ref, cb_ref, w_out_ref) = refs[:8]
    g_ref = refs[8] if final else None
    cast_src = refs[n_in:n_in + n_cast]
    o_ref = refs[n_in + n_cast]
    cast_dst = refs[n_in + n_cast + 1:n_in + 2 * n_cast + 1]
    carry_ref, vs_ref = refs[n_in + 2 * n_cast + 1:]
    tm, D = x_ref.shape
    E = w_out_ref.shape[0]

    @pl.when(pl.program_id(1) == 0)
    def _():
        carry_ref[...] = jnp.zeros_like(carry_ref)

    _cast_slabs(cast_src, cast_dst)
    x = x_ref[...]
    h = _mod_norm(x, scale_ref[...], shift_ref[...]).astype(BF16)
    n_chunks = E // ec

    def in_proj(j):
        c0 = j * ec
        return tuple(_dot(h, w_in_ref[:, k * E + c0:k * E + c0 + ec]) for k in range(4))

    def gated_conv(j, proj):
        u, gc, gb, z = proj
        c0 = j * ec
        v = gc * u
        vs_ref[0:SUBLANES, :] = carry_ref[j]
        vs_ref[SUBLANES:SUBLANES + tm, :] = v
        carry_ref[j] = v[tm - SUBLANES:tm, :]
        v1 = vs_ref[SUBLANES - 1:SUBLANES - 1 + tm, :]
        v2 = vs_ref[SUBLANES - 2:SUBLANES - 2 + tm, :]
        cw = cw_ref[:, c0:c0 + ec]
        conv = cw[0:1, :] * v2 + cw[1:2, :] * v1 + cw[2:3, :] * v + cb_ref[:, c0:c0 + ec]
        return (gb * conv * _silu(z)).astype(BF16)

    acc = jnp.zeros((tm, D), F32)
    proj = in_proj(0)
    for j in range(n_chunks):
        nxt = in_proj(j + 1) if j + 1 < n_chunks else None
        y = gated_conv(j, proj)
        acc = acc + _dot(y, w_out_ref[j * ec:(j + 1) * ec, :])
        proj = nxt
    out = x + gate_ref[...] * acc
    if final:
        ms = jnp.mean(out * out, axis=-1, keepdims=True)
        out = out * lax.rsqrt(ms + EPS) * g_ref[...]
    o_ref[...] = out


def _conv_layer(x, mod, w_in, conv_w, conv_b, w_out, *, final_g=None, cast=(), tm=1024, ec=256):
    B, S, D = x.shape
    x_spec = pl.BlockSpec((None, tm, D), lambda b, i: (b, i, 0))
    E = w_out.shape[0]
    final = final_g is not None
    mod_specs = [pl.BlockSpec((None, 1, D), functools.partial(lambda b, i, k: (b, 0, k), k=k))
                 for k in range(3)]
    in_specs = [x_spec] + mod_specs + [_resident(w_in.shape), _resident(conv_w.shape),
                                       _resident((1, E)), _resident(w_out.shape)]
    args = [x, mod, mod, mod, w_in, conv_w, conv_b.reshape(1, E), w_out]
    if final:
        in_specs.append(_resident((1, D)))
        args.append(final_g.reshape(1, D))
    n_tiles = S // tm
    cast_in, cast_out, cast_shape = _cast_specs(cast, B * n_tiles, lambda b, i: b * n_tiles + i)
    res = pl.pallas_call(
        functools.partial(_conv_layer_kernel, ec=ec, final=final, n_cast=len(cast)),
        grid=(B, n_tiles),
        in_specs=in_specs + cast_in,
        out_specs=[pl.BlockSpec((None, tm, D), lambda b, i: (b, i, 0))] + cast_out,
        out_shape=[jax.ShapeDtypeStruct((B, S, D), F32)] + cast_shape,
        scratch_shapes=[pltpu.VMEM((E // ec, SUBLANES, ec), F32),
                        pltpu.VMEM((SUBLANES + tm, ec), F32)],
        compiler_params=_cparams(("arbitrary", "arbitrary")),
        name="conv_layer_final" if final else "conv_layer",
    )(*args, *cast)
    return res if cast else res[0]


def _t5_bucket(dist):
    max_exact = REL_BUCKETS // 2
    d = np.maximum(dist, 0)
    ratio = np.log(np.maximum(d, 1) / max_exact) / math.log(REL_MAX_DIST / max_exact)
    large = np.minimum(max_exact + (ratio * (REL_BUCKETS - max_exact)).astype(np.int64), REL_BUCKETS - 1)
    return np.where(d < max_exact, d, large).astype(np.int32)


def _band_structure(window, dil):
    wsub = window // dil
    qi = np.arange(ATT_BLOCK)[:, None]
    ki = np.arange(2 * ATT_BLOCK)[None, :]
    dist = ATT_BLOCK + qi - ki
    in_band = (dist >= 0) & (dist <= wsub)
    first = in_band & (ki >= ATT_BLOCK)
    return _t5_bucket(dist * dil), np.stack([in_band, first]).astype(np.float32)


def _qkv_kernel(x_ref, shift_ref, scale_ref, w_ref, *refs, dils, n_cast):
    n_out = 3 * len(dils)
    cast_src, out_refs = refs[:n_cast], refs[n_cast:n_cast + n_out]
    cast_dst, h_ref = refs[n_cast + n_out:-1], refs[-1]
    tn, D = x_ref.shape
    E = out_refs[0].shape[-1]
    _cast_slabs(cast_src, cast_dst)
    h = _mod_norm(x_ref[...], scale_ref[...], shift_ref[...])
    n_col = D // LANES
    for c in range(n_col):
        h_ref[c] = h[:, c * LANES:(c + 1) * LANES]
    for g, dil in enumerate(dils):
        n = tn // dil
        if dil == 1:
            hp = h
        else:
            hp = jnp.concatenate(
                [jnp.concatenate([h_ref[c, pl.ds(r, n, stride=dil), :] for c in range(n_col)], axis=1)
                 for r in range(dil)], axis=0)
        hp = hp.astype(BF16)
        for j in range(3):
            c0 = (3 * g + j) * E
            res = _dot(hp, w_ref[:, c0:c0 + E])
            if j == 0:
                res = res * (LOG2E / math.sqrt(HEAD_DIM))
            out_refs[3 * g + j][...] = res.astype(BF16).reshape(dil, n, E)


def _qkv_proj(x, mod, w_qkv, dils, E, cast=(), tn=512):
    B, S, D = x.shape
    n_tiles = S // tn
    cast_in, cast_out, cast_shape = _cast_specs(cast, B * n_tiles, lambda b, i: b * n_tiles + i)
    mod_specs = [pl.BlockSpec((None, 1, D), functools.partial(lambda b, i, k: (b, 0, k), k=k))
                 for k in range(2)]
    out_specs, out_shape = [], []
    for dil in dils:
        out_specs += [pl.BlockSpec((None, dil, tn // dil, E), lambda b, i: (b, 0, i, 0))] * 3
        out_shape += [jax.ShapeDtypeStruct((B, dil, S // dil, E), BF16)] * 3
    return pl.pallas_call(
        functools.partial(_qkv_kernel, dils=dils, n_cast=len(cast)),
        grid=(B, n_tiles),
        in_specs=[pl.BlockSpec((None, tn, D), lambda b, i: (b, i, 0))] + mod_specs
                 + [_resident(w_qkv.shape)] + cast_in,
        out_specs=out_specs + cast_out,
        out_shape=out_shape + cast_shape,
        scratch_shapes=[pltpu.VMEM((D // LANES, tn, LANES), F32)],
        compiler_params=_cparams(("arbitrary", "arbitrary")),
        name="qkv_proj",
    )(x, mod, mod, w_qkv, *cast)


def _attn_kernel(*refs, nr, nsub, has_prev, stride, n_heads):
    if has_prev:
        (q_ref, k_ref, v_ref, kp_ref, vp_ref, rb_ref, bucket_ref, mask_ref,
         o_ref, m_ref, den_ref, tbl_ref, s_ref, p_ref, stat_ref) = refs
    else:
        (q_ref, k_ref, v_ref, rb_ref, bucket_ref, mask_ref,
         o_ref, m_ref, den_ref, tbl_ref, s_ref, p_ref, stat_ref) = refs
    n_pairs = n_heads // 2
    pair_rows = 2 * ATT_BLOCK
    nk = s_ref.shape[1]
    step = pl.program_id(1)

    @pl.when((pl.program_id(0) == 0) & (step == 0))
    def _():
        def fill(b, carry):
            hit = bucket_ref[...] == b
            for h in range(n_heads):
                rows = slice(h * ATT_BLOCK, (h + 1) * ATT_BLOCK)
                s_ref[rows, :] = jnp.where(hit, rb_ref[b, h] * LOG2E, s_ref[rows, :])
            return carry
        s_ref[...] = jnp.zeros_like(s_ref)
        lax.fori_loop(0, REL_BUCKETS, fill, 0)
        for t in range(tbl_ref.shape[0]):
            for h in range(n_heads):
                rows = slice(h * ATT_BLOCK, (h + 1) * ATT_BLOCK)
                tbl_ref[t, rows, :] = jnp.where(mask_ref[t] > 0.0, s_ref[rows, :], NEG_INF)
        stat_ref[0] = jnp.zeros((ATT_BLOCK, LANES), F32)
        stat_ref[1] = jnp.ones((ATT_BLOCK, LANES), F32)

    lane = lax.broadcasted_iota(jnp.int32, (ATT_BLOCK, LANES), 1)
    lo = lane < HEAD_DIM

    def block(q_of, k_of, v_of, table, out_rows):
        for hp in range(n_pairs):
            cols = slice(hp * LANES, (hp + 1) * LANES)
            rows = slice(hp * pair_rows, (hp + 1) * pair_rows)
            q2 = q_of(cols)
            zero = jnp.zeros_like(q2)
            q_pair = jnp.concatenate([jnp.where(lo, q2, zero), jnp.where(lo, zero, q2)], axis=0)
            s_ref[rows, :] = lax.dot_general(q_pair, k_of(cols), (((1,), (1,)), ((), ())),
                                             preferred_element_type=F32)
        head_rows = [slice(h * ATT_BLOCK, (h + 1) * ATT_BLOCK) for h in range(n_heads)]
        if not has_prev:
            maxima = []
            for rows in head_rows:
                s = s_ref[rows, :] + tbl_ref[table, rows, :]
                s_ref[rows, :] = s
                maxima.append(jnp.max(s, axis=-1, keepdims=True))
        for h, rows in enumerate(head_rows):
            if has_prev:
                s = s_ref[rows, :] + tbl_ref[table, rows, :]
                m = jnp.max(s, axis=-1, keepdims=True)
            else:
                s, m = s_ref[rows, :], maxima[h]
            p = jnp.exp2(s - m)
            den = jnp.sum(p, axis=-1, keepdims=True)
            p_ref[rows, :] = p.astype(BF16)
            stat_ref[0, :, h:h + 1] = m
            stat_ref[1, :, h:h + 1] = den
        m_ref[out_rows, :] = stat_ref[0]
        den_ref[out_rows, :] = stat_ref[1]
        for hp in range(n_pairs):
            cols = slice(hp * LANES, (hp + 1) * LANES)
            rows = slice(hp * pair_rows, (hp + 1) * pair_rows)
            res = _dot(p_ref[rows, :], v_of(cols))
            o_pair = jnp.where(lo, res[:ATT_BLOCK], res[ATT_BLOCK:])
            bits = lax.bitcast_convert_type(o_pair.astype(BF16).astype(F32), jnp.uint32)
            if hp % 2 == 0:
                high = bits
            else:
                o_ref[hp // 2, out_rows, :] = high | (bits >> 16)

    for r in range(nr):
        for j in range(nsub):
            row = j * ATT_BLOCK
            q_of = lambda cols, r=r, row=row: q_ref[r, row:row + ATT_BLOCK, cols]
            if has_prev:
                out_rows = _rows(row * stride + r, ATT_BLOCK, stride)
                if j == 0:
                    k_of = lambda cols, r=r: jnp.concatenate([kp_ref[r, :, cols], k_ref[r, 0:ATT_BLOCK, cols]], axis=0)
                    v_of = lambda cols, r=r: jnp.concatenate([vp_ref[r, :, cols], v_ref[r, 0:ATT_BLOCK, cols]], axis=0)
                    table = jnp.where(step == 0, 1, 0)
                else:
                    k_of = lambda cols, r=r, row=row: k_ref[r, row - ATT_BLOCK:row + ATT_BLOCK, cols]
                    v_of = lambda cols, r=r, row=row: v_ref[r, row - ATT_BLOCK:row + ATT_BLOCK, cols]
                    table = 0
            else:
                out_rows = _rows(step * nr + r, ATT_BLOCK, stride)
                k_of = lambda cols, r=r: k_ref[r, :, cols]
                v_of = lambda cols, r=r: v_ref[r, :, cols]
                table = 0
            block(q_of, k_of, v_of, table, out_rows)


def _attention(q, k, v, rel_bias_g, window, dil, tokens_per_step=1024):
    B, _, L, E = q.shape
    S = L * dil
    H = E // HEAD_DIM
    bucket, masks = _band_structure(window, dil)
    has_prev = L > ATT_BLOCK
    if has_prev:
        nr = dil
        nsub = tokens_per_step // (dil * ATT_BLOCK)
        tq = nsub * ATT_BLOCK
        n_steps = L // tq
        nk = 2 * ATT_BLOCK
        qkv_spec = pl.BlockSpec((None, nr, tq, E), lambda b, i: (b, 0, i, 0))
        prev_spec = pl.BlockSpec((None, nr, ATT_BLOCK, E),
                                 lambda b, i: (b, 0, jnp.maximum(i * nsub - 1, 0), 0))
        in_specs = [qkv_spec] * 3 + [prev_spec] * 2
        args = [q, k, v, k, v]
        out_block_rows = tokens_per_step
        out_step = lambda i: i
    else:
        nr = tokens_per_step // ATT_BLOCK
        nsub = 1
        n_steps = dil // nr
        nk = ATT_BLOCK
        bucket, masks = bucket[:, ATT_BLOCK:], masks[1:, :, ATT_BLOCK:]
        qkv_spec = pl.BlockSpec((None, nr, ATT_BLOCK, E), lambda b, i: (b, i, 0, 0))
        in_specs = [qkv_spec] * 3
        args = [q, k, v]
        out_block_rows = S
        out_step = lambda i: 0
    in_specs += [pl.BlockSpec(memory_space=pltpu.SMEM), _resident(bucket.shape), _resident(masks.shape)]
    args += [rel_bias_g, jnp.asarray(bucket), jnp.asarray(masks)]
    stat_spec = pl.BlockSpec((None, out_block_rows, LANES), lambda b, i: (b, out_step(i), 0))
    stat_shape = jax.ShapeDtypeStruct((B, S, LANES), F32)
    return pl.pallas_call(
        functools.partial(_attn_kernel, nr=nr, nsub=nsub, has_prev=has_prev, stride=dil, n_heads=H),
        grid=(B, n_steps),
        in_specs=in_specs,
        out_specs=[pl.BlockSpec((None, E // LANES // 2, out_block_rows, LANES),
                                lambda b, i: (b, 0, out_step(i), 0)),
                   stat_spec, stat_spec],
        out_shape=[jax.ShapeDtypeStruct((B, E // LANES // 2, S, LANES), jnp.uint32),
                   stat_shape, stat_shape],
        scratch_shapes=[pltpu.VMEM((masks.shape[0], H * ATT_BLOCK, nk), F32),
                        pltpu.VMEM((H * ATT_BLOCK, nk), F32),
                        pltpu.VMEM((H * ATT_BLOCK, nk), BF16),
                        pltpu.VMEM((2, ATT_BLOCK, LANES), F32)],
        compiler_params=_cparams(("arbitrary", "arbitrary")),
        name=f"attention_d{dil}",
    )(*args)


def _attn_out_kernel(x_ref, shift_ref, scale_ref, gate_ref, o0_ref, o1_ref, o2_ref,
                     m0_ref, m1_ref, m2_ref, d0_ref, d1_ref, d2_ref, wz_ref, w_out_ref, expand_ref,
                     out_ref):
    tm, D = x_ref.shape
    E = wz_ref.shape[1]
    n_heads = E // HEAD_DIM
    x = x_ref[...]
    h = _mod_norm(x, scale_ref[...], shift_ref[...]).astype(BF16)
    z = _dot(h, wz_ref[...])
    m0, m1, m2 = m0_ref[...], m1_ref[...], m2_ref[...]
    m = jnp.maximum(jnp.maximum(m0, m1), m2)
    e0, e1, e2 = jnp.exp2(m0 - m), jnp.exp2(m1 - m), jnp.exp2(m2 - m)
    den = e0 * d0_ref[...] + e1 * d1_ref[...] + e2 * d2_ref[...]
    head_lane = lax.broadcasted_iota(jnp.int32, (tm, LANES), 1) < n_heads
    packed = None
    for g, e in enumerate((e0, e1, e2)):
        w = jnp.where(head_lane, e / den, 0.0)
        hi = w.astype(BF16).astype(F32)
        for part, val in enumerate((hi, w - hi)):
            shift = (2 * g + part) * n_heads
            val = pltpu.roll(val, shift, axis=1) if shift else val
            packed = val if packed is None else packed + val
    wexp = _dot(packed.astype(BF16), expand_ref[...])
    o = None
    for g, o_ref in enumerate((o0_ref, o1_ref, o2_ref)):
        parts = []
        for w in range(o_ref.shape[0]):
            word = o_ref[w]
            parts.append(lax.bitcast_convert_type(word & jnp.uint32(0xFFFF0000), F32))
            parts.append(lax.bitcast_convert_type(word << 16, F32))
        og = jnp.concatenate(parts, axis=-1)
        term = wexp[:, g * E:(g + 1) * E] * og
        o = term if o is None else o + term
    y = (o * _silu(z)).astype(BF16)
    out_ref[...] = x + gate_ref[...] * _dot(y, w_out_ref[...])


def _attn_out(x, mod, os_, ms, dens, w_in, w_out, tm=512):
    B, S, D = x.shape
    E = w_out.shape[0]
    z_block = w_in.shape[1] // E - 1
    tile = lambda w: pl.BlockSpec((None, tm, w), lambda b, i: (b, i, 0))
    mod_specs = [pl.BlockSpec((None, 1, D), functools.partial(lambda b, i, k: (b, 0, k), k=k))
                 for k in range(3)]
    n_groups, H = len(os_), E // HEAD_DIM
    expand = np.zeros((LANES, n_groups * E), np.float32)
    for g in range(n_groups):
        for part in range(2):
            for hd in range(H):
                expand[(2 * g + part) * H + hd, g * E + hd * HEAD_DIM:g * E + (hd + 1) * HEAD_DIM] = 1.0
    expand = jnp.asarray(expand, BF16)
    return pl.pallas_call(
        _attn_out_kernel,
        grid=(B, S // tm),
        in_specs=[tile(D)] + mod_specs
                 + [pl.BlockSpec((None, E // LANES // 2, tm, LANES), lambda b, i: (b, 0, i, 0))] * 3
                 + [tile(LANES)] * 6
                 + [pl.BlockSpec((D, E), lambda b, i: (0, z_block), pipeline_mode=pl.Buffered(1)),
                    _resident(w_out.shape), _resident(expand.shape)],
        out_specs=tile(D),
        out_shape=jax.ShapeDtypeStruct((B, S, D), F32),
        compiler_params=_cparams(("arbitrary", "arbitrary")),
        name="attention_out",
    )(x, mod, mod, mod, *os_, *ms, *dens, w_in, w_out, expand)


def _s5_kernel(x_ref, shift_ref, scale_ref, gate_ref, w_in_ref, bbr_ref, bbi_ref, ar_ref, ai_ref,
               cr_ref, ci_ref, dskip_ref, w_glu_ref, b_glu_ref, w_out_ref,
               o_ref, state_ref, sr_ref, si_ref, *, scan_lanes):
    B, ts, D = x_ref.shape
    R = ts * B
    n_chunks, cn, ck = bbr_ref.shape
    E = n_chunks * ck
    dot_t = lambda a, b: lax.dot_general(a, b, (((1,), (1,)), ((), ())), preferred_element_type=F32)

    @pl.when(pl.program_id(0) == 0)
    def _():
        state_ref[...] = jnp.zeros_like(state_ref)

    x3 = jnp.swapaxes(x_ref[...], 0, 1)
    ms = jnp.mean(x3 * x3, axis=-1, keepdims=True)
    h3 = (x3 * lax.rsqrt(ms + EPS)) * (1.0 + scale_ref[...][None]) + shift_ref[...][None]
    h = h3.reshape(R, D).astype(BF16)
    u_parts = [None] * n_chunks
    z_parts = [None] * n_chunks

    def w_in_cols(first):
        return _dot(h, w_in_ref[:, first:first + ck])

    def in_proj(c):
        u_parts[c] = w_in_cols(c * ck)
        uc = u_parts[c].astype(BF16)
        sr_ref[c] = dot_t(uc, bbr_ref[c])
        si_ref[c] = dot_t(uc, bbi_ref[c])

    def scan(c):
        for l0 in range(0, cn, scan_lanes):
            ls = slice(l0, l0 + scan_lanes)
            gl = slice(c * cn + l0, c * cn + l0 + scan_lanes)
            ar, ai = ar_ref[:, gl], ai_ref[:, gl]
            pr, pi = state_ref[0, :, gl], state_ref[1, :, gl]
            for t in range(ts):
                rows = slice(t * B, (t + 1) * B)
                nr = ar * pr - ai * pi + sr_ref[c, rows, ls]
                ni = ar * pi + ai * pr + si_ref[c, rows, ls]
                sr_ref[c, rows, ls] = nr
                si_ref[c, rows, ls] = ni
                pr, pi = nr, ni
            state_ref[0, :, gl] = pr
            state_ref[1, :, gl] = pi

    def out_proj(c):
        cols = slice(c * ck, (c + 1) * ck)
        y = (dot_t(sr_ref[c].astype(BF16), cr_ref[c]) - dot_t(si_ref[c].astype(BF16), ci_ref[c])
             + dskip_ref[:, cols] * u_parts[c])
        z_parts[c] = w_in_cols(E + c * ck)
        return y * (0.5 * (1.0 + jnp.tanh(math.sqrt(2.0 / math.pi) * (y + 0.044715 * (y * y * y)))))

    g_parts = []
    in_proj(0)
    for c in range(n_chunks):
        if c + 1 < n_chunks:
            in_proj(c + 1)
        scan(c)
        if c >= 1:
            g_parts.append(out_proj(c - 1))
    g_parts.append(out_proj(n_chunks - 1))
    g = jnp.concatenate(g_parts, axis=-1)
    z = jnp.concatenate(z_parts, axis=-1)
    y = g * _sigmoid(_dot(g.astype(BF16), w_glu_ref[...]) + b_glu_ref[...])
    y = (y * _silu(z)).astype(BF16)
    out = _dot(y, w_out_ref[...]).reshape(ts, B, D)
    o_ref[...] = jnp.swapaxes(x3 + gate_ref[...][None] * out, 0, 1)


def _s5_layer(x, mod, w_in, bbr, bbi, ar, ai, cr, ci, d_skip, w_glu, b_glu, w_out, ts=64, scan_lanes=512):
    B, S, D = x.shape
    E = w_glu.shape[0]
    N = ar.shape[1]
    mod_specs = [pl.BlockSpec((B, D), functools.partial(lambda i, k: (0, k), k=k)) for k in range(3)]
    weights = [w_in, bbr, bbi, ar, ai, cr, ci, d_skip.reshape(1, E), w_glu, b_glu.reshape(1, E), w_out]
    return pl.pallas_call(
        functools.partial(_s5_kernel, scan_lanes=scan_lanes),
        grid=(S // ts,),
        in_specs=[pl.BlockSpec((B, ts, D), lambda i: (0, i, 0))] + mod_specs
                 + [_resident(w.shape) for w in weights],
        out_specs=pl.BlockSpec((B, ts, D), lambda i: (0, i, 0)),
        out_shape=jax.ShapeDtypeStruct((B, S, D), F32),
        scratch_shapes=[pltpu.VMEM((2, B, N), F32),
                        pltpu.VMEM((bbr.shape[0], ts * B, bbr.shape[1]), F32),
                        pltpu.VMEM((bbr.shape[0], ts * B, bbr.shape[1]), F32)],
        compiler_params=_cparams(("arbitrary",)),
        name="s5_layer",
    )(x, mod, mod, mod, *weights)


def _s5_params(log_dt, lambda_re, lambda_im, b_re, b_im, c_re, c_im, B, groups_per_chunk=16):
    G, P = lambda_re.shape
    K = b_re.shape[2]
    dt = jnp.exp(log_dt.astype(F32))[:, None]
    lr = lambda_re.astype(F32)
    li = lambda_im.astype(F32)
    mag = jnp.exp(lr * dt)
    ar = mag * jnp.cos(li * dt)
    ai = mag * jnp.sin(li * dt)
    inv = 1.0 / (lr * lr + li * li)
    qr = ((ar - 1.0) * lr + ai * li) * inv
    qi = (ai * lr - (ar - 1.0) * li) * inv
    br = b_re.astype(F32)
    bi = b_im.astype(F32)
    bbr = qr[..., None] * br - qi[..., None] * bi
    bbi = qr[..., None] * bi + qi[..., None] * br
    gc = groups_per_chunk
    nc = G // gc

    def block_diag(m):
        a, b = m.shape[1:]
        rows = m.astype(BF16).reshape(nc, gc * a, b)
        on_diag = np.kron(np.eye(gc, dtype=bool), np.ones((a, b), dtype=bool))
        return jnp.where(on_diag, jnp.tile(rows, (1, 1, gc)), jnp.zeros((), BF16))

    bcast = lambda a: jnp.broadcast_to(a.reshape(1, G * P), (B, G * P))
    return (block_diag(bbr), block_diag(bbi), bcast(ar), bcast(ai),
            block_diag(c_re.astype(F32)), block_diag(c_im.astype(F32)))


def kernel(x, c, rel_bias, final_g, l0_ada_w, l0_ada_b, l0_w_in, l0_conv_w, l0_conv_b, l0_w_out, l1_ada_w, l1_ada_b, l1_w_in, l1_w_out, l2_ada_w, l2_ada_b, l2_w_in, l2_log_dt, l2_lambda_re, l2_lambda_im, l2_b_re, l2_b_im, l2_c_re, l2_c_im, l2_d_skip, l2_w_glu, l2_b_glu, l2_w_out, l3_ada_w, l3_ada_b, l3_w_in, l3_conv_w, l3_conv_b, l3_w_out):
    B, S, D = x.shape
    assert B == SUBLANES, "the S5 scan keeps the batch on the sublane axis"
    mods = _modulation(c, (l0_ada_w, l1_ada_w, l2_ada_w, l3_ada_w), (l0_ada_b, l1_ada_b, l2_ada_b, l3_ada_b))
    mod3 = [m.reshape(B, 1, 3 * D) for m in mods]

    x, w1_in, w1_out = _conv_layer(x, mod3[0], l0_w_in.astype(BF16), l0_conv_w, l0_conv_b,
                                   l0_w_out.astype(BF16), cast=(l1_w_in, l1_w_out))

    dils = tuple(dil for _, dil in DILATION_PATTERNS)
    n_qkv = 3 * len(dils)
    E = l1_w_out.shape[0]
    H = E // HEAD_DIM
    res = _qkv_proj(x, mod3[1], w1_in, dils, E, cast=(l2_w_in, l2_w_glu, l2_w_out, l3_w_in, l3_w_out))
    qkv, (w2_in, w2_glu, w2_out, w3_in, w3_out) = res[:n_qkv], res[n_qkv:]
    os_, ms, dens = [], [], []
    for g, (window, dil) in enumerate(DILATION_PATTERNS):
        o, m, den = _attention(*qkv[3 * g:3 * g + 3], rel_bias[:, g * H:(g + 1) * H], window, dil)
        os_.append(o)
        ms.append(m)
        dens.append(den)
    x = _attn_out(x, mod3[1], os_, ms, dens, w1_in, w1_out)

    ssm = _s5_params(l2_log_dt, l2_lambda_re, l2_lambda_im, l2_b_re, l2_b_im, l2_c_re, l2_c_im, B)
    x = _s5_layer(x, mods[2], w2_in, *ssm, l2_d_skip, w2_glu, l2_b_glu, w2_out)

    return _conv_layer(x, mod3[3], w3_in, l3_conv_w, l3_conv_b, w3_out, final_g=final_g)
```

```python
import functools
import math

import numpy as np
import jax
import jax.numpy as jnp
from jax import lax
from jax.experimental import pallas as pl
from jax.experimental.pallas import tpu as pltpu

EPS = 1e-6
HEAD_DIM = 64
ATT_BLOCK = 128
DILATION_PATTERNS = ((128, 1), (512, 4), (2048, 16))
REL_BUCKETS = 32
REL_MAX_DIST = 2048
NEG_INF = -1e30
LOG2E = math.log2(math.e)
LANES = 128
SUBLANES = 8
VMEM_LIMIT_BYTES = 60 * 1024 * 1024

F32 = jnp.float32
BF16 = jnp.bfloat16


def _cparams(semantics):
    return pltpu.CompilerParams(dimension_semantics=semantics, vmem_limit_bytes=VMEM_LIMIT_BYTES)


def _resident(shape):
    nd = len(shape)
    return pl.BlockSpec(shape, lambda *_: (0,) * nd, pipeline_mode=pl.Buffered(1))


def _silu(z):
    return z * (1.0 / (1.0 + jnp.exp(-z)))


def _sigmoid(z):
    return 1.0 / (1.0 + jnp.exp(-z))


def _mod_norm(x, scale, shift):
    ms = jnp.mean(x * x, axis=-1, keepdims=True)
    return (x * lax.rsqrt(ms + EPS)) * (1.0 + scale) + shift


def _dot(a, b):
    return jnp.dot(a, b, preferred_element_type=F32)


def _rows(start, size, stride):
    return pl.ds(start, size) if stride == 1 else pl.ds(start, size, stride=stride)


def _modulation_kernel(c_ref, w0, w1, w2, w3, b0, b1, b2, b3, o0, o1, o2, o3):
    sc = _silu(c_ref[...]).astype(BF16)
    for w, b, o in ((w0, b0, o0), (w1, b1, o1), (w2, b2, o2), (w3, b3, o3)):
        o[...] = _dot(sc, w[...].astype(BF16)) + b[...]


def _modulation(c, ws, bs, tn=512):
    B, D = c.shape
    N = ws[0].shape[1]
    w_spec = pl.BlockSpec((D, tn), lambda j: (0, j))
    b_spec = pl.BlockSpec((1, tn), lambda j: (0, j))
    o_spec = pl.BlockSpec((B, tn), lambda j: (0, j))
    return pl.pallas_call(
        _modulation_kernel,
        grid=(N // tn,),
        in_specs=[pl.BlockSpec((B, D), lambda j: (0, 0))] + [w_spec] * 4 + [b_spec] * 4,
        out_specs=[o_spec] * 4,
        out_shape=[jax.ShapeDtypeStruct((B, N), F32)] * 4,
        compiler_params=_cparams(("arbitrary",)),
        name="modulation",
    )(c, *ws, *[b.reshape(1, N) for b in bs])


def _cast_specs(arrays, n_steps, step_of):
    in_specs, out_specs, out_shape = [], [], []
    for a in arrays:
        rows, cols = a.shape
        spec = pl.BlockSpec((rows // n_steps, cols), lambda *ids: (step_of(*ids), 0))
        in_specs.append(spec)
        out_specs.append(spec)
        out_shape.append(jax.ShapeDtypeStruct(a.shape, BF16))
    return in_specs, out_specs, out_shape


def _cast_slabs(src_refs, dst_refs):
    for src, dst in zip(src_refs, dst_refs):
        dst[...] = src[...].astype(BF16)


def _conv_layer_kernel(*refs, ec, final, n_cast):
    n_in = 9 if final else 8
    (x_ref, shift_ref, scale_ref, gate_ref, w_in_ref, cw_ref, cb_ref, w_out_ref) = refs[:8]
    g_ref = refs[8] if final else None
    cast_src = refs[n_in:n_in + n_cast]
    o_ref = refs[n_in + n_cast]
    cast_dst = refs[n_in + n_cast + 1:n_in + 2 * n_cast + 1]
    carry_ref, vs_ref = refs[n_in + 2 * n_cast + 1:]
    tm, D = x_ref.shape
    E = w_out_ref.shape[0]

    @pl.when(pl.program_id(1) == 0)
    def _():
        carry_ref[...] = jnp.zeros_like(carry_ref)

    _cast_slabs(cast_src, cast_dst)
    x = x_ref[...]
    h = _mod_norm(x, scale_ref[...], shift_ref[...]).astype(BF16)
    n_chunks = E // ec

    def in_proj(j):
        c0 = j * ec
        return tuple(_dot(h, w_in_ref[:, k * E + c0:k * E + c0 + ec]) for k in range(4))

    def gated_conv(j, proj):
        u, gc, gb, z = proj
        c0 = j * ec
        v = gc * u
        vs_ref[0:SUBLANES, :] = carry_ref[j]
        vs_ref[SUBLANES:SUBLANES + tm, :] = v
        carry_ref[j] = v[tm - SUBLANES:tm, :]
        v1 = vs_ref[SUBLANES - 1:SUBLANES - 1 + tm, :]
        v2 = vs_ref[SUBLANES - 2:SUBLANES - 2 + tm, :]
        cw = cw_ref[:, c0:c0 + ec]
        conv = cw[0:1, :] * v2 + cw[1:2, :] * v1 + cw[2:3, :] * v + cb_ref[:, c0:c0 + ec]
        return (gb * conv * _silu(z)).astype(BF16)

    acc = jnp.zeros((tm, D), F32)
    proj = in_proj(0)
    for j in range(n_chunks):
        nxt = in_proj(j + 1) if j + 1 < n_chunks else None
        y = gated_conv(j, proj)
        acc = acc + _dot(y, w_out_ref[j * ec:(j + 1) * ec, :])
        proj = nxt
    out = x + gate_ref[...] * acc
    if final:
        ms = jnp.mean(out * out, axis=-1, keepdims=True)
        out = out * lax.rsqrt(ms + EPS) * g_ref[...]
    o_ref[...] = out


def _conv_layer(x, mod, w_in, conv_w, conv_b, w_out, *, final_g=None, cast=(), tm=1024, ec=256):
    B, S, D = x.shape
    x_spec = pl.BlockSpec((None, tm, D), lambda b, i: (b, i, 0))
    E = w_out.shape[0]
    final = final_g is not None
    mod_specs = [pl.BlockSpec((None, 1, D), functools.partial(lambda b, i, k: (b, 0, k), k=k))
                 for k in range(3)]
    in_specs = [x_spec] + mod_specs + [_resident(w_in.shape), _resident(conv_w.shape),
                                       _resident((1, E)), _resident(w_out.shape)]
    args = [x, mod, mod, mod, w_in, conv_w, conv_b.reshape(1, E), w_out]
    if final:
        in_specs.append(_resident((1, D)))
        args.append(final_g.reshape(1, D))
    n_tiles = S // tm
    cast_in, cast_out, cast_shape = _cast_specs(cast, B * n_tiles, lambda b, i: b * n_tiles + i)
    res = pl.pallas_call(
        functools.partial(_conv_layer_kernel, ec=ec, final=final, n_cast=len(cast)),
        grid=(B, n_tiles),
        in_specs=in_specs + cast_in,
        out_specs=[pl.BlockSpec((None, tm, D), lambda b, i: (b, i, 0))] + cast_out,
        out_shape=[jax.ShapeDtypeStruct((B, S, D), F32)] + cast_shape,
        scratch_shapes=[pltpu.VMEM((E // ec, SUBLANES, ec), F32),
                        pltpu.VMEM((SUBLANES + tm, ec), F32)],
        compiler_params=_cparams(("arbitrary", "arbitrary")),
        name="conv_layer_final" if final else "conv_layer",
    )(*args, *cast)
    return res if cast else res[0]


def _t5_bucket(dist):
    max_exact = REL_BUCKETS // 2
    d = np.maximum(dist, 0)
    ratio = np.log(np.maximum(d, 1) / max_exact) / math.log(REL_MAX_DIST / max_exact)
    large = np.minimum(max_exact + (ratio * (REL_BUCKETS - max_exact)).astype(np.int64), REL_BUCKETS - 1)
    return np.where(d < max_exact, d, large).astype(np.int32)


def _band_structure(window, dil):
    wsub = window // dil
    qi = np.arange(ATT_BLOCK)[:, None]
    ki = np.arange(2 * ATT_BLOCK)[None, :]
    dist = ATT_BLOCK + qi - ki
    in_band = (dist >= 0) & (dist <= wsub)
    first = in_band & (ki >= ATT_BLOCK)
    return _t5_bucket(dist * dil), np.stack([in_band, first]).astype(np.float32)


def _qkv_kernel(x_ref, shift_ref, scale_ref, w_ref, *refs, dils, n_cast):
    n_out = 3 * len(dils)
    cast_src, out_refs = refs[:n_cast], refs[n_cast:n_cast + n_out]
    cast_dst, h_ref = refs[n_cast + n_out:-1], refs[-1]
    tn, D = x_ref.shape
    E = out_refs[0].shape[-1]
    _cast_slabs(cast_src, cast_dst)
    h = _mod_norm(x_ref[...], scale_ref[...], shift_ref[...])
    n_col = D // LANES
    for c in range(n_col):
        h_ref[c] = h[:, c * LANES:(c + 1) * LANES]
    for g, dil in enumerate(dils):
        n = tn // dil
        if dil == 1:
            hp = h
        else:
            hp = jnp.concatenate(
                [jnp.concatenate([h_ref[c, pl.ds(r, n, stride=dil), :] for c in range(n_col)], axis=1)
                 for r in range(dil)], axis=0)
        hp = hp.astype(BF16)
        for j in range(3):
            c0 = (3 * g + j) * E
            res = _dot(hp, w_ref[:, c0:c0 + E])
            if j == 0:
                res = res * (LOG2E / math.sqrt(HEAD_DIM))
            out_refs[3 * g + j][...] = res.astype(BF16).reshape(dil, n, E)


def _qkv_proj(x, mod, w_qkv, dils, E, cast=(), tn=512):
    B, S, D = x.shape
    n_tiles = S // tn
    cast_in, cast_out, cast_shape = _cast_specs(cast, B * n_tiles, lambda b, i: b * n_tiles + i)
    mod_specs = [pl.BlockSpec((None, 1, D), functools.partial(lambda b, i, k: (b, 0, k), k=k))
                 for k in range(2)]
    out_specs, out_shape = [], []
    for dil in dils:
        out_specs += [pl.BlockSpec((None, dil, tn // dil, E), lambda b, i: (b, 0, i, 0))] * 3
        out_shape += [jax.ShapeDtypeStruct((B, dil, S // dil, E), BF16)] * 3
    return pl.pallas_call(
        functools.partial(_qkv_kernel, dils=dils, n_cast=len(cast)),
        grid=(B, n_tiles),
        in_specs=[pl.BlockSpec((None, tn, D), lambda b, i: (b, i, 0))] + mod_specs
                 + [_resident(w_qkv.shape)] + cast_in,
        out_specs=out_specs + cast_out,
        out_shape=out_shape + cast_shape,
        scratch_shapes=[pltpu.VMEM((D // LANES, tn, LANES), F32)],
        compiler_params=_cparams(("arbitrary", "arbitrary")),
        name="qkv_proj",
    )(x, mod, mod, w_qkv, *cast)


def _attn_kernel(*refs, nr, nsub, has_prev, stride, n_heads):
    if has_prev:
        (q_ref, k_ref, v_ref, kp_ref, vp_ref, rb_ref, bucket_ref, mask_ref,
         o_ref, m_ref, den_ref, tbl_ref, s_ref, p_ref, stat_ref) = refs
    else:
        (q_ref, k_ref, v_ref, rb_ref, bucket_ref, mask_ref,
         o_ref, m_ref, den_ref, tbl_ref, s_ref, p_ref, stat_ref) = refs
    n_pairs = n_heads // 2
    pair_rows = 2 * ATT_BLOCK
    nk = s_ref.shape[1]
    step = pl.program_id(1)

    @pl.when((pl.program_id(0) == 0) & (step == 0))
    def _():
        def fill(b, carry):
            hit = bucket_ref[...] == b
            for h in range(n_heads):
                rows = slice(h * ATT_BLOCK, (h + 1) * ATT_BLOCK)
                s_ref[rows, :] = jnp.where(hit, rb_ref[b, h] * LOG2E, s_ref[rows, :])
            return carry
        s_ref[...] = jnp.zeros_like(s_ref)
        lax.fori_loop(0, REL_BUCKETS, fill, 0)
        for t in range(tbl_ref.shape[0]):
            for h in range(n_heads):
                rows = slice(h * ATT_BLOCK, (h + 1) * ATT_BLOCK)
                tbl_ref[t, rows, :] = jnp.where(mask_ref[t] > 0.0, s_ref[rows, :], NEG_INF)
        stat_ref[0] = jnp.zeros((ATT_BLOCK, LANES), F32)
        stat_ref[1] = jnp.ones((ATT_BLOCK, LANES), F32)

    lane = lax.broadcasted_iota(jnp.int32, (ATT_BLOCK, LANES), 1)
    lo = lane < HEAD_DIM

    def block(q_of, k_of, v_of, table, out_rows):
        for hp in range(n_pairs):
            cols = slice(hp * LANES, (hp + 1) * LANES)
            rows = slice(hp * pair_rows, (hp + 1) * pair_rows)
            q2 = q_of(cols)
            zero = jnp.zeros_like(q2)
            q_pair = jnp.concatenate([jnp.where(lo, q2, zero), jnp.where(lo, zero, q2)], axis=0)
            s_ref[rows, :] = lax.dot_general(q_pair, k_of(cols), (((1,), (1,)), ((), ())),
                                             preferred_element_type=F32)
        head_rows = [slice(h * ATT_BLOCK, (h + 1) * ATT_BLOCK) for h in range(n_heads)]
        if not has_prev:
            maxima = []
            for rows in head_rows:
                s = s_ref[rows, :] + tbl_ref[table, rows, :]
                s_ref[rows, :] = s
                maxima.append(jnp.max(s, axis=-1, keepdims=True))
        for h, rows in enumerate(head_rows):
            if has_prev:
                s = s_ref[rows, :] + tbl_ref[table, rows, :]
                m = jnp.max(s, axis=-1, keepdims=True)
            else:
                s, m = s_ref[rows, :], maxima[h]
            p = jnp.exp2(s - m)
            den = jnp.sum(p, axis=-1, keepdims=True)
            p_ref[rows, :] = p.astype(BF16)
            stat_ref[0, :, h:h + 1] = m
            stat_ref[1, :, h:h + 1] = den
        m_ref[out_rows, :] = stat_ref[0]
        den_ref[out_rows, :] = stat_ref[1]
        for hp in range(n_pairs):
            cols = slice(hp * LANES, (hp + 1) * LANES)
            rows = slice(hp * pair_rows, (hp + 1) * pair_rows)
            res = _dot(p_ref[rows, :], v_of(cols))
            o_pair = jnp.where(lo, res[:ATT_BLOCK], res[ATT_BLOCK:])
            bits = lax.bitcast_convert_type(o_pair.astype(BF16).astype(F32), jnp.uint32)
            if hp % 2 == 0:
                high = bits
            else:
                o_ref[hp // 2, out_rows, :] = high | (bits >> 16)

    for r in range(nr):
        for j in range(nsub):
            row = j * ATT_BLOCK
            q_of = lambda cols, r=r, row=row: q_ref[r, row:row + ATT_BLOCK, cols]
            if has_prev:
                out_rows = _rows(row * stride + r, ATT_BLOCK, stride)
                if j == 0:
                    k_of = lambda cols, r=r: jnp.concatenate([kp_ref[r, :, cols], k_ref[r, 0:ATT_BLOCK, cols]], axis=0)
                    v_of = lambda cols, r=r: jnp.concatenate([vp_ref[r, :, cols], v_ref[r, 0:ATT_BLOCK, cols]], axis=0)
                    table = jnp.where(step == 0, 1, 0)
                else:
                    k_of = lambda cols, r=r, row=row: k_ref[r, row - ATT_BLOCK:row + ATT_BLOCK, cols]
                    v_of = lambda cols, r=r, row=row: v_ref[r, row - ATT_BLOCK:row + ATT_BLOCK, cols]
                    table = 0
            else:
                out_rows = _rows(step * nr + r, ATT_BLOCK, stride)
                k_of = lambda cols, r=r: k_ref[r, :, cols]
                v_of = lambda cols, r=r: v_ref[r, :, cols]
                table = 0
            block(q_of, k_of, v_of, table, out_rows)


def _attention(q, k, v, rel_bias_g, window, dil, tokens_per_step=1024):
    B, _, L, E = q.shape
    S = L * dil
    H = E // HEAD_DIM
    bucket, masks = _band_structure(window, dil)
    has_prev = L > ATT_BLOCK
    if has_prev:
        nr = dil
        nsub = tokens_per_step // (dil * ATT_BLOCK)
        tq = nsub * ATT_BLOCK
        n_steps = L // tq
        nk = 2 * ATT_BLOCK
        qkv_spec = pl.BlockSpec((None, nr, tq, E), lambda b, i: (b, 0, i, 0))
        prev_spec = pl.BlockSpec((None, nr, ATT_BLOCK, E),
                                 lambda b, i: (b, 0, jnp.maximum(i * nsub - 1, 0), 0))
        in_specs = [qkv_spec] * 3 + [prev_spec] * 2
        args = [q, k, v, k, v]
        out_block_rows = tokens_per_step
        out_step = lambda i: i
    else:
        nr = tokens_per_step // ATT_BLOCK
        nsub = 1
        n_steps = dil // nr
        nk = ATT_BLOCK
        bucket, masks = bucket[:, ATT_BLOCK:], masks[1:, :, ATT_BLOCK:]
        qkv_spec = pl.BlockSpec((None, nr, ATT_BLOCK, E), lambda b, i: (b, i, 0, 0))
        in_specs = [qkv_spec] * 3
        args = [q, k, v]
        out_block_rows = S
        out_step = lambda i: 0
    in_specs += [pl.BlockSpec(memory_space=pltpu.SMEM), _resident(bucket.shape), _resident(masks.shape)]
    args += [rel_bias_g, jnp.asarray(bucket), jnp.asarray(masks)]
    stat_spec = pl.BlockSpec((None, out_block_rows, LANES), lambda b, i: (b, out_step(i), 0))
    stat_shape = jax.ShapeDtypeStruct((B, S, LANES), F32)
    return pl.pallas_call(
        functools.partial(_attn_kernel, nr=nr, nsub=nsub, has_prev=has_prev, stride=dil, n_heads=H),
        grid=(B, n_steps),
        in_specs=in_specs,
        out_specs=[pl.BlockSpec((None, E // LANES // 2, out_block_rows, LANES),
                                lambda b, i: (b, 0, out_step(i), 0)),
                   stat_spec, stat_spec],
        out_shape=[jax.ShapeDtypeStruct((B, E // LANES // 2, S, LANES), jnp.uint32),
                   stat_shape, stat_shape],
        scratch_shapes=[pltpu.VMEM((masks.shape[0], H * ATT_BLOCK, nk), F32),
                        pltpu.VMEM((H * ATT_BLOCK, nk), F32),
                        pltpu.VMEM((H * ATT_BLOCK, nk), BF16),
                        pltpu.VMEM((2, ATT_BLOCK, LANES), F32)],
        compiler_params=_cparams(("arbitrary", "arbitrary")),
        name=f"attention_d{dil}",
    )(*args)


def _attn_out_kernel(x_ref, shift_ref, scale_ref, gate_ref, o0_ref, o1_ref, o2_ref,
                     m0_ref, m1_ref, m2_ref, d0_ref, d1_ref, d2_ref, wz_ref, w_out_ref, expand_ref,
                     out_ref):
    tm, D = x_ref.shape
    E = wz_ref.shape[1]
    n_heads = E // HEAD_DIM
    x = x_ref[...]
    h = _mod_norm(x, scale_ref[...], shift_ref[...]).astype(BF16)
    z = _dot(h, wz_ref[...])
    m0, m1, m2 = m0_ref[...], m1_ref[...], m2_ref[...]
    m = jnp.maximum(jnp.maximum(m0, m1), m2)
    e0, e1, e2 = jnp.exp2(m0 - m), jnp.exp2(m1 - m), jnp.exp2(m2 - m)
    den = e0 * d0_ref[...] + e1 * d1_ref[...] + e2 * d2_ref[...]
    head_lane = lax.broadcasted_iota(jnp.int32, (tm, LANES), 1) < n_heads
    packed = None
    for g, e in enumerate((e0, e1, e2)):
        w = jnp.where(head_lane, e / den, 0.0)
        hi = w.astype(BF16).astype(F32)
        for part, val in enumerate((hi, w - hi)):
            shift = (2 * g + part) * n_heads
            val = pltpu.roll(val, shift, axis=1) if shift else val
            packed = val if packed is None else packed + val
    packed = packed.astype(BF16)
    o = None
    for g, o_ref in enumerate((o0_ref, o1_ref, o2_ref)):
        wexp = _dot(packed, expand_ref[:, g * E:(g + 1) * E])
        parts = []
        for w in range(o_ref.shape[0]):
            word = o_ref[w]
            parts.append(lax.bitcast_convert_type(word & jnp.uint32(0xFFFF0000), F32))
            parts.append(lax.bitcast_convert_type(word << 16, F32))
        og = jnp.concatenate(parts, axis=-1)
        term = wexp * og
        o = term if o is None else o + term
    y = (o * _silu(z)).astype(BF16)
    out_ref[...] = x + gate_ref[...] * _dot(y, w_out_ref[...])


def _attn_out(x, mod, os_, ms, dens, w_in, w_out, tm=1024):
    B, S, D = x.shape
    E = w_out.shape[0]
    z_block = w_in.shape[1] // E - 1
    tile = lambda w: pl.BlockSpec((None, tm, w), lambda b, i: (b, i, 0))
    mod_specs = [pl.BlockSpec((None, 1, D), functools.partial(lambda b, i, k: (b, 0, k), k=k))
                 for k in range(3)]
    n_groups, H = len(os_), E // HEAD_DIM
    expand = np.zeros((LANES, n_groups * E), np.float32)
    for g in range(n_groups):
        for part in range(2):
            for hd in range(H):
                expand[(2 * g + part) * H + hd, g * E + hd * HEAD_DIM:g * E + (hd + 1) * HEAD_DIM] = 1.0
    expand = jnp.asarray(expand, BF16)
    return pl.pallas_call(
        _attn_out_kernel,
        grid=(B, S // tm),
        in_specs=[tile(D)] + mod_specs
                 + [pl.BlockSpec((None, E // LANES // 2, tm, LANES), lambda b, i: (b, 0, i, 0))] * 3
                 + [tile(LANES)] * 6
                 + [pl.BlockSpec((D, E), lambda b, i: (0, z_block), pipeline_mode=pl.Buffered(1)),
                    _resident(w_out.shape), _resident(expand.shape)],
        out_specs=tile(D),
        out_shape=jax.ShapeDtypeStruct((B, S, D), F32),
        compiler_params=_cparams(("arbitrary", "arbitrary")),
        name="attention_out",
    )(x, mod, mod, mod, *os_, *ms, *dens, w_in, w_out, expand)


def _s5_kernel(x_ref, shift_ref, scale_ref, gate_ref, w_in_ref, bbr_ref, bbi_ref, ar_ref, ai_ref,
               cr_ref, ci_ref, dskip_ref, w_glu_ref, b_glu_ref, w_out_ref,
               o_ref, state_ref, sr_ref, si_ref, *, scan_lanes):
    B, ts, D = x_ref.shape
    R = ts * B
    n_chunks, cn, ck = bbr_ref.shape
    E = n_chunks * ck
    dot_t = lambda a, b: lax.dot_general(a, b, (((1,), (1,)), ((), ())), preferred_element_type=F32)

    @pl.when(pl.program_id(0) == 0)
    def _():
        state_ref[...] = jnp.zeros_like(state_ref)

    x3 = jnp.swapaxes(x_ref[...], 0, 1)
    ms = jnp.mean(x3 * x3, axis=-1, keepdims=True)
    h3 = (x3 * lax.rsqrt(ms + EPS)) * (1.0 + scale_ref[...][None]) + shift_ref[...][None]
    h = h3.reshape(R, D).astype(BF16)
    u_parts = [None] * n_chunks
    z_parts = [None] * n_chunks

    def w_in_cols(first):
        return _dot(h, w_in_ref[:, first:first + ck])

    def in_proj(c):
        u_parts[c] = w_in_cols(c * ck)
        uc = u_parts[c].astype(BF16)
        sr_ref[c] = dot_t(uc, bbr_ref[c])
        si_ref[c] = dot_t(uc, bbi_ref[c])

    def scan(c):
        for l0 in range(0, cn, scan_lanes):
            ls = slice(l0, l0 + scan_lanes)
            gl = slice(c * cn + l0, c * cn + l0 + scan_lanes)
            ar, ai = ar_ref[:, gl], ai_ref[:, gl]
            pr, pi = state_ref[0, :, gl], state_ref[1, :, gl]
            for t in range(ts):
                rows = slice(t * B, (t + 1) * B)
                nr = ar * pr - ai * pi + sr_ref[c, rows, ls]
                ni = ar * pi + ai * pr + si_ref[c, rows, ls]
                sr_ref[c, rows, ls] = nr
                si_ref[c, rows, ls] = ni
                pr, pi = nr, ni
            state_ref[0, :, gl] = pr
            state_ref[1, :, gl] = pi

    def out_proj(c):
        cols = slice(c * ck, (c + 1) * ck)
        y = (dot_t(sr_ref[c].astype(BF16), cr_ref[c]) - dot_t(si_ref[c].astype(BF16), ci_ref[c])
             + dskip_ref[:, cols] * u_parts[c])
        z_parts[c] = w_in_cols(E + c * ck)
        return y * (0.5 * (1.0 + jnp.tanh(math.sqrt(2.0 / math.pi) * (y + 0.044715 * (y * y * y)))))

    g_parts = []
    in_proj(0)
    for c in range(n_chunks):
        if c + 1 < n_chunks:
            in_proj(c + 1)
        scan(c)
        if c >= 1:
            g_parts.append(out_proj(c - 1))
    g_parts.append(out_proj(n_chunks - 1))
    g = jnp.concatenate(g_parts, axis=-1)
    z = jnp.concatenate(z_parts, axis=-1)
    y = g * _sigmoid(_dot(g.astype(BF16), w_glu_ref[...]) + b_glu_ref[...])
    y = (y * _silu(z)).astype(BF16)
    out = _dot(y, w_out_ref[...]).reshape(ts, B, D)
    o_ref[...] = jnp.swapaxes(x3 + gate_ref[...][None] * out, 0, 1)


def _s5_layer(x, mod, w_in, bbr, bbi, ar, ai, cr, ci, d_skip, w_glu, b_glu, w_out, ts=64, scan_lanes=512):
    B, S, D = x.shape
    E = w_glu.shape[0]
    N = ar.shape[1]
    mod_specs = [pl.BlockSpec((B, D), functools.partial(lambda i, k: (0, k), k=k)) for k in range(3)]
    weights = [w_in, bbr, bbi, ar, ai, cr, ci, d_skip.reshape(1, E), w_glu, b_glu.reshape(1, E), w_out]
    return pl.pallas_call(
        functools.partial(_s5_kernel, scan_lanes=scan_lanes),
        grid=(S // ts,),
        in_specs=[pl.BlockSpec((B, ts, D), lambda i: (0, i, 0))] + mod_specs
                 + [_resident(w.shape) for w in weights],
        out_specs=pl.BlockSpec((B, ts, D), lambda i: (0, i, 0)),
        out_shape=jax.ShapeDtypeStruct((B, S, D), F32),
        scratch_shapes=[pltpu.VMEM((2, B, N), F32),
                        pltpu.VMEM((bbr.shape[0], ts * B, bbr.shape[1]), F32),
                        pltpu.VMEM((bbr.shape[0], ts * B, bbr.shape[1]), F32)],
        compiler_params=_cparams(("arbitrary",)),
        name="s5_layer",
    )(x, mod, mod, mod, *weights)


def _s5_params(log_dt, lambda_re, lambda_im, b_re, b_im, c_re, c_im, B, groups_per_chunk=16):
    G, P = lambda_re.shape
    K = b_re.shape[2]
    dt = jnp.exp(log_dt.astype(F32))[:, None]
    lr = lambda_re.astype(F32)
    li = lambda_im.astype(F32)
    mag = jnp.exp(lr * dt)
    ar = mag * jnp.cos(li * dt)
    ai = mag * jnp.sin(li * dt)
    inv = 1.0 / (lr * lr + li * li)
    qr = ((ar - 1.0) * lr + ai * li) * inv
    qi = (ai * lr - (ar - 1.0) * li) * inv
    br = b_re.astype(F32)
    bi = b_im.astype(F32)
    bbr = qr[..., None] * br - qi[..., None] * bi
    bbi = qr[..., None] * bi + qi[..., None] * br
    gc = groups_per_chunk
    nc = G // gc

    def block_diag(m):
        a, b = m.shape[1:]
        rows = m.astype(BF16).reshape(nc, gc * a, b)
        on_diag = np.kron(np.eye(gc, dtype=bool), np.ones((a, b), dtype=bool))
        return jnp.where(on_diag, jnp.tile(rows, (1, 1, gc)), jnp.zeros((), BF16))

    bcast = lambda a: jnp.broadcast_to(a.reshape(1, G * P), (B, G * P))
    return (block_diag(bbr), block_diag(bbi), bcast(ar), bcast(ai),
            block_diag(c_re.astype(F32)), block_diag(c_im.astype(F32)))


def kernel(x, c, rel_bias, final_g, l0_ada_w, l0_ada_b, l0_w_in, l0_conv_w, l0_conv_b, l0_w_out, l1_ada_w, l1_ada_b, l1_w_in, l1_w_out, l2_ada_w, l2_ada_b, l2_w_in, l2_log_dt, l2_lambda_re, l2_lambda_im, l2_b_re, l2_b_im, l2_c_re, l2_c_im, l2_d_skip, l2_w_glu, l2_b_glu, l2_w_out, l3_ada_w, l3_ada_b, l3_w_in, l3_conv_w, l3_conv_b, l3_w_out):
    B, S, D = x.shape
    assert B == SUBLANES, "the S5 scan keeps the batch on the sublane axis"
    mods = _modulation(c, (l0_ada_w, l1_ada_w, l2_ada_w, l3_ada_w), (l0_ada_b, l1_ada_b, l2_ada_b, l3_ada_b))
    mod3 = [m.reshape(B, 1, 3 * D) for m in mods]

    x, w1_in, w1_out = _conv_layer(x, mod3[0], l0_w_in.astype(BF16), l0_conv_w, l0_conv_b,
                                   l0_w_out.astype(BF16), cast=(l1_w_in, l1_w_out))

    dils = tuple(dil for _, dil in DILATION_PATTERNS)
    n_qkv = 3 * len(dils)
    E = l1_w_out.shape[0]
    H = E // HEAD_DIM
    res = _qkv_proj(x, mod3[1], w1_in, dils, E, cast=(l2_w_in, l2_w_glu, l2_w_out, l3_w_in, l3_w_out))
    qkv, (w2_in, w2_glu, w2_out, w3_in, w3_out) = res[:n_qkv], res[n_qkv:]
    os_, ms, dens = [], [], []
    for g, (window, dil) in enumerate(DILATION_PATTERNS):
        o, m, den = _attention(*qkv[3 * g:3 * g + 3], rel_bias[:, g * H:(g + 1) * H], window, dil)
        os_.append(o)
        ms.append(m)
        dens.append(den)
    x = _attn_out(x, mod3[1], os_, ms, dens, w1_in, w1_out)

    ssm = _s5_params(l2_log_dt, l2_lambda_re, l2_lambda_im, l2_b_re, l2_b_im, l2_c_re, l2_c_im, B)
    x = _s5_layer(x, mods[2], w2_in, *ssm, l2_d_skip, w2_glu, l2_b_glu, w2_out)

    return _conv_layer(x, mod3[3], w3_in, l3_conv_w, l3_conv_b, w3_out, final_g=final_g)
```

```python
import functools
import math

import numpy as np
import jax
import jax.numpy as jnp
from jax import lax
from jax.experimental import pallas as pl
from jax.experimental.pallas import tpu as pltpu

EPS = 1e-6
HEAD_DIM = 64
ATT_BLOCK = 128
DILATION_PATTERNS = ((128, 1), (512, 4), (2048, 16))
REL_BUCKETS = 32
REL_MAX_DIST = 2048
NEG_INF = -1e30
LOG2E = math.log2(math.e)
LANES = 128
SUBLANES = 8
RING_SLOTS = 3
VMEM_LIMIT_BYTES = 60 * 1024 * 1024

F32 = jnp.float32
BF16 = jnp.bfloat16


def _cparams(semantics):
    return pltpu.CompilerParams(dimension_semantics=semantics, vmem_limit_bytes=VMEM_LIMIT_BYTES)


def _resident(shape):
    nd = len(shape)
    return pl.BlockSpec(shape, lambda *_: (0,) * nd, pipeline_mode=pl.Buffered(1))


def _silu(z):
    return z * (1.0 / (1.0 + jnp.exp(-z)))


def _sigmoid(z):
    return 1.0 / (1.0 + jnp.exp(-z))


def _mod_norm(x, scale, shift):
    ms = jnp.mean(x * x, axis=-1, keepdims=True)
    return (x * lax.rsqrt(ms + EPS)) * (1.0 + scale) + shift


def _dot(a, b):
    return jnp.dot(a, b, preferred_element_type=F32)


def _rows(start, size, stride):
    return pl.ds(start, size) if stride == 1 else pl.ds(start, size, stride=stride)


def _modulation_kernel(c_ref, w0, w1, w2, w3, b0, b1, b2, b3, o0, o1, o2, o3):
    sc = _silu(c_ref[...]).astype(BF16)
    for w, b, o in ((w0, b0, o0), (w1, b1, o1), (w2, b2, o2), (w3, b3, o3)):
        o[...] = _dot(sc, w[...].astype(BF16)) + b[...]


def _modulation(c, ws, bs, tn=512):
    B, D = c.shape
    N = ws[0].shape[1]
    w_spec = pl.BlockSpec((D, tn), lambda j: (0, j))
    b_spec = pl.BlockSpec((1, tn), lambda j: (0, j))
    o_spec = pl.BlockSpec((B, tn), lambda j: (0, j))
    return pl.pallas_call(
        _modulation_kernel,
        grid=(N // tn,),
        in_specs=[pl.BlockSpec((B, D), lambda j: (0, 0))] + [w_spec] * 4 + [b_spec] * 4,
        out_specs=[o_spec] * 4,
        out_shape=[jax.ShapeDtypeStruct((B, N), F32)] * 4,
        compiler_params=_cparams(("arbitrary",)),
        name="modulation",
    )(c, *ws, *[b.reshape(1, N) for b in bs])


def _cast_specs(arrays, n_steps, step_of):
    in_specs, out_specs, out_shape = [], [], []
    for a in arrays:
        rows, cols = a.shape
        spec = pl.BlockSpec((rows // n_steps, cols), lambda *ids: (step_of(*ids), 0))
        in_specs.append(spec)
        out_specs.append(spec)
        out_shape.append(jax.ShapeDtypeStruct(a.shape, BF16))
    return in_specs, out_specs, out_shape


def _cast_slabs(src_refs, dst_refs):
    for src, dst in zip(src_refs, dst_refs):
        dst[...] = src[...].astype(BF16)


def _conv_layer_kernel(*refs, ec, final, n_cast):
    n_in = 9 if final else 8
    (x_ref, shift_ref, scale_ref, gate_ref, w_in_ref, cw_ref, cb_ref, w_out_ref) = refs[:8]
    g_ref = refs[8] if final else None
    cast_src = refs[n_in:n_in + n_cast]
    o_ref = refs[n_in + n_cast]
    cast_dst = refs[n_in + n_cast + 1:n_in + 2 * n_cast + 1]
    carry_ref, vs_ref = refs[n_in + 2 * n_cast + 1:]
    tm, D = x_ref.shape
    E = w_out_ref.shape[0]

    @pl.when(pl.program_id(1) == 0)
    def _():
        carry_ref[...] = jnp.zeros_like(carry_ref)

    _cast_slabs(cast_src, cast_dst)
    x = x_ref[...]
    h = _mod_norm(x, scale_ref[...], shift_ref[...]).astype(BF16)
    n_chunks = E // ec

    def in_proj(j):
        c0 = j * ec
        return tuple(_dot(h, w_in_ref[:, k * E + c0:k * E + c0 + ec]) for k in range(4))

    def gated_conv(j, proj):
        u, gc, gb, z = proj
        c0 = j * ec
        v = gc * u
        vs_ref[0:SUBLANES, :] = carry_ref[j]
        vs_ref[SUBLANES:SUBLANES + tm, :] = v
        carry_ref[j] = v[tm - SUBLANES:tm, :]
        v1 = vs_ref[SUBLANES - 1:SUBLANES - 1 + tm, :]
        v2 = vs_ref[SUBLANES - 2:SUBLANES - 2 + tm, :]
        cw = cw_ref[:, c0:c0 + ec]
        conv = cw[0:1, :] * v2 + cw[1:2, :] * v1 + cw[2:3, :] * v + cb_ref[:, c0:c0 + ec]
        return (gb * conv * _silu(z)).astype(BF16)

    acc = jnp.zeros((tm, D), F32)
    proj = in_proj(0)
    for j in range(n_chunks):
        nxt = in_proj(j + 1) if j + 1 < n_chunks else None
        y = gated_conv(j, proj)
        acc = acc + _dot(y, w_out_ref[j * ec:(j + 1) * ec, :])
        proj = nxt
    out = x + gate_ref[...] * acc
    if final:
        ms = jnp.mean(out * out, axis=-1, keepdims=True)
        out = out * lax.rsqrt(ms + EPS) * g_ref[...]
    o_ref[...] = out


def _conv_layer(x, mod, w_in, conv_w, conv_b, w_out, *, final_g=None, cast=(), tm=1024, ec=256):
    B, S, D = x.shape
    x_spec = pl.BlockSpec((None, tm, D), lambda b, i: (b, i, 0))
    E = w_out.shape[0]
    final = final_g is not None
    mod_specs = [pl.BlockSpec((None, 1, D), functools.partial(lambda b, i, k: (b, 0, k), k=k))
                 for k in range(3)]
    in_specs = [x_spec] + mod_specs + [_resident(w_in.shape), _resident(conv_w.shape),
                                       _resident((1, E)), _resident(w_out.shape)]
    args = [x, mod, mod, mod, w_in, conv_w, conv_b.reshape(1, E), w_out]
    if final:
        in_specs.append(_resident((1, D)))
        args.append(final_g.reshape(1, D))
    n_tiles = S // tm
    cast_in, cast_out, cast_shape = _cast_specs(cast, B * n_tiles, lambda b, i: b * n_tiles + i)
    res = pl.pallas_call(
        functools.partial(_conv_layer_kernel, ec=ec, final=final, n_cast=len(cast)),
        grid=(B, n_tiles),
        in_specs=in_specs + cast_in,
        out_specs=[pl.BlockSpec((None, tm, D), lambda b, i: (b, i, 0))] + cast_out,
        out_shape=[jax.ShapeDtypeStruct((B, S, D), F32)] + cast_shape,
        scratch_shapes=[pltpu.VMEM((E // ec, SUBLANES, ec), F32),
                        pltpu.VMEM((SUBLANES + tm, ec), F32)],
        compiler_params=_cparams(("arbitrary", "arbitrary")),
        name="conv_layer_final" if final else "conv_layer",
    )(*args, *cast)
    return res if cast else res[0]


def _t5_bucket(dist):
    max_exact = REL_BUCKETS // 2
    d = np.maximum(dist, 0)
    ratio = np.log(np.maximum(d, 1) / max_exact) / math.log(REL_MAX_DIST / max_exact)
    large = np.minimum(max_exact + (ratio * (REL_BUCKETS - max_exact)).astype(np.int64), REL_BUCKETS - 1)
    return np.where(d < max_exact, d, large).astype(np.int32)


def _band_structure(window, dil):
    wsub = window // dil
    qi = np.arange(ATT_BLOCK)[:, None]
    ki = np.arange(2 * ATT_BLOCK)[None, :]
    dist = ATT_BLOCK + qi - ki
    in_band = (dist >= 0) & (dist <= wsub)
    first = in_band & (ki >= ATT_BLOCK)
    return _t5_bucket(dist * dil), np.stack([in_band, first]).astype(np.float32)


def _qkv_kernel(x_ref, shift_ref, scale_ref, w_ref, *refs, dils, n_cast):
    n_out = 3 * len(dils)
    cast_src, out_refs = refs[:n_cast], refs[n_cast:n_cast + n_out]
    cast_dst, h_ref = refs[n_cast + n_out:-1], refs[-1]
    tn, D = x_ref.shape
    E = out_refs[0].shape[-1]
    _cast_slabs(cast_src, cast_dst)
    h = _mod_norm(x_ref[...], scale_ref[...], shift_ref[...])
    n_col = D // LANES
    for c in range(n_col):
        h_ref[c] = h[:, c * LANES:(c + 1) * LANES]
    for g, dil in enumerate(dils):
        n = tn // dil
        if dil == 1:
            hp = h
        else:
            hp = jnp.concatenate(
                [jnp.concatenate([h_ref[c, pl.ds(r, n, stride=dil), :] for c in range(n_col)], axis=1)
                 for r in range(dil)], axis=0)
        hp = hp.astype(BF16)
        for j in range(3):
            c0 = (3 * g + j) * E
            res = _dot(hp, w_ref[:, c0:c0 + E])
            if j == 0:
                res = res * (LOG2E / math.sqrt(HEAD_DIM))
            out_refs[3 * g + j][...] = res.astype(BF16).reshape(dil, n, E)


def _qkv_proj(x, mod, w_qkv, dils, E, cast=(), tn=512):
    B, S, D = x.shape
    n_tiles = S // tn
    cast_in, cast_out, cast_shape = _cast_specs(cast, B * n_tiles, lambda b, i: b * n_tiles + i)
    mod_specs = [pl.BlockSpec((None, 1, D), functools.partial(lambda b, i, k: (b, 0, k), k=k))
                 for k in range(2)]
    out_specs, out_shape = [], []
    for dil in dils:
        out_specs += [pl.BlockSpec((None, dil, tn // dil, E), lambda b, i: (b, 0, i, 0))] * 3
        out_shape += [jax.ShapeDtypeStruct((B, dil, S // dil, E), BF16)] * 3
    return pl.pallas_call(
        functools.partial(_qkv_kernel, dils=dils, n_cast=len(cast)),
        grid=(B, n_tiles),
        in_specs=[pl.BlockSpec((None, tn, D), lambda b, i: (b, i, 0))] + mod_specs
                 + [_resident(w_qkv.shape)] + cast_in,
        out_specs=out_specs + cast_out,
        out_shape=out_shape + cast_shape,
        scratch_shapes=[pltpu.VMEM((D // LANES, tn, LANES), F32)],
        compiler_params=_cparams(("arbitrary", "arbitrary")),
        name="qkv_proj",
    )(x, mod, mod, w_qkv, *cast)


def _attn_kernel(*refs, nr, nsub, has_prev, stride, n_heads):
    if has_prev:
        (q_ref, k_ref, v_ref, kp_ref, vp_ref, rb_ref, bucket_ref, mask_ref,
         o_ref, m_ref, den_ref, tbl_ref, s_ref, p_ref, stat_ref) = refs
    else:
        (q_ref, k_ref, v_ref, rb_ref, bucket_ref, mask_ref,
         o_ref, m_ref, den_ref, tbl_ref, s_ref, p_ref, stat_ref) = refs
    n_pairs = n_heads // 2
    pair_rows = 2 * ATT_BLOCK
    nk = s_ref.shape[1]
    step = pl.program_id(1)

    @pl.when((pl.program_id(0) == 0) & (step == 0))
    def _():
        def fill(b, carry):
            hit = bucket_ref[...] == b
            for h in range(n_heads):
                rows = slice(h * ATT_BLOCK, (h + 1) * ATT_BLOCK)
                s_ref[rows, :] = jnp.where(hit, rb_ref[b, h] * LOG2E, s_ref[rows, :])
            return carry
        s_ref[...] = jnp.zeros_like(s_ref)
        lax.fori_loop(0, REL_BUCKETS, fill, 0)
        for t in range(tbl_ref.shape[0]):
            for h in range(n_heads):
                rows = slice(h * ATT_BLOCK, (h + 1) * ATT_BLOCK)
                tbl_ref[t, rows, :] = jnp.where(mask_ref[t] > 0.0, s_ref[rows, :], NEG_INF)
        stat_ref[0] = jnp.zeros((ATT_BLOCK, LANES), F32)
        stat_ref[1] = jnp.ones((ATT_BLOCK, LANES), F32)

    lane = lax.broadcasted_iota(jnp.int32, (ATT_BLOCK, LANES), 1)
    lo = lane < HEAD_DIM

    def block(q_of, k_of, v_of, table, out_rows):
        for hp in range(n_pairs):
            cols = slice(hp * LANES, (hp + 1) * LANES)
            rows = slice(hp * pair_rows, (hp + 1) * pair_rows)
            q2 = q_of(cols)
            zero = jnp.zeros_like(q2)
            q_pair = jnp.concatenate([jnp.where(lo, q2, zero), jnp.where(lo, zero, q2)], axis=0)
            s_ref[rows, :] = lax.dot_general(q_pair, k_of(cols), (((1,), (1,)), ((), ())),
                                             preferred_element_type=F32)
        head_rows = [slice(h * ATT_BLOCK, (h + 1) * ATT_BLOCK) for h in range(n_heads)]
        if not has_prev:
            maxima = []
            for rows in head_rows:
                s = s_ref[rows, :] + tbl_ref[table, rows, :]
                s_ref[rows, :] = s
                maxima.append(jnp.max(s, axis=-1, keepdims=True))
        for h, rows in enumerate(head_rows):
            if has_prev:
                s = s_ref[rows, :] + tbl_ref[table, rows, :]
                m = jnp.max(s, axis=-1, keepdims=True)
            else:
                s, m = s_ref[rows, :], maxima[h]
            p = jnp.exp2(s - m)
            den = jnp.sum(p, axis=-1, keepdims=True)
            p_ref[rows, :] = p.astype(BF16)
            stat_ref[0, :, h:h + 1] = m
            stat_ref[1, :, h:h + 1] = den
        m_ref[out_rows, :] = stat_ref[0]
        den_ref[out_rows, :] = stat_ref[1]
        for hp in range(n_pairs):
            cols = slice(hp * LANES, (hp + 1) * LANES)
            rows = slice(hp * pair_rows, (hp + 1) * pair_rows)
            res = _dot(p_ref[rows, :], v_of(cols))
            o_pair = jnp.where(lo, res[:ATT_BLOCK], res[ATT_BLOCK:])
            bits = lax.bitcast_convert_type(o_pair.astype(BF16).astype(F32), jnp.uint32)
            if hp % 2 == 0:
                high = bits
            else:
                o_ref[hp // 2, out_rows, :] = high | (bits >> 16)

    for r in range(nr):
        for j in range(nsub):
            row = j * ATT_BLOCK
            q_of = lambda cols, r=r, row=row: q_ref[r, row:row + ATT_BLOCK, cols]
            if has_prev:
                out_rows = _rows(row * stride + r, ATT_BLOCK, stride)
                if j == 0:
                    k_of = lambda cols, r=r: jnp.concatenate([kp_ref[r, :, cols], k_ref[r, 0:ATT_BLOCK, cols]], axis=0)
                    v_of = lambda cols, r=r: jnp.concatenate([vp_ref[r, :, cols], v_ref[r, 0:ATT_BLOCK, cols]], axis=0)
                    table = jnp.where(step == 0, 1, 0)
                else:
                    k_of = lambda cols, r=r, row=row: k_ref[r, row - ATT_BLOCK:row + ATT_BLOCK, cols]
                    v_of = lambda cols, r=r, row=row: v_ref[r, row - ATT_BLOCK:row + ATT_BLOCK, cols]
                    table = 0
            else:
                out_rows = _rows(step * nr + r, ATT_BLOCK, stride)
                k_of = lambda cols, r=r: k_ref[r, :, cols]
                v_of = lambda cols, r=r: v_ref[r, :, cols]
                table = 0
            block(q_of, k_of, v_of, table, out_rows)


def _attention(q, k, v, rel_bias_g, window, dil, tokens_per_step=1024):
    B, _, L, E = q.shape
    S = L * dil
    H = E // HEAD_DIM
    bucket, masks = _band_structure(window, dil)
    has_prev = L > ATT_BLOCK
    if has_prev:
        nr = dil
        nsub = tokens_per_step // (dil * ATT_BLOCK)
        tq = nsub * ATT_BLOCK
        n_steps = L // tq
        nk = 2 * ATT_BLOCK
        qkv_spec = pl.BlockSpec((None, nr, tq, E), lambda b, i: (b, 0, i, 0))
        prev_spec = pl.BlockSpec((None, nr, ATT_BLOCK, E),
                                 lambda b, i: (b, 0, jnp.maximum(i * nsub - 1, 0), 0))
        in_specs = [qkv_spec] * 3 + [prev_spec] * 2
        args = [q, k, v, k, v]
        out_block_rows = tokens_per_step
        out_step = lambda i: i
    else:
        nr = tokens_per_step // ATT_BLOCK
        nsub = 1
        n_steps = dil // nr
        nk = ATT_BLOCK
        bucket, masks = bucket[:, ATT_BLOCK:], masks[1:, :, ATT_BLOCK:]
        qkv_spec = pl.BlockSpec((None, nr, ATT_BLOCK, E), lambda b, i: (b, i, 0, 0))
        in_specs = [qkv_spec] * 3
        args = [q, k, v]
        out_block_rows = S
        out_step = lambda i: 0
    in_specs += [pl.BlockSpec(memory_space=pltpu.SMEM), _resident(bucket.shape), _resident(masks.shape)]
    args += [rel_bias_g, jnp.asarray(bucket), jnp.asarray(masks)]
    stat_spec = pl.BlockSpec((None, out_block_rows, LANES), lambda b, i: (b, out_step(i), 0))
    stat_shape = jax.ShapeDtypeStruct((B, S, LANES), F32)
    return pl.pallas_call(
        functools.partial(_attn_kernel, nr=nr, nsub=nsub, has_prev=has_prev, stride=dil, n_heads=H),
        grid=(B, n_steps),
        in_specs=in_specs,
        out_specs=[pl.BlockSpec((None, E // LANES // 2, out_block_rows, LANES),
                                lambda b, i: (b, 0, out_step(i), 0)),
                   stat_spec, stat_spec],
        out_shape=[jax.ShapeDtypeStruct((B, E // LANES // 2, S, LANES), jnp.uint32),
                   stat_shape, stat_shape],
        scratch_shapes=[pltpu.VMEM((masks.shape[0], H * ATT_BLOCK, nk), F32),
                        pltpu.VMEM((H * ATT_BLOCK, nk), F32),
                        pltpu.VMEM((H * ATT_BLOCK, nk), BF16),
                        pltpu.VMEM((2, ATT_BLOCK, LANES), F32)],
        compiler_params=_cparams(("arbitrary", "arbitrary")),
        name=f"attention_d{dil}",
    )(*args)


def _attn_out_kernel(x_ref, shift_ref, scale_ref, gate_ref, o0_ref, o1_ref, o2_ref,
                     m0_ref, m1_ref, m2_ref, d0_ref, d1_ref, d2_ref, wz_ref, w_out_ref, expand_ref,
                     out_ref, o_buf, o_sem):
    tm, D = x_ref.shape
    E = wz_ref.shape[1]
    n_heads = E // HEAD_DIM

    o_hbm = (o0_ref, o1_ref, o2_ref)
    n_tiles = pl.num_programs(1)
    n_steps = pl.num_programs(0) * n_tiles
    step = pl.program_id(0) * n_tiles + pl.program_id(1)

    def o_copy(g, t):
        rows = pl.ds(pl.multiple_of((t % n_tiles) * tm, tm), tm)
        return pltpu.make_async_copy(o_hbm[g].at[t // n_tiles, :, rows, :],
                                     o_buf.at[g, t % RING_SLOTS], o_sem.at[g, t % RING_SLOTS])

    @pl.when(step == 0)
    def _():
        for t in range(RING_SLOTS - 1):
            for g in range(len(o_hbm)):
                o_copy(g, t).start()

    @pl.when(step + RING_SLOTS - 1 < n_steps)
    def _():
        for g in range(len(o_hbm)):
            o_copy(g, step + RING_SLOTS - 1).start()

    for g in range(len(o_hbm)):
        o_copy(g, step).wait()
    slot = step % RING_SLOTS

    x = x_ref[...]
    h = _mod_norm(x, scale_ref[...], shift_ref[...]).astype(BF16)
    z = _dot(h, wz_ref[...])
    m0, m1, m2 = m0_ref[...], m1_ref[...], m2_ref[...]
    m = jnp.maximum(jnp.maximum(m0, m1), m2)
    e0, e1, e2 = jnp.exp2(m0 - m), jnp.exp2(m1 - m), jnp.exp2(m2 - m)
    den = e0 * d0_ref[...] + e1 * d1_ref[...] + e2 * d2_ref[...]
    head_lane = lax.broadcasted_iota(jnp.int32, (tm, LANES), 1) < n_heads
    packed = None
    for g, e in enumerate((e0, e1, e2)):
        w = jnp.where(head_lane, e / den, 0.0)
        hi = w.astype(BF16).astype(F32)
        for part, val in enumerate((hi, w - hi)):
            shift = (2 * g + part) * n_heads
            val = pltpu.roll(val, shift, axis=1) if shift else val
            packed = val if packed is None else packed + val
    packed = packed.astype(BF16)
    o = None
    for g in range(len(o_hbm)):
        wexp = _dot(packed, expand_ref[:, g * E:(g + 1) * E])
        parts = []
        for w in range(o_buf.shape[2]):
            word = o_buf[g, slot, w]
            parts.append(lax.bitcast_convert_type(word & jnp.uint32(0xFFFF0000), F32))
            parts.append(lax.bitcast_convert_type(word << 16, F32))
        og = jnp.concatenate(parts, axis=-1)
        term = wexp * og
        o = term if o is None else o + term
    y = (o * _silu(z)).astype(BF16)
    out_ref[...] = x + gate_ref[...] * _dot(y, w_out_ref[...])


def _attn_out(x, mod, os_, ms, dens, w_in, w_out, tm=1024):
    B, S, D = x.shape
    E = w_out.shape[0]
    z_block = w_in.shape[1] // E - 1
    tile = lambda w: pl.BlockSpec((None, tm, w), lambda b, i: (b, i, 0))
    mod_specs = [pl.BlockSpec((None, 1, D), functools.partial(lambda b, i, k: (b, 0, k), k=k))
                 for k in range(3)]
    n_groups, H = len(os_), E // HEAD_DIM
    expand = np.zeros((LANES, n_groups * E), np.float32)
    for g in range(n_groups):
        for part in range(2):
            for hd in range(H):
                expand[(2 * g + part) * H + hd, g * E + hd * HEAD_DIM:g * E + (hd + 1) * HEAD_DIM] = 1.0
    expand = jnp.asarray(expand, BF16)
    return pl.pallas_call(
        _attn_out_kernel,
        grid=(B, S // tm),
        in_specs=[tile(D)] + mod_specs
                 + [pl.BlockSpec(memory_space=pl.ANY)] * 3
                 + [tile(LANES)] * 6
                 + [pl.BlockSpec((D, E), lambda b, i: (0, z_block), pipeline_mode=pl.Buffered(1)),
                    _resident(w_out.shape), _resident(expand.shape)],
        out_specs=tile(D),
        out_shape=jax.ShapeDtypeStruct((B, S, D), F32),
        scratch_shapes=[pltpu.VMEM((n_groups, RING_SLOTS, E // LANES // 2, tm, LANES), jnp.uint32),
                        pltpu.SemaphoreType.DMA((n_groups, RING_SLOTS))],
        compiler_params=_cparams(("arbitrary", "arbitrary")),
        name="attention_out",
    )(x, mod, mod, mod, *os_, *ms, *dens, w_in, w_out, expand)


def _s5_kernel(x_ref, shift_ref, scale_ref, gate_ref, w_in_ref, bbr_ref, bbi_ref, ar_ref, ai_ref,
               cr_ref, ci_ref, dskip_ref, w_glu_ref, b_glu_ref, w_out_ref,
               o_ref, state_ref, sr_ref, si_ref, *, scan_lanes):
    B, ts, D = x_ref.shape
    R = ts * B
    n_chunks, cn, ck = bbr_ref.shape
    E = n_chunks * ck
    dot_t = lambda a, b: lax.dot_general(a, b, (((1,), (1,)), ((), ())), preferred_element_type=F32)

    @pl.when(pl.program_id(0) == 0)
    def _():
        state_ref[...] = jnp.zeros_like(state_ref)

    x3 = jnp.swapaxes(x_ref[...], 0, 1)
    ms = jnp.mean(x3 * x3, axis=-1, keepdims=True)
    h3 = (x3 * lax.rsqrt(ms + EPS)) * (1.0 + scale_ref[...][None]) + shift_ref[...][None]
    h = h3.reshape(R, D).astype(BF16)
    u_parts = [None] * n_chunks
    z_parts = [None] * n_chunks

    def w_in_cols(first):
        return _dot(h, w_in_ref[:, first:first + ck])

    def in_proj(c):
        u_parts[c] = w_in_cols(c * ck)
        uc = u_parts[c].astype(BF16)
        sr_ref[c] = dot_t(uc, bbr_ref[c])
        si_ref[c] = dot_t(uc, bbi_ref[c])

    def scan(c):
        for l0 in range(0, cn, scan_lanes):
            ls = slice(l0, l0 + scan_lanes)
            gl = slice(c * cn + l0, c * cn + l0 + scan_lanes)
            ar, ai = ar_ref[:, gl], ai_ref[:, gl]
            pr, pi = state_ref[0, :, gl], state_ref[1, :, gl]
            for t in range(ts):
                rows = slice(t * B, (t + 1) * B)
                nr = ar * pr - ai * pi + sr_ref[c, rows, ls]
                ni = ar * pi + ai * pr + si_ref[c, rows, ls]
                sr_ref[c, rows, ls] = nr
                si_ref[c, rows, ls] = ni
                pr, pi = nr, ni
            state_ref[0, :, gl] = pr
            state_ref[1, :, gl] = pi

    def out_proj(c):
        cols = slice(c * ck, (c + 1) * ck)
        y = (dot_t(sr_ref[c].astype(BF16), cr_ref[c]) - dot_t(si_ref[c].astype(BF16), ci_ref[c])
             + dskip_ref[:, cols] * u_parts[c])
        z_parts[c] = w_in_cols(E + c * ck)
        return y * (0.5 * (1.0 + jnp.tanh(math.sqrt(2.0 / math.pi) * (y + 0.044715 * (y * y * y)))))

    g_parts = []
    in_proj(0)
    for c in range(n_chunks):
        if c + 1 < n_chunks:
            in_proj(c + 1)
        scan(c)
        if c >= 1:
            g_parts.append(out_proj(c - 1))
    g_parts.append(out_proj(n_chunks - 1))
    g = jnp.concatenate(g_parts, axis=-1)
    z = jnp.concatenate(z_parts, axis=-1)
    y = g * _sigmoid(_dot(g.astype(BF16), w_glu_ref[...]) + b_glu_ref[...])
    y = (y * _silu(z)).astype(BF16)
    out = _dot(y, w_out_ref[...]).reshape(ts, B, D)
    o_ref[...] = jnp.swapaxes(x3 + gate_ref[...][None] * out, 0, 1)


def _s5_layer(x, mod, w_in, bbr, bbi, ar, ai, cr, ci, d_skip, w_glu, b_glu, w_out, ts=64, scan_lanes=512):
    B, S, D = x.shape
    E = w_glu.shape[0]
    N = ar.shape[1]
    mod_specs = [pl.BlockSpec((B, D), functools.partial(lambda i, k: (0, k), k=k)) for k in range(3)]
    weights = [w_in, bbr, bbi, ar, ai, cr, ci, d_skip.reshape(1, E), w_glu, b_glu.reshape(1, E), w_out]
    return pl.pallas_call(
        functools.partial(_s5_kernel, scan_lanes=scan_lanes),
        grid=(S // ts,),
        in_specs=[pl.BlockSpec((B, ts, D), lambda i: (0, i, 0))] + mod_specs
                 + [_resident(w.shape) for w in weights],
        out_specs=pl.BlockSpec((B, ts, D), lambda i: (0, i, 0)),
        out_shape=jax.ShapeDtypeStruct((B, S, D), F32),
        scratch_shapes=[pltpu.VMEM((2, B, N), F32),
                        pltpu.VMEM((bbr.shape[0], ts * B, bbr.shape[1]), F32),
                        pltpu.VMEM((bbr.shape[0], ts * B, bbr.shape[1]), F32)],
        compiler_params=_cparams(("arbitrary",)),
        name="s5_layer",
    )(x, mod, mod, mod, *weights)


def _s5_params(log_dt, lambda_re, lambda_im, b_re, b_im, c_re, c_im, B, groups_per_chunk=16):
    G, P = lambda_re.shape
    K = b_re.shape[2]
    dt = jnp.exp(log_dt.astype(F32))[:, None]
    lr = lambda_re.astype(F32)
    li = lambda_im.astype(F32)
    mag = jnp.exp(lr * dt)
    ar = mag * jnp.cos(li * dt)
    ai = mag * jnp.sin(li * dt)
    inv = 1.0 / (lr * lr + li * li)
    qr = ((ar - 1.0) * lr + ai * li) * inv
    qi = (ai * lr - (ar - 1.0) * li) * inv
    br = b_re.astype(F32)
    bi = b_im.astype(F32)
    bbr = qr[..., None] * br - qi[..., None] * bi
    bbi = qr[..., None] * bi + qi[..., None] * br
    gc = groups_per_chunk
    nc = G // gc

    def block_diag(m):
        a, b = m.shape[1:]
        rows = m.astype(BF16).reshape(nc, gc * a, b)
        on_diag = np.kron(np.eye(gc, dtype=bool), np.ones((a, b), dtype=bool))
        return jnp.where(on_diag, jnp.tile(rows, (1, 1, gc)), jnp.zeros((), BF16))

    bcast = lambda a: jnp.broadcast_to(a.reshape(1, G * P), (B, G * P))
    return (block_diag(bbr), block_diag(bbi), bcast(ar), bcast(ai),
            block_diag(c_re.astype(F32)), block_diag(c_im.astype(F32)))


def kernel(x, c, rel_bias, final_g, l0_ada_w, l0_ada_b, l0_w_in, l0_conv_w, l0_conv_b, l0_w_out, l1_ada_w, l1_ada_b, l1_w_in, l1_w_out, l2_ada_w, l2_ada_b, l2_w_in, l2_log_dt, l2_lambda_re, l2_lambda_im, l2_b_re, l2_b_im, l2_c_re, l2_c_im, l2_d_skip, l2_w_glu, l2_b_glu, l2_w_out, l3_ada_w, l3_ada_b, l3_w_in, l3_conv_w, l3_conv_b, l3_w_out):
    B, S, D = x.shape
    assert B == SUBLANES, "the S5 scan keeps the batch on the sublane axis"
    mods = _modulation(c, (l0_ada_w, l1_ada_w, l2_ada_w, l3_ada_w), (l0_ada_b, l1_ada_b, l2_ada_b, l3_ada_b))
    mod3 = [m.reshape(B, 1, 3 * D) for m in mods]

    x, w1_in, w1_out = _conv_layer(x, mod3[0], l0_w_in.astype(BF16), l0_conv_w, l0_conv_b,
                                   l0_w_out.astype(BF16), cast=(l1_w_in, l1_w_out))

    dils = tuple(dil for _, dil in DILATION_PATTERNS)
    n_qkv = 3 * len(dils)
    E = l1_w_out.shape[0]
    H = E // HEAD_DIM
    res = _qkv_proj(x, mod3[1], w1_in, dils, E, cast=(l2_w_in, l2_w_glu, l2_w_out, l3_w_in, l3_w_out))
    qkv, (w2_in, w2_glu, w2_out, w3_in, w3_out) = res[:n_qkv], res[n_qkv:]
    os_, ms, dens = [], [], []
    for g, (window, dil) in enumerate(DILATION_PATTERNS):
        o, m, den = _attention(*qkv[3 * g:3 * g + 3], rel_bias[:, g * H:(g + 1) * H], window, dil)
        os_.append(o)
        ms.append(m)
        dens.append(den)
    x = _attn_out(x, mod3[1], os_, ms, dens, w1_in, w1_out)

    ssm = _s5_params(l2_log_dt, l2_lambda_re, l2_lambda_im, l2_b_re, l2_b_im, l2_c_re, l2_c_im, B)
    x = _s5_layer(x, mods[2], w2_in, *ssm, l2_d_skip, w2_glu, l2_b_glu, w2_out)

    return _conv_layer(x, mod3[3], w3_in, l3_conv_w, l3_conv_b, w3_out, final_g=final_g)
```
